```python
import math
import jax, jax.numpy as jnp
from jax import lax
import numpy as np

D_MODEL = 1024
BATCH = 2
SEQ = 16384
DEPTH = 2
DEC_BATCH = 16
DEC_SEQ = 2048
PAST_LEN = 128

GRID_W = 64
HEAD_DIM = 64
NA_HEADS = 4
GDN_HEADS = 6
MLSTM_HEADS = 6
NA_W = NA_HEADS * HEAD_DIM
GDN_W = GDN_HEADS * HEAD_DIM
MLSTM_W = MLSTM_HEADS * HEAD_DIM
D_MIX = NA_W + GDN_W + MLSTM_W
NA_WIN_ROWS = 8
NA_WIN_COLS = 16
NA_Q_BLOCK = 16
NA_K_BLOCK = NA_Q_BLOCK + NA_WIN_COLS
GDN_CONV = 5
CHUNK = 64
M_INIT = -1e30
D_FF = 2816
EPS = 1e-6
SPLITS = (3 * NA_W,
          3 * GDN_W,
          GDN_W,
          2 * GDN_HEADS,
          2 * GDN_HEADS,
          3 * MLSTM_W,
          MLSTM_W,
          2 * MLSTM_HEADS,
          2 * MLSTM_HEADS)
P_IN = sum(SPLITS)

kernel_name = "hybrid_na_gdn_mlstm_macaron_encoder"


def rms_norm(x, gain):
    xf = x.astype(jnp.float32)
    y = xf * lax.rsqrt(jnp.mean(xf * xf, axis=-1, keepdims=True) + EPS)
    return (y * gain.astype(jnp.float32)).astype(x.dtype)


def l2_normalize(x):
    return x * lax.rsqrt(jnp.sum(x * x, axis=-1, keepdims=True) + EPS)


def swiglu(x, w_in, w_out):
    gate, up = jnp.split(x @ w_in, 2, axis=-1)
    return (jax.nn.silu(gate) * up) @ w_out


def flip_seq(t):
    return jnp.flip(t, axis=2)


def centred_depthwise_conv(x, w):
    c = x.shape[-1]
    k = w.shape[0]
    return lax.conv_general_dilated(x, w[:, None, :], window_strides=(1,),
                                    padding=[(k // 2, k // 2)],
                                    dimension_numbers=('NWC', 'WIO', 'NWC'),
                                    feature_group_count=c)


def neighbourhood_attention(q, k, v, rpb):
    b, l, h, dh = q.shape
    rows = l // GRID_W
    wr = min(NA_WIN_ROWS, rows)
    qg = q.reshape(b, rows, GRID_W, h, dh)
    kg = k.reshape(b, rows, GRID_W, h, dh)
    vg = v.reshape(b, rows, GRID_W, h, dh)
    r = np.arange(rows)
    row_start = np.clip(r - wr // 2, 0, rows - wr)
    row_idx = row_start[:, None] + np.arange(wr)[None, :]
    k_rows = kg[:, row_idx]
    v_rows = vg[:, row_idx]
    dr = row_idx - r[:, None] + NA_WIN_ROWS - 1
    scale = dh ** -0.5
    outs = []
    for j in range(GRID_W // NA_Q_BLOCK):
        q_cols = np.arange(j * NA_Q_BLOCK, (j + 1) * NA_Q_BLOCK)
        k0 = min(max(j * NA_Q_BLOCK - NA_WIN_COLS // 2, 0), GRID_W - NA_K_BLOCK)
        k_cols = k0 + np.arange(NA_K_BLOCK)
        c_start = np.clip(q_cols - NA_WIN_COLS // 2, 0, GRID_W - NA_WIN_COLS)
        in_win = (k_cols[None, :] >= c_start[:, None]) & (k_cols[None, :] < c_start[:, None] + NA_WIN_COLS)
        dc = np.clip(k_cols[None, :] - q_cols[:, None], 1 - NA_WIN_COLS, NA_WIN_COLS - 1) + NA_WIN_COLS - 1
        kb = k_rows[:, :, :, k0:k0 + NA_K_BLOCK]
        vb = v_rows[:, :, :, k0:k0 + NA_K_BLOCK]
        qb = qg[:, :, j * NA_Q_BLOCK:(j + 1) * NA_Q_BLOCK]
        s = jnp.einsum('brqhd,brwkhd->bhrqwk', qb, kb).astype(jnp.float32) * scale
        bias = rpb[:, dr[:, None, :, None], dc[None, :, None, :]].astype(jnp.float32)
        s = jnp.where(in_win[:, None, :], s + bias[None], -jnp.inf)
        p = jax.nn.softmax(s.reshape(b, h, rows, NA_Q_BLOCK, wr * NA_K_BLOCK), axis=-1)
        p = p.reshape(s.shape).astype(v.dtype)
        outs.append(jnp.einsum('bhrqwk,brwkhd->brqhd', p, vb))
    return jnp.concatenate(outs, axis=2).reshape(b, l, h, dh)


def gated_delta_chunked(q, k, v, g, beta):
    b, h, l, dk = q.shape
    dv = v.shape[-1]
    n = l // CHUNK
    q = q.reshape(b, h, n, CHUNK, dk)
    k = k.reshape(b, h, n, CHUNK, dk)
    v = v.reshape(b, h, n, CHUNK, dv)
    g = g.reshape(b, h, n, CHUNK)
    beta = beta.reshape(b, h, n, CHUNK)
    incl = np.tril(np.ones((CHUNK, CHUNK), bool))
    strict = np.tril(np.ones((CHUNK, CHUNK), bool), -1)
    g_cum = jnp.cumsum(g, axis=-1)
    decay = jnp.exp(jnp.where(incl, g_cum[..., :, None] - g_cum[..., None, :], -jnp.inf))
    k_beta = k * beta[..., None]
    a = jnp.where(strict, jnp.einsum('bhnid,bhnjd->bhnij', k_beta, k) * decay, 0.0)
    eye = jnp.eye(CHUNK, dtype=q.dtype)
    rhs = jnp.concatenate([v * beta[..., None], k_beta * jnp.exp(g_cum)[..., None]], axis=-1)
    sol = lax.linalg.triangular_solve(a + eye, rhs, left_side=True, lower=True, unit_diagonal=True)
    u, w = sol[..., :dv], sol[..., dv:]
    k_dec = k * jnp.exp(g_cum[..., -1:] - g_cum)[..., None]
    p_kv = jnp.einsum('bhncd,bhnce->bhnde', k_dec, u)
    p_kw = jnp.einsum('bhncd,bhnce->bhnde', k_dec, w)
    g_last = jnp.exp(g_cum[..., -1])

    def step(s, xs):
        pkv, pkw, gl = xs
        return gl[..., None, None] * s + pkv - pkw @ s, s

    xs = (jnp.moveaxis(p_kv, 2, 0), jnp.moveaxis(p_kw, 2, 0), jnp.moveaxis(g_last, 2, 0))
    _, s_prev = lax.scan(step, jnp.zeros((b, h, dk, dv), q.dtype), xs)
    s_prev = jnp.moveaxis(s_prev, 0, 2)
    v_new = u - jnp.einsum('bhncd,bhnde->bhnce', w, s_prev)
    o = (jnp.einsum('bhncd,bhnde->bhnce', q * jnp.exp(g_cum)[..., None], s_prev)
         + jnp.einsum('bhnij,bhnje->bhnie', jnp.einsum('bhnid,bhnjd->bhnij', q, k) * decay, v_new))
    return o.reshape(b, h, l, dv)


def mlstm_chunked(q, k, v, log_i, log_f):
    b, h, l, dk = q.shape
    dv = v.shape[-1]
    n = l // CHUNK
    q = q.reshape(b, h, n, CHUNK, dk)
    k = k.reshape(b, h, n, CHUNK, dk)
    v = v.reshape(b, h, n, CHUNK, dv)
    log_i = log_i.reshape(b, h, n, CHUNK)
    log_f = log_f.reshape(b, h, n, CHUNK)
    incl = np.tril(np.ones((CHUNK, CHUNK), bool))
    f_cum = jnp.cumsum(log_f, axis=-1)
    d = jnp.where(incl, f_cum[..., :, None] - f_cum[..., None, :] + log_i[..., None, :], -jnp.inf)
    m_intra = jnp.max(d, axis=-1)
    g_end = f_cum[..., -1:] - f_cum + log_i
    m_loc = jnp.max(g_end, axis=-1)
    w_end = jnp.exp(g_end - m_loc[..., None])
    c_sum = jnp.einsum('bhnc,bhncd,bhnce->bhnde', w_end, k, v)
    n_sum = jnp.einsum('bhnc,bhncd->bhnd', w_end, k)
    f_last = f_cum[..., -1]

    def step(carry, xs):
        c_st, n_st, m_st = carry
        cs, ns, fl, ml = xs
        m_new = jnp.maximum(fl + m_st, ml)
        a = jnp.exp(fl + m_st - m_new)
        bb = jnp.exp(ml - m_new)
        new = (a[..., None, None] * c_st + bb[..., None, None] * cs,
               a[..., None] * n_st + bb[..., None] * ns,
               m_new)
        return new, (c_st, n_st, m_st)

    init = (jnp.zeros((b, h, dk, dv), q.dtype), jnp.zeros((b, h, dk), q.dtype),
            jnp.full((b, h), M_INIT, q.dtype))
    xs = (jnp.moveaxis(c_sum, 2, 0), jnp.moveaxis(n_sum, 2, 0),
          jnp.moveaxis(f_last, 2, 0), jnp.moveaxis(m_loc, 2, 0))
    _, (c_prev, n_prev, m_prev) = lax.scan(step, init, xs)
    c_prev = jnp.moveaxis(c_prev, 0, 2)
    n_prev = jnp.moveaxis(n_prev, 0, 2)
    m_prev = jnp.moveaxis(m_prev, 0, 2)
    m_inter = f_cum + m_prev[..., None]
    m_t = jnp.maximum(m_inter, m_intra)
    s = jnp.einsum('bhnid,bhnjd->bhnij', q, k) * jnp.exp(d - m_t[..., None])
    inter = jnp.exp(m_inter - m_t)
    num = (jnp.einsum('bhnij,bhnje->bhnie', s, v)
           + inter[..., None] * jnp.einsum('bhnid,bhnde->bhnie', q, c_prev))
    den = jnp.sum(s, axis=-1) + inter * jnp.einsum('bhnid,bhnd->bhni', q, n_prev)
    out = num / jnp.maximum(jnp.abs(den), jnp.exp(-m_t))[..., None]
    return out.reshape(b, h, l, dv)


def hybrid_mixer(h, w_in, w_out, na_rpb, gdn_conv, gdn_a_log, gdn_dt_bias, gdn_norm,
                 mlstm_i_bias, mlstm_f_bias, mlstm_norm):
    b, l, _ = h.shape
    f32 = jnp.float32
    proj = h @ w_in
    offsets = np.cumsum(SPLITS)[:-1].tolist()
    na_qkv, gdn_qkv, gdn_z, gdn_a, gdn_b, ml_qkv, ml_o, ml_i, ml_f = jnp.split(proj, offsets, axis=-1)

    def to_heads(t, nh):
        return t.reshape(b, l, nh, HEAD_DIM).transpose(0, 2, 1, 3)

    na_q, na_k, na_v = [t.reshape(b, l, NA_HEADS, HEAD_DIM) for t in jnp.split(na_qkv, 3, axis=-1)]
    na_out = neighbourhood_attention(na_q, na_k, na_v, na_rpb).reshape(b, l, NA_W)

    gqkv = jax.nn.silu(centred_depthwise_conv(gdn_qkv, gdn_conv)).astype(f32)
    gq, gk, gv = [to_heads(t, GDN_HEADS) for t in jnp.split(gqkv, 3, axis=-1)]
    gq = l2_normalize(gq) * (HEAD_DIM ** -0.5)
    gk = l2_normalize(gk)
    ga = gdn_a.astype(f32).reshape(b, l, 2, GDN_HEADS)
    gb = gdn_b.astype(f32).reshape(b, l, 2, GDN_HEADS)
    log_decay = (-jnp.exp(gdn_a_log.astype(f32)) * jax.nn.softplus(ga + gdn_dt_bias.astype(f32))).transpose(2, 0, 3, 1)
    beta = jax.nn.sigmoid(gb).transpose(2, 0, 3, 1)
    o_fwd = gated_delta_chunked(gq, gk, gv, log_decay[0], beta[0])
    o_bwd = flip_seq(gated_delta_chunked(flip_seq(gq), flip_seq(gk), flip_seq(gv),
                                         flip_seq(log_decay[1]), flip_seq(beta[1])))
    go = (o_fwd + o_bwd).transpose(0, 2, 1, 3)
    gz = gdn_z.astype(f32).reshape(b, l, GDN_HEADS, HEAD_DIM)
    gdn_out = (rms_norm(go, gdn_norm) * jax.nn.silu(gz)).reshape(b, l, GDN_W).astype(h.dtype)

    mq, mk, mv = [to_heads(t.astype(f32), MLSTM_HEADS) for t in jnp.split(ml_qkv, 3, axis=-1)]
    mk = mk * (HEAD_DIM ** -0.5)
    log_i = (ml_i.astype(f32).reshape(b, l, 2, MLSTM_HEADS) + mlstm_i_bias.astype(f32)).transpose(2, 0, 3, 1)
    log_f = jax.nn.log_sigmoid(ml_f.astype(f32).reshape(b, l, 2, MLSTM_HEADS)
                               + mlstm_f_bias.astype(f32)).transpose(2, 0, 3, 1)
    h_fwd = mlstm_chunked(mq, mk, mv, log_i[0], log_f[0])
    h_bwd = flip_seq(mlstm_chunked(flip_seq(mq), flip_seq(mk), flip_seq(mv),
                                   flip_seq(log_i[1]), flip_seq(log_f[1])))
    mh = rms_norm((h_fwd + h_bwd).transpose(0, 2, 1, 3), mlstm_norm).reshape(b, l, MLSTM_W)
    ml_out = (jax.nn.sigmoid(ml_o.astype(f32)) * mh).astype(h.dtype)

    return jnp.concatenate([na_out, gdn_out, ml_out], axis=-1) @ w_out


def encoder_layer(x, ffn1_norm, ffn1_w_in, ffn1_w_out, mix_norm, w_in, w_out, na_rpb, gdn_conv,
                  gdn_a_log, gdn_dt_bias, gdn_norm, mlstm_i_bias, mlstm_f_bias, mlstm_norm,
                  ffn2_norm, ffn2_w_in, ffn2_w_out):
    x = x + 0.5 * swiglu(rms_norm(x, ffn1_norm), ffn1_w_in, ffn1_w_out)
    x = x + hybrid_mixer(rms_norm(x, mix_norm), w_in, w_out, na_rpb, gdn_conv, gdn_a_log,
                         gdn_dt_bias, gdn_norm, mlstm_i_bias, mlstm_f_bias, mlstm_norm)
    x = x + 0.5 * swiglu(rms_norm(x, ffn2_norm), ffn2_w_in, ffn2_w_out)
    return x


def setup_inputs(seed: int = 0) -> dict:
    key = jax.random.key(seed)
    ks = jax.random.split(key, 24)
    f32 = jnp.float32

    def nrm(k, shape, scale):
        return jax.random.normal(k, shape, f32) * scale

    def gain(k, shape):
        return 1.0 + 0.02 * jax.random.normal(k, shape, f32)

    dt = jnp.exp(jax.random.uniform(ks[11], (DEPTH, 2, GDN_HEADS), f32,
                                    minval=math.log(1e-3), maxval=math.log(1e-1)))
    return {
        "x_prompt": jax.random.normal(ks[0], (BATCH, SEQ, D_MODEL), f32),
        "x_sample": jax.random.normal(ks[1], (DEC_BATCH, DEC_SEQ, D_MODEL), f32),
        "ffn1_norm": gain(ks[2], (DEPTH, D_MODEL)),
        "ffn1_w_in": nrm(ks[3], (DEPTH, D_MODEL, 2 * D_FF), D_MODEL ** -0.5),
        "ffn1_w_out": nrm(ks[4], (DEPTH, D_FF, D_MODEL), D_FF ** -0.5),
        "mix_norm": gain(ks[5], (DEPTH, D_MODEL)),
        "w_in": nrm(ks[6], (DEPTH, D_MODEL, P_IN), D_MODEL ** -0.5),
        "w_out": nrm(ks[7], (DEPTH, D_MIX, D_MODEL), D_MIX ** -0.5),
        "na_rpb": nrm(ks[8], (DEPTH, NA_HEADS, 2 * NA_WIN_ROWS - 1, 2 * NA_WIN_COLS - 1), 0.1),
        "gdn_conv": nrm(ks[9], (DEPTH, GDN_CONV, 3 * GDN_W), GDN_CONV ** -0.5),
        "gdn_a_log": jnp.log(jax.random.uniform(ks[10], (DEPTH, 2, GDN_HEADS), f32, minval=1.0, maxval=16.0)),
        "gdn_dt_bias": dt + jnp.log(-jnp.expm1(-dt)),
        "gdn_norm": gain(ks[12], (DEPTH, HEAD_DIM)),
        "mlstm_i_bias": nrm(ks[13], (DEPTH, 2, MLSTM_HEADS), 0.1),
        "mlstm_f_bias": jnp.linspace(3.0, 6.0, MLSTM_HEADS, dtype=f32) + nrm(ks[14], (DEPTH, 2, MLSTM_HEADS), 0.1),
        "mlstm_norm": gain(ks[15], (DEPTH, MLSTM_HEADS, HEAD_DIM)),
        "ffn2_norm": gain(ks[16], (DEPTH, D_MODEL)),
        "ffn2_w_in": nrm(ks[17], (DEPTH, D_MODEL, 2 * D_FF), D_MODEL ** -0.5),
        "ffn2_w_out": nrm(ks[18], (DEPTH, D_FF, D_MODEL), D_FF ** -0.5),
        "final_norm": gain(ks[19], (D_MODEL,)),
    }


def reference(x_prompt, x_sample, ffn1_norm, ffn1_w_in, ffn1_w_out, mix_norm, w_in, w_out, na_rpb,
              gdn_conv, gdn_a_log, gdn_dt_bias, gdn_norm, mlstm_i_bias, mlstm_f_bias, mlstm_norm,
              ffn2_norm, ffn2_w_in, ffn2_w_out, final_norm):
    def trunk(x):
        for i in range(DEPTH):
            x = encoder_layer(x, ffn1_norm[i], ffn1_w_in[i], ffn1_w_out[i], mix_norm[i], w_in[i],
                              w_out[i], na_rpb[i], gdn_conv[i], gdn_a_log[i], gdn_dt_bias[i],
                              gdn_norm[i], mlstm_i_bias[i], mlstm_f_bias[i], mlstm_norm[i],
                              ffn2_norm[i], ffn2_w_in[i], ffn2_w_out[i])
        return rms_norm(x, final_norm)

    y_prompt = trunk(x_prompt)
    y_sample = trunk(x_sample)
    return (y_prompt, y_sample)
```

```python
import functools

import jax
import jax.numpy as jnp
import numpy as np
from jax import lax
from jax.experimental import pallas as pl
from jax.experimental.pallas import tpu as pltpu

D_MODEL = 1024
D_FF = 2816
HEAD_DIM = 64
NA_HEADS = 4
GDN_HEADS = 6
MLSTM_HEADS = 6
NA_W = NA_HEADS * HEAD_DIM
GDN_W = GDN_HEADS * HEAD_DIM
MLSTM_W = MLSTM_HEADS * HEAD_DIM
GRID_W = 64
NA_WIN_ROWS = 8
NA_WIN_COLS = 16
GDN_CONV = 5
CHUNK = 64
M_INIT = -1e30
EPS = 1e-6
NEG = -1e30

LANES = 128
PAIR_W = 2 * HEAD_DIM
N_PAIRS = GDN_HEADS // 2
GATE_W = LANES
GATE_GROUP = 2 * GDN_HEADS
MIX_BLOCK = 512
VMEM_LIMIT = 56 * 1024 * 1024

F32 = jnp.float32
BF16 = jnp.bfloat16


def _bf(x):
    return x.astype(BF16)


def _mm(a, b):
    return jnp.dot(_bf(a), _bf(b), preferred_element_type=F32)


def _mm_nt(a, b):
    return lax.dot_general(_bf(a), _bf(b), (((1,), (1,)), ((), ())), preferred_element_type=F32)


def _mm_tn(a, b):
    return lax.dot_general(_bf(a), _bf(b), (((0,), (0,)), ((), ())), preferred_element_type=F32)


def _split_mm(a, b_exact, parts):
    acc = None
    rem = a
    for _ in range(parts):
        piece = _bf(rem)
        term = jnp.dot(piece, b_exact, preferred_element_type=F32)
        acc = term if acc is None else acc + term
        rem = rem - piece.astype(F32)
    return acc


def _split_mm_left(a_exact, b, parts):
    acc = None
    rem = b
    for _ in range(parts):
        piece = _bf(rem)
        term = jnp.dot(a_exact, piece, preferred_element_type=F32)
        acc = term if acc is None else acc + term
        rem = rem - piece.astype(F32)
    return acc


def _sigmoid(x):
    return 1.0 / (1.0 + jnp.exp(-x))


def _softplus(x):
    return jnp.maximum(x, 0.0) + jnp.log(1.0 + jnp.exp(-jnp.abs(x)))


def _rms(x, gain):
    return x * lax.rsqrt(jnp.mean(x * x, axis=-1, keepdims=True) + EPS) * gain


def _params(*sem):
    return pltpu.CompilerParams(dimension_semantics=sem, vmem_limit_bytes=VMEM_LIMIT)


FFN_TM = 1024
FFN_TF = 256


def _ffn_kernel(x_ref, g_ref, wg_ref, wu_ref, wo_ref, fn_ref, o_ref, xn_ref, acc_ref, *, final):
    j = pl.program_id(1)

    @pl.when(j == 0)
    def _():
        xn_ref[...] = _bf(_rms(x_ref[...], g_ref[...]))
        acc_ref[...] = jnp.zeros_like(acc_ref)

    xn = xn_ref[...]
    gate = jnp.dot(xn, wg_ref[...], preferred_element_type=F32)
    up = jnp.dot(xn, wu_ref[...], preferred_element_type=F32)
    act = gate * _sigmoid(gate) * up
    acc_ref[...] += jnp.dot(_bf(act), wo_ref[...], preferred_element_type=F32)

    @pl.when(j == pl.num_programs(1) - 1)
    def _():
        y = x_ref[...] + 0.5 * acc_ref[...]
        if final:
            y = _rms(y, fn_ref[...])
        o_ref[...] = y


def _ffn(x, gain, w_in, w_out, final_gain, final):
    t = x.shape[0]
    tm = min(FFN_TM, t)
    nf = D_FF // FFN_TF
    return pl.pallas_call(
        functools.partial(_ffn_kernel, final=final),
        grid=(t // tm, nf),
        in_specs=[
            pl.BlockSpec((tm, D_MODEL), lambda i, j: (i, 0)),
            pl.BlockSpec((1, D_MODEL), lambda i, j: (0, 0)),
            pl.BlockSpec((D_MODEL, FFN_TF), lambda i, j: (0, j)),
            pl.BlockSpec((D_MODEL, FFN_TF), lambda i, j: (0, j + nf)),
            pl.BlockSpec((FFN_TF, D_MODEL), lambda i, j: (j, 0)),
            pl.BlockSpec((1, D_MODEL), lambda i, j: (0, 0)),
        ],
        out_specs=pl.BlockSpec((tm, D_MODEL), lambda i, j: (i, 0)),
        out_shape=jax.ShapeDtypeStruct((t, D_MODEL), F32),
        scratch_shapes=[pltpu.VMEM((tm, D_MODEL), BF16), pltpu.VMEM((tm, D_MODEL), F32)],
        compiler_params=_params("parallel", "arbitrary"),
        name="ffn",
    )(x, gain.reshape(1, D_MODEL), w_in, w_in, w_out, final_gain.reshape(1, D_MODEL))


PROJ_TM = 512
PROJ_WIDTHS = (3 * NA_W, 3 * GDN_W, GDN_W, 3 * MLSTM_W, MLSTM_W, GATE_W)
PROJ_DTYPES = (BF16, F32, F32, F32, F32, F32)
P_PERM = sum(PROJ_WIDTHS)


def _permute_w_in(w_in):
    o = np.cumsum([0, 3 * NA_W, 3 * GDN_W, GDN_W, 2 * GDN_HEADS, 2 * GDN_HEADS,
                   3 * MLSTM_W, MLSTM_W, 2 * MLSTM_HEADS, 2 * MLSTM_HEADS])
    n_gates = 4 * 2 * GDN_HEADS
    cols = [w_in[:, o[0]:o[3]], w_in[:, o[5]:o[7]], w_in[:, o[3]:o[5]], w_in[:, o[7]:o[9]],
            jnp.zeros((D_MODEL, GATE_W - n_gates), w_in.dtype)]
    return _bf(jnp.concatenate(cols, axis=1))


def _inproj_kernel(x_ref, g_ref, w_ref, *o_refs):
    xn = _bf(_rms(x_ref[...], g_ref[...]))
    off = 0
    for o_ref, width in zip(o_refs, PROJ_WIDTHS):
        o_ref[...] = jnp.dot(xn, w_ref[:, off:off + width], preferred_element_type=F32).astype(o_ref.dtype)
        off += width


def _inproj(x, gain, w_perm):
    t = x.shape[0]
    tm = min(PROJ_TM, t)
    return pl.pallas_call(
        _inproj_kernel,
        grid=(t // tm,),
        in_specs=[
            pl.BlockSpec((tm, D_MODEL), lambda i: (i, 0)),
            pl.BlockSpec((1, D_MODEL), lambda i: (0, 0)),
            pl.BlockSpec((D_MODEL, P_PERM), lambda i: (0, 0)),
        ],
        out_specs=[pl.BlockSpec((tm, w), lambda i: (i, 0)) for w in PROJ_WIDTHS],
        out_shape=[jax.ShapeDtypeStruct((t, w), d) for w, d in zip(PROJ_WIDTHS, PROJ_DTYPES)],
        compiler_params=_params("parallel"),
        name="inproj",
    )(x, gain.reshape(1, D_MODEL), w_perm)


def _block_tables(seq_lens, block):
    first, last = [], []
    for n in seq_lens:
        nb = n // block
        first += [1] + [0] * (nb - 1)
        last += [0] * (nb - 1) + [1]
    first = np.asarray(first, np.int32)
    last = np.asarray(last, np.int32)
    idx = np.arange(len(first), dtype=np.int32)
    prev = np.where(first == 1, idx, idx - 1).astype(np.int32)
    nxt = np.where(last == 1, idx, idx + 1).astype(np.int32)
    return first, last, prev, nxt


NA_ROWS = MIX_BLOCK // GRID_W
NA_KEYS = NA_WIN_ROWS * GRID_W


def _na_bias_table(rpb):
    c = np.arange(GRID_W)
    c_start = np.clip(c - NA_WIN_COLS // 2, 0, GRID_W - NA_WIN_COLS)
    in_win = (c[None, :] >= c_start[:, None]) & (c[None, :] < c_start[:, None] + NA_WIN_COLS)
    dc = np.clip(c[None, :] - c[:, None], 1 - NA_WIN_COLS, NA_WIN_COLS - 1) + NA_WIN_COLS - 1
    v = np.arange(NA_WIN_ROWS)
    w = np.arange(NA_WIN_ROWS)
    dr = w[None, :] - v[:, None] + NA_WIN_ROWS - 1
    tab = rpb.astype(F32)[:, dr[:, None, :, None], dc[None, :, None, :]]
    tab = jnp.where(in_win[None, None, :, None, :], tab, NEG)
    return tab.reshape(NA_HEADS, NA_WIN_ROWS, GRID_W, NA_KEYS)


def _na_kernel(first_ref, last_ref, prev_ref, next_ref,
               q_ref, kp_ref, kc_ref, kn_ref, vp_ref, vc_ref, vn_ref, bias_ref, o_ref, kbuf, vbuf):
    del prev_ref, next_ref
    g = pl.program_id(0)
    kbuf[0:MIX_BLOCK] = kp_ref[...]
    kbuf[MIX_BLOCK:2 * MIX_BLOCK] = kc_ref[...]
    kbuf[2 * MIX_BLOCK:3 * MIX_BLOCK] = kn_ref[...]
    vbuf[0:MIX_BLOCK] = vp_ref[...]
    vbuf[MIX_BLOCK:2 * MIX_BLOCK] = vc_ref[...]
    vbuf[2 * MIX_BLOCK:3 * MIX_BLOCK] = vn_ref[...]
    scale = HEAD_DIM ** -0.5
    half = NA_WIN_ROWS // 2
    for a in range(NA_ROWS):
        edge = (first_ref[g] if a < half else last_ref[g]) == 1
        start = pl.multiple_of(jnp.where(edge, NA_ROWS, a + half) * GRID_W, GRID_W)
        variant = jnp.where(edge, a, half)
        for h in range(NA_HEADS):
            cols = slice(h * HEAD_DIM, (h + 1) * HEAD_DIM)
            q = q_ref[a * GRID_W:(a + 1) * GRID_W, cols]
            k = kbuf[pl.ds(start, NA_KEYS), cols]
            v = vbuf[pl.ds(start, NA_KEYS), cols]
            s = lax.dot_general(q, k, (((1,), (1,)), ((), ())), preferred_element_type=F32)
            s = s * scale + bias_ref[h, variant]
            p = jnp.exp(s - jnp.max(s, axis=-1, keepdims=True))
            l = jnp.sum(p, axis=-1, keepdims=True)
            o = jnp.dot(_bf(p), v, preferred_element_type=F32) / l
            o_ref[a * GRID_W:(a + 1) * GRID_W, cols] = o.astype(o_ref.dtype)


def _na(na_qkv, bias, tables):
    t = na_qkv.shape[0]
    nb = t // MIX_BLOCK
    blk = (MIX_BLOCK, NA_W)
    cur = lambda col: pl.BlockSpec(blk, lambda g, f, l, p, n: (g, col))
    prv = lambda col: pl.BlockSpec(blk, lambda g, f, l, p, n: (p[g], col))
    nxt = lambda col: pl.BlockSpec(blk, lambda g, f, l, p, n: (n[g], col))
    grid_spec = pltpu.PrefetchScalarGridSpec(
        num_scalar_prefetch=4,
        grid=(nb,),
        in_specs=[cur(0), prv(1), cur(1), nxt(1), prv(2), cur(2), nxt(2),
                  pl.BlockSpec((NA_HEADS, NA_WIN_ROWS, GRID_W, NA_KEYS), lambda g, f, l, p, n: (0, 0, 0, 0))],
        out_specs=pl.BlockSpec(blk, lambda g, f, l, p, n: (g, 0)),
        scratch_shapes=[pltpu.VMEM((3 * MIX_BLOCK, NA_W), BF16), pltpu.VMEM((3 * MIX_BLOCK, NA_W), BF16)],
    )
    return pl.pallas_call(
        _na_kernel,
        grid_spec=grid_spec,
        out_shape=jax.ShapeDtypeStruct((t, NA_W), BF16),
        compiler_params=_params("arbitrary"),
        name="na",
    )(*tables, na_qkv, na_qkv, na_qkv, na_qkv, na_qkv, na_qkv, na_qkv, bias)


HALO = 8


def _head_sum_matrix(width):
    r = lax.broadcasted_iota(jnp.int32, (width, width), 0) // HEAD_DIM
    c = lax.broadcasted_iota(jnp.int32, (width, width), 1) // HEAD_DIM
    return jnp.where(r == c, 1.0, 0.0).astype(BF16)


def _gdn_prep_kernel(first_ref, last_ref, x_ref, xp_ref, xn_ref, w_ref, o_ref):
    g = pl.program_id(0)
    tb = x_ref.shape[0]
    prev = jnp.where(first_ref[g] == 1, 0.0, xp_ref[...])
    nxt = jnp.where(last_ref[g] == 1, 0.0, xn_ref[...])
    xe = jnp.concatenate([prev, x_ref[...], nxt], axis=0)
    n = tb + 2 * HALO
    acc = None
    for j in range(GDN_CONV):
        shift = (GDN_CONV // 2 - j) % n
        tap = xe if shift == 0 else pltpu.roll(xe, shift, 0)
        term = tap[HALO:HALO + tb] * w_ref[j:j + 1, :]
        acc = term if acc is None else acc + term
    y = acc * _sigmoid(acc)
    ones_bd = _head_sum_matrix(GDN_W)
    q = y[:, 0:GDN_W]
    k = y[:, GDN_W:2 * GDN_W]
    qn = q * lax.rsqrt(_split_mm(q * q, ones_bd, 2) + EPS) * (HEAD_DIM ** -0.5)
    kn = k * lax.rsqrt(_split_mm(k * k, ones_bd, 2) + EPS)
    o_ref[:, 0:GDN_W] = qn
    o_ref[:, GDN_W:2 * GDN_W] = kn
    o_ref[:, 2 * GDN_W:3 * GDN_W] = y[:, 2 * GDN_W:3 * GDN_W]


def _gdn_prep(gdn_qkv, conv_w, tables):
    t, c = gdn_qkv.shape
    nb = t // MIX_BLOCK
    per = MIX_BLOCK // HALO
    n_halo = t // HALO
    first, last = tables[0], tables[1]
    grid_spec = pltpu.PrefetchScalarGridSpec(
        num_scalar_prefetch=2,
        grid=(nb,),
        in_specs=[
            pl.BlockSpec((MIX_BLOCK, c), lambda g, f, l: (g, 0)),
            pl.BlockSpec((HALO, c), lambda g, f, l: (jnp.maximum(g * per - 1, 0), 0)),
            pl.BlockSpec((HALO, c), lambda g, f, l: (jnp.minimum((g + 1) * per, n_halo - 1), 0)),
            pl.BlockSpec((8, c), lambda g, f, l: (0, 0)),
        ],
        out_specs=pl.BlockSpec((MIX_BLOCK, c), lambda g, f, l: (g, 0)),
    )
    w8 = jnp.concatenate([conv_w.astype(F32), jnp.zeros((8 - GDN_CONV, c), F32)], axis=0)
    return pl.pallas_call(
        _gdn_prep_kernel,
        grid_spec=grid_spec,
        out_shape=jax.ShapeDtypeStruct((t, c), F32),
        compiler_params=_params("arbitrary"),
        name="gdn_prep",
    )(first, last, gdn_qkv, gdn_qkv, gdn_qkv, w8)


STK = 2 * CHUNK


class _Masks:
    def __init__(self, backward):
        r = lax.broadcasted_iota(jnp.int32, (STK, STK), 0)
        c = lax.broadcasted_iota(jnp.int32, (STK, STK), 1)
        same = (r // CHUNK) == (c // CHUNK)
        if backward:
            self.incl = jnp.logical_and(same, c >= r)
            self.strict = jnp.logical_and(same, c > r)
        else:
            self.incl = jnp.logical_and(same, c <= r)
            self.strict = jnp.logical_and(same, c < r)
        self.same = same
        self.blk16 = (r // 16) == (c // 16)
        self.eye = jnp.where(r == c, 1.0, 0.0).astype(F32)
        self.last_row = 0 if backward else CHUNK - 1


def _cumsum_matrix(backward):
    r = lax.broadcasted_iota(jnp.int32, (CHUNK, CHUNK), 0)
    c = lax.broadcasted_iota(jnp.int32, (CHUNK, CHUNK), 1)
    keep = (c >= r) if backward else (c <= r)
    return jnp.where(keep, 1.0, 0.0).astype(BF16)


def _lane_half():
    return lax.broadcasted_iota(jnp.int32, (CHUNK, PAIR_W), 1) < HEAD_DIM


def _expand(tile, c0, c1, m0):
    return jnp.where(m0, tile[:, c0:c0 + 1], tile[:, c1:c1 + 1])


def _stack_cols(tile, c0, c1):
    top = jnp.broadcast_to(tile[:, c0:c0 + 1], (CHUNK, STK))
    bot = jnp.broadcast_to(tile[:, c1:c1 + 1], (CHUNK, STK))
    return jnp.concatenate([top, bot], axis=0)


def _stack(x, m0):
    return jnp.concatenate([jnp.where(m0, x, 0.0), jnp.where(m0, 0.0, x)], axis=0)


def _unstack(x):
    return x[:CHUNK] + x[CHUNK:]


def _unit_tri_inverse(a, mk):
    eye = mk.eye
    d = jnp.where(mk.blk16, a, 0.0)
    rest = a - d
    d2 = _mm(d, d)
    d4 = _mm(d2, d2)
    d8 = _mm(d4, d4)
    td = _mm(_mm(_mm(eye - d, eye + d2), eye + d4), eye + d8)
    n = _mm(td, rest)
    n2 = _mm(n, n)
    return _mm(_mm(eye - n, eye + n2), td)


def _gdn_chunk(mk, m0, q2, k2, v2, gc, beta, c0, c1, s):
    gc2 = _expand(gc, c0, c1, m0)
    b2 = _expand(beta, GATE_GROUP + c0, GATE_GROUP + c1, m0)
    gl = gc2[mk.last_row:mk.last_row + 1, :]
    eg2 = jnp.exp(gc2)
    kb2 = k2 * b2
    vb2 = v2 * b2
    kbg2 = kb2 * eg2
    qg2 = q2 * eg2
    kdec2 = k2 * jnp.exp(gl - gc2)
    cst = _stack_cols(gc, c0, c1)
    decay = jnp.exp(jnp.where(mk.incl, cst - cst.T, NEG))
    aq = _mm_nt(jnp.concatenate([kb2, kb2, q2, q2], axis=0), _stack(k2, m0))
    a = jnp.where(mk.strict, aq[:STK] * decay, 0.0)
    p = aq[STK:] * decay
    t = _unit_tri_inverse(a, mk)
    uw = _mm(t, jnp.concatenate([_stack(vb2, m0), _stack(kbg2, m0)], axis=1))
    u2 = _unstack(uw[:, :PAIR_W])
    w2 = _unstack(uw[:, PAIR_W:])
    wq = _mm(jnp.concatenate([w2, qg2], axis=0), s)
    vn = u2 - wq[:CHUNK]
    o2 = wq[CHUNK:] + _unstack(_mm(p, _stack(vn, m0)))
    s_new = jnp.exp(gl) * s + jnp.where(mk.same, _mm_tn(kdec2, vn), 0.0)
    return o2, s_new


def _gdn_kernel(first_ref, last_ref, xf_ref, xb_ref, gf_ref, gb_ref, gp_ref, of_ref, ob_ref, s_ref):
    i = pl.program_id(0)
    nb = pl.num_programs(0)
    n_chunks = xf_ref.shape[0] // CHUNK

    @pl.when(first_ref[i] == 1)
    def _():
        s_ref[0] = jnp.zeros_like(s_ref[0])

    @pl.when(last_ref[nb - 1 - i] == 1)
    def _():
        s_ref[1] = jnp.zeros_like(s_ref[1])

    m0 = _lane_half()
    bias_row = gp_ref[0:1, :]
    a_row = -jnp.exp(gp_ref[1:2, :])

    for d, (x_ref, g_ref, o_ref) in enumerate(((xf_ref, gf_ref, of_ref), (xb_ref, gb_ref, ob_ref))):
        mk = _Masks(backward=(d == 1))
        cs = _cumsum_matrix(backward=(d == 1))

        def body(c, carry, d=d, x_ref=x_ref, g_ref=g_ref, o_ref=o_ref, mk=mk, cs=cs):
            cc = (n_chunks - 1 - c) if d == 1 else c
            rows = pl.ds(pl.multiple_of(cc * CHUNK, CHUNK), CHUNK)
            gates = g_ref[rows, :]
            log_decay = a_row * _softplus(gates + bias_row)
            gc = _split_mm_left(cs, log_decay, 3)
            beta = _sigmoid(gates)
            for p in range(N_PAIRS):
                lanes = slice(p * PAIR_W, (p + 1) * PAIR_W)
                q2 = x_ref[rows, lanes]
                k2 = x_ref[rows, GDN_W + p * PAIR_W:GDN_W + (p + 1) * PAIR_W]
                v2 = x_ref[rows, 2 * GDN_W + p * PAIR_W:2 * GDN_W + (p + 1) * PAIR_W]
                c0 = d * GDN_HEADS + 2 * p
                o2, s_new = _gdn_chunk(mk, m0, q2, k2, v2, gc, beta, c0, c0 + 1, s_ref[d, p])
                o_ref[rows, lanes] = o2
                s_ref[d, p] = s_new
            return carry

        lax.fori_loop(0, n_chunks, body, 0)


def _gdn(gqkv, gates, gate_params, tables):
    t = gqkv.shape[0]
    nb = t // MIX_BLOCK
    first, last = tables[0], tables[1]
    fwd = lambda w: pl.BlockSpec((MIX_BLOCK, w), lambda i, f, l: (i, 0))
    bwd = lambda w: pl.BlockSpec((MIX_BLOCK, w), lambda i, f, l: (nb - 1 - i, 0))
    grid_spec = pltpu.PrefetchScalarGridSpec(
        num_scalar_prefetch=2,
        grid=(nb,),
        in_specs=[fwd(3 * GDN_W), bwd(3 * GDN_W), fwd(GATE_W), bwd(GATE_W),
                  pl.BlockSpec((8, GATE_W), lambda i, f, l: (0, 0))],
        out_specs=[fwd(GDN_W), bwd(GDN_W)],
        scratch_shapes=[pltpu.VMEM((2, N_PAIRS, STK, STK), F32)],
    )
    return pl.pallas_call(
        _gdn_kernel,
        grid_spec=grid_spec,
        out_shape=[jax.ShapeDtypeStruct((t, GDN_W), F32)] * 2,
        compiler_params=_params("arbitrary"),
        name="gdn",
    )(first, last, gqkv, gqkv, gates, gates, gate_params)


def _mlstm_chunk(mk, m0, bd2, q2, k2, v2, fc, li, c0, c1, cn, m_st):
    ci0, ci1 = 2 * GATE_GROUP + c0, 2 * GATE_GROUP + c1
    cf0, cf1 = 3 * GATE_GROUP + c0, 3 * GATE_GROUP + c1
    fc2 = _expand(fc, cf0, cf1, m0)
    li2 = _expand(li, ci0, ci1, m0)
    fl = fc2[mk.last_row:mk.last_row + 1, :]
    g_end = fl - fc2 + li2
    ml = jnp.max(g_end, axis=0, keepdims=True)
    kw2 = k2 * jnp.exp(g_end - ml)
    m_new = jnp.maximum(fl + m_st, ml)
    a_row = jnp.exp(fl + m_st - m_new)
    b_row = jnp.exp(ml - m_new)
    ones = jnp.ones((CHUNK, PAIR_W), F32)
    upd = jnp.where(bd2, _mm_tn(kw2, jnp.concatenate([v2, ones], axis=1)), 0.0)
    cn_new = jnp.concatenate([a_row, a_row], axis=1) * cn + jnp.concatenate([b_row, b_row], axis=1) * upd

    m_inter2 = fc2 + m_st
    cst = _stack_cols(fc, cf0, cf1)
    est = cst - _stack_cols(li, ci0, ci1)
    dmat = jnp.where(mk.incl, cst - est.T, NEG)
    m_intra = jnp.max(dmat, axis=1, keepdims=True)
    m_inter_st = cst + jnp.concatenate(
        [jnp.broadcast_to(m_st[:, 0:1], (CHUNK, 1)), jnp.broadcast_to(m_st[:, HEAD_DIM:HEAD_DIM + 1], (CHUNK, 1))],
        axis=0)
    m_t_st = jnp.maximum(m_inter_st, m_intra)
    m_t2 = jnp.where(m0, m_t_st[:CHUNK], m_t_st[CHUNK:])
    qk = _mm_nt(jnp.concatenate([q2, q2], axis=0), _stack(k2, m0))
    smat = qk * jnp.exp(dmat - m_t_st)
    intra = _unstack(_mm(smat, jnp.concatenate([_stack(v2, m0), _stack(ones, m0)], axis=1)))
    inter = _mm(q2, cn)
    w_inter = jnp.exp(m_inter2 - m_t2)
    num = intra[:, :PAIR_W] + w_inter * inter[:, :PAIR_W]
    den = intra[:, PAIR_W:] + w_inter * inter[:, PAIR_W:]
    out = num / jnp.maximum(jnp.abs(den), jnp.exp(-m_t2))
    return out, cn_new, m_new


def _mlstm_kernel(first_ref, last_ref, xf_ref, xb_ref, gf_ref, gb_ref, gp_ref, of_ref, ob_ref, c_ref, m_ref):
    i = pl.program_id(0)
    nb = pl.num_programs(0)
    n_chunks = xf_ref.shape[0] // CHUNK

    @pl.when(first_ref[i] == 1)
    def _():
        c_ref[0] = jnp.zeros_like(c_ref[0])
        m_ref[0] = jnp.full_like(m_ref[0], M_INIT)

    @pl.when(last_ref[nb - 1 - i] == 1)
    def _():
        c_ref[1] = jnp.zeros_like(c_ref[1])
        m_ref[1] = jnp.full_like(m_ref[1], M_INIT)

    m0 = _lane_half()
    r = lax.broadcasted_iota(jnp.int32, (STK, 2 * STK), 0) // HEAD_DIM
    c = (lax.broadcasted_iota(jnp.int32, (STK, 2 * STK), 1) % STK) // HEAD_DIM
    bd2 = r == c
    bias_row = gp_ref[0:1, :]

    for d, (x_ref, g_ref, o_ref) in enumerate(((xf_ref, gf_ref, of_ref), (xb_ref, gb_ref, ob_ref))):
        mk = _Masks(backward=(d == 1))
        cs = _cumsum_matrix(backward=(d == 1))

        def body(cidx, carry, d=d, x_ref=x_ref, g_ref=g_ref, o_ref=o_ref, mk=mk, cs=cs):
            cc = (n_chunks - 1 - cidx) if d == 1 else cidx
            rows = pl.ds(pl.multiple_of(cc * CHUNK, CHUNK), CHUNK)
            li = g_ref[rows, :] + bias_row
            log_f = -_softplus(-li)
            fc = _split_mm_left(cs, log_f, 3)
            for p in range(N_PAIRS):
                lanes = slice(p * PAIR_W, (p + 1) * PAIR_W)
                q2 = x_ref[rows, lanes]
                k2 = x_ref[rows, MLSTM_W + p * PAIR_W:MLSTM_W + (p + 1) * PAIR_W] * (HEAD_DIM ** -0.5)
                v2 = x_ref[rows, 2 * MLSTM_W + p * PAIR_W:2 * MLSTM_W + (p + 1) * PAIR_W]
                c0 = d * MLSTM_HEADS + 2 * p
                out, cn_new, m_new = _mlstm_chunk(mk, m0, bd2, q2, k2, v2, fc, li, c0, c0 + 1,
                                                  c_ref[d, p], m_ref[d, p])
                o_ref[rows, lanes] = out
                c_ref[d, p] = cn_new
                m_ref[d, p] = m_new
            return carry

        lax.fori_loop(0, n_chunks, body, 0)


def _mlstm(ml_qkv, gates, gate_params, tables):
    t = ml_qkv.shape[0]
    nb = t // MIX_BLOCK
    first, last = tables[0], tables[1]
    fwd = lambda w: pl.BlockSpec((MIX_BLOCK, w), lambda i, f, l: (i, 0))
    bwd = lambda w: pl.BlockSpec((MIX_BLOCK, w), lambda i, f, l: (nb - 1 - i, 0))
    grid_spec = pltpu.PrefetchScalarGridSpec(
        num_scalar_prefetch=2,
        grid=(nb,),
        in_specs=[fwd(3 * MLSTM_W), bwd(3 * MLSTM_W), fwd(GATE_W), bwd(GATE_W),
                  pl.BlockSpec((8, GATE_W), lambda i, f, l: (0, 0))],
        out_specs=[fwd(MLSTM_W), bwd(MLSTM_W)],
        scratch_shapes=[pltpu.VMEM((2, N_PAIRS, STK, 2 * STK), F32), pltpu.VMEM((2, N_PAIRS, 1, PAIR_W), F32)],
    )
    return pl.pallas_call(
        _mlstm_kernel,
        grid_spec=grid_spec,
        out_shape=[jax.ShapeDtypeStruct((t, MLSTM_W), F32)] * 2,
        compiler_params=_params("arbitrary"),
        name="mlstm",
    )(first, last, ml_qkv, ml_qkv, gates, gates, gate_params)


OUT_TM = 512


def _outproj_kernel(x_ref, na_ref, gf_ref, gb_ref, z_ref, hf_ref, hb_ref, o_ref, gn_ref, mn_ref,
                    w_ref, y_ref):
    ones_bd = _head_sum_matrix(GDN_W)

    def head_rms(v, gain):
        ms = _split_mm(v * v, ones_bd, 2) * (1.0 / HEAD_DIM)
        return v * lax.rsqrt(ms + EPS) * gain

    go = head_rms(gf_ref[...] + gb_ref[...], gn_ref[...])
    z = z_ref[...]
    gdn_out = go * (z * _sigmoid(z))
    mh = head_rms(hf_ref[...] + hb_ref[...], mn_ref[...])
    ml_out = _sigmoid(o_ref[...]) * mh
    y = x_ref[...]
    y = y + jnp.dot(na_ref[...], w_ref[0:NA_W, :], preferred_element_type=F32)
    y = y + jnp.dot(_bf(gdn_out), w_ref[NA_W:NA_W + GDN_W, :], preferred_element_type=F32)
    y = y + jnp.dot(_bf(ml_out), w_ref[NA_W + GDN_W:, :], preferred_element_type=F32)
    y_ref[...] = y


def _outproj(x, na_out, gf, gb, z, hf, hb, o, gdn_norm, mlstm_norm, w_out):
    t = x.shape[0]
    tm = min(OUT_TM, t)
    tok = lambda w: pl.BlockSpec((tm, w), lambda i: (i, 0))
    const = lambda r, w: pl.BlockSpec((r, w), lambda i: (0, 0))
    gn = jnp.tile(gdn_norm.astype(F32), GDN_HEADS).reshape(1, GDN_W)
    mn = mlstm_norm.astype(F32).reshape(1, MLSTM_W)
    return pl.pallas_call(
        _outproj_kernel,
        grid=(t // tm,),
        in_specs=[tok(D_MODEL), tok(NA_W), tok(GDN_W), tok(GDN_W), tok(GDN_W), tok(MLSTM_W), tok(MLSTM_W),
                  tok(MLSTM_W), const(1, GDN_W), const(1, MLSTM_W), const(D_MODEL, D_MODEL)],
        out_specs=tok(D_MODEL),
        out_shape=jax.ShapeDtypeStruct((t, D_MODEL), F32),
        compiler_params=_params("parallel"),
        name="outproj",
    )(x, na_out, gf, gb, z, hf, hb, o, gn, mn, _bf(w_out))


def _gate_params(bias_rows, a_log):
    pad = GATE_W - 4 * GATE_GROUP
    row0 = jnp.concatenate([b.astype(F32).reshape(-1) for b in bias_rows] + [jnp.zeros((pad,), F32)])
    row1 = jnp.concatenate([a_log.astype(F32).reshape(-1), jnp.zeros((GATE_W - GATE_GROUP,), F32)])
    return jnp.concatenate([row0[None], row1[None], jnp.zeros((6, GATE_W), F32)], axis=0)


def _trunk(x, seq_lens, p):
    depth = p["ffn1_norm"].shape[0]
    tables = tuple(jnp.asarray(a) for a in _block_tables(seq_lens, MIX_BLOCK))
    zeros_g = jnp.zeros((2, GDN_HEADS), F32)
    for i in range(depth):
        x = _ffn(x, p["ffn1_norm"][i], _bf(p["ffn1_w_in"][i]), _bf(p["ffn1_w_out"][i]), p["final_norm"], False)
        na_qkv, gdn_qkv, gdn_z, ml_qkv, ml_o, gates = _inproj(x, p["mix_norm"][i], _permute_w_in(p["w_in"][i]))
        na_out = _na(na_qkv, _na_bias_table(p["na_rpb"][i]), tables)
        gqkv = _gdn_prep(gdn_qkv, p["gdn_conv"][i], tables)
        gdn_gp = _gate_params([p["gdn_dt_bias"][i], zeros_g, zeros_g, zeros_g], p["gdn_a_log"][i])
        gf, gb = _gdn(gqkv, gates, gdn_gp, tables)
        ml_gp = _gate_params([zeros_g, zeros_g, p["mlstm_i_bias"][i], p["mlstm_f_bias"][i]], zeros_g)
        hf, hb = _mlstm(ml_qkv, gates, ml_gp, tables)
        x = _outproj(x, na_out, gf, gb, gdn_z, hf, hb, ml_o, p["gdn_norm"][i], p["mlstm_norm"][i], p["w_out"][i])
        x = _ffn(x, p["ffn2_norm"][i], _bf(p["ffn2_w_in"][i]), _bf(p["ffn2_w_out"][i]), p["final_norm"],
                 i == depth - 1)
    return x


def kernel(x_prompt, x_sample, ffn1_norm, ffn1_w_in, ffn1_w_out, mix_norm, w_in, w_out, na_rpb, gdn_conv,
           gdn_a_log, gdn_dt_bias, gdn_norm, mlstm_i_bias, mlstm_f_bias, mlstm_norm, ffn2_norm, ffn2_w_in,
           ffn2_w_out, final_norm):
    p = dict(ffn1_norm=ffn1_norm, ffn1_w_in=ffn1_w_in, ffn1_w_out=ffn1_w_out, mix_norm=mix_norm, w_in=w_in,
             w_out=w_out, na_rpb=na_rpb, gdn_conv=gdn_conv, gdn_a_log=gdn_a_log, gdn_dt_bias=gdn_dt_bias,
             gdn_norm=gdn_norm, mlstm_i_bias=mlstm_i_bias, mlstm_f_bias=mlstm_f_bias, mlstm_norm=mlstm_norm,
             ffn2_norm=ffn2_norm, ffn2_w_in=ffn2_w_in, ffn2_w_out=ffn2_w_out, final_norm=final_norm)
    bp, lp, _ = x_prompt.shape
    bs, ls, _ = x_sample.shape
    x = jnp.concatenate([x_prompt.reshape(bp * lp, D_MODEL), x_sample.reshape(bs * ls, D_MODEL)], axis=0)
    y = _trunk(x, (lp,) * bp + (ls,) * bs, p)
    return (y[:bp * lp].reshape(bp, lp, D_MODEL), y[bp * lp:].reshape(bs, ls, D_MODEL))
```

```python
import functools

import jax
import jax.numpy as jnp
import numpy as np
from jax import lax
from jax.experimental import pallas as pl
from jax.experimental.pallas import tpu as pltpu

D_MODEL = 1024
D_FF = 2816
HEAD_DIM = 64
NA_HEADS = 4
GDN_HEADS = 6
MLSTM_HEADS = 6
NA_W = NA_HEADS * HEAD_DIM
GDN_W = GDN_HEADS * HEAD_DIM
MLSTM_W = MLSTM_HEADS * HEAD_DIM
GRID_W = 64
NA_WIN_ROWS = 8
NA_WIN_COLS = 16
GDN_CONV = 5
CHUNK = 64
M_INIT = -1e30
EPS = 1e-6
NEG = -1e30

LANES = 128
PAIR_W = 2 * HEAD_DIM
N_PAIRS = GDN_HEADS // 2
GATE_W = LANES
GATE_GROUP = 2 * GDN_HEADS
MIX_BLOCK = 512
VMEM_LIMIT = 56 * 1024 * 1024

F32 = jnp.float32
BF16 = jnp.bfloat16


def _bf(x):
    return x.astype(BF16)


def _mm(a, b):
    return jnp.dot(_bf(a), _bf(b), preferred_element_type=F32)


def _mm_nt(a, b):
    return lax.dot_general(_bf(a), _bf(b), (((1,), (1,)), ((), ())), preferred_element_type=F32)


def _mm_tn(a, b):
    return lax.dot_general(_bf(a), _bf(b), (((0,), (0,)), ((), ())), preferred_element_type=F32)


def _split_mm(a, b_exact, parts):
    acc = None
    rem = a
    for _ in range(parts):
        piece = _bf(rem)
        term = jnp.dot(piece, b_exact, preferred_element_type=F32)
        acc = term if acc is None else acc + term
        rem = rem - piece.astype(F32)
    return acc


def _split_mm_left(a_exact, b, parts):
    acc = None
    rem = b
    for _ in range(parts):
        piece = _bf(rem)
        term = jnp.dot(a_exact, piece, preferred_element_type=F32)
        acc = term if acc is None else acc + term
        rem = rem - piece.astype(F32)
    return acc


def _sigmoid(x):
    return 1.0 / (1.0 + jnp.exp(-x))


def _softplus(x):
    return jnp.maximum(x, 0.0) + jnp.log(1.0 + jnp.exp(-jnp.abs(x)))


def _rms(x, gain):
    return x * lax.rsqrt(jnp.mean(x * x, axis=-1, keepdims=True) + EPS) * gain


def _params(*sem):
    return pltpu.CompilerParams(dimension_semantics=sem, vmem_limit_bytes=VMEM_LIMIT)


FFN_TM = 1024
FFN_TF = 256


def _ffn_kernel(x_ref, g_ref, wg_ref, wu_ref, wo_ref, fn_ref, o_ref, xn_ref, acc_ref, *, final):
    j = pl.program_id(1)

    @pl.when(j == 0)
    def _():
        xn_ref[...] = _bf(_rms(x_ref[...], g_ref[...]))
        acc_ref[...] = jnp.zeros_like(acc_ref)

    xn = xn_ref[...]
    gate = jnp.dot(xn, wg_ref[...], preferred_element_type=F32)
    up = jnp.dot(xn, wu_ref[...], preferred_element_type=F32)
    act = gate * _sigmoid(gate) * up
    acc_ref[...] += jnp.dot(_bf(act), wo_ref[...], preferred_element_type=F32)

    @pl.when(j == pl.num_programs(1) - 1)
    def _():
        y = x_ref[...] + 0.5 * acc_ref[...]
        if final:
            y = _rms(y, fn_ref[...])
        o_ref[...] = y


def _ffn(x, gain, w_in, w_out, final_gain, final):
    t = x.shape[0]
    tm = min(FFN_TM, t)
    nf = D_FF // FFN_TF
    return pl.pallas_call(
        functools.partial(_ffn_kernel, final=final),
        grid=(t // tm, nf),
        in_specs=[
            pl.BlockSpec((tm, D_MODEL), lambda i, j: (i, 0)),
            pl.BlockSpec((1, D_MODEL), lambda i, j: (0, 0)),
            pl.BlockSpec((D_MODEL, FFN_TF), lambda i, j: (0, j)),
            pl.BlockSpec((D_MODEL, FFN_TF), lambda i, j: (0, j + nf)),
            pl.BlockSpec((FFN_TF, D_MODEL), lambda i, j: (j, 0)),
            pl.BlockSpec((1, D_MODEL), lambda i, j: (0, 0)),
        ],
        out_specs=pl.BlockSpec((tm, D_MODEL), lambda i, j: (i, 0)),
        out_shape=jax.ShapeDtypeStruct((t, D_MODEL), F32),
        scratch_shapes=[pltpu.VMEM((tm, D_MODEL), BF16), pltpu.VMEM((tm, D_MODEL), F32)],
        compiler_params=_params("parallel", "arbitrary"),
        name="ffn",
    )(x, gain.reshape(1, D_MODEL), w_in, w_in, w_out, final_gain.reshape(1, D_MODEL))


PROJ_TM = 512
PROJ_WIDTHS = (3 * NA_W, 3 * GDN_W, GDN_W, 3 * MLSTM_W, MLSTM_W, GATE_W)
PROJ_DTYPES = (BF16, F32, F32, F32, F32, F32)
P_PERM = sum(PROJ_WIDTHS)


def _permute_w_in(w_in):
    o = np.cumsum([0, 3 * NA_W, 3 * GDN_W, GDN_W, 2 * GDN_HEADS, 2 * GDN_HEADS,
                   3 * MLSTM_W, MLSTM_W, 2 * MLSTM_HEADS, 2 * MLSTM_HEADS])
    n_gates = 4 * 2 * GDN_HEADS
    cols = [w_in[:, o[0]:o[3]], w_in[:, o[5]:o[7]], w_in[:, o[3]:o[5]], w_in[:, o[7]:o[9]],
            jnp.zeros((D_MODEL, GATE_W - n_gates), w_in.dtype)]
    return _bf(jnp.concatenate(cols, axis=1))


def _inproj_kernel(x_ref, g_ref, w_ref, *o_refs):
    xn = _bf(_rms(x_ref[...], g_ref[...]))
    off = 0
    for o_ref, width in zip(o_refs, PROJ_WIDTHS):
        o_ref[...] = jnp.dot(xn, w_ref[:, off:off + width], preferred_element_type=F32).astype(o_ref.dtype)
        off += width


def _inproj(x, gain, w_perm):
    t = x.shape[0]
    tm = min(PROJ_TM, t)
    return pl.pallas_call(
        _inproj_kernel,
        grid=(t // tm,),
        in_specs=[
            pl.BlockSpec((tm, D_MODEL), lambda i: (i, 0)),
            pl.BlockSpec((1, D_MODEL), lambda i: (0, 0)),
            pl.BlockSpec((D_MODEL, P_PERM), lambda i: (0, 0)),
        ],
        out_specs=[pl.BlockSpec((tm, w), lambda i: (i, 0)) for w in PROJ_WIDTHS],
        out_shape=[jax.ShapeDtypeStruct((t, w), d) for w, d in zip(PROJ_WIDTHS, PROJ_DTYPES)],
        compiler_params=_params("parallel"),
        name="inproj",
    )(x, gain.reshape(1, D_MODEL), w_perm)


def _block_tables(seq_lens, block):
    first, last = [], []
    for n in seq_lens:
        nb = n // block
        first += [1] + [0] * (nb - 1)
        last += [0] * (nb - 1) + [1]
    first = np.asarray(first, np.int32)
    last = np.asarray(last, np.int32)
    idx = np.arange(len(first), dtype=np.int32)
    prev = np.where(first == 1, idx, idx - 1).astype(np.int32)
    nxt = np.where(last == 1, idx, idx + 1).astype(np.int32)
    return first, last, prev, nxt


NA_ROWS = MIX_BLOCK // GRID_W
NA_KEYS = NA_WIN_ROWS * GRID_W
NA_ROW_GROUP = 2


def _na_bias_table(rpb):
    c = np.arange(GRID_W)
    c_start = np.clip(c - NA_WIN_COLS // 2, 0, GRID_W - NA_WIN_COLS)
    in_win = (c[None, :] >= c_start[:, None]) & (c[None, :] < c_start[:, None] + NA_WIN_COLS)
    dc = np.clip(c[None, :] - c[:, None], 1 - NA_WIN_COLS, NA_WIN_COLS - 1) + NA_WIN_COLS - 1
    one_hot = jnp.asarray(dc[None] == np.arange(2 * NA_WIN_COLS - 1)[:, None, None], F32)
    by_col = jnp.einsum("hrc,cqk->hrqk", rpb.astype(F32), one_hot, precision=lax.Precision.HIGHEST)
    tab = jnp.stack([by_col[:, NA_WIN_ROWS - 1 - v:2 * NA_WIN_ROWS - 1 - v] for v in range(NA_WIN_ROWS)],
                    axis=1)
    tab = jnp.where(in_win[None, None, None], tab, NEG).transpose(0, 1, 3, 2, 4)
    return tab.reshape(NA_HEADS, NA_WIN_ROWS, GRID_W, NA_KEYS)


def _na_kernel(first_ref, last_ref, prev_ref, next_ref,
               q_ref, kp_ref, kc_ref, kn_ref, vp_ref, vc_ref, vn_ref, bias_ref, o_ref, kbuf, vbuf):
    del prev_ref, next_ref
    g = pl.program_id(0)
    kbuf[0:MIX_BLOCK] = kp_ref[...]
    kbuf[MIX_BLOCK:2 * MIX_BLOCK] = kc_ref[...]
    kbuf[2 * MIX_BLOCK:3 * MIX_BLOCK] = kn_ref[...]
    vbuf[0:MIX_BLOCK] = vp_ref[...]
    vbuf[MIX_BLOCK:2 * MIX_BLOCK] = vc_ref[...]
    vbuf[2 * MIX_BLOCK:3 * MIX_BLOCK] = vn_ref[...]
    scale = HEAD_DIM ** -0.5
    half = NA_WIN_ROWS // 2
    for a0 in range(0, NA_ROWS, NA_ROW_GROUP):
        units, q, k, v, bias = [], [], [], [], []
        for a in range(a0, a0 + NA_ROW_GROUP):
            edge = (first_ref[g] if a < half else last_ref[g]) == 1
            start = pl.multiple_of(jnp.where(edge, NA_ROWS, a + half) * GRID_W, GRID_W)
            variant = jnp.where(edge, a, half)
            for h in range(NA_HEADS):
                cols = slice(h * HEAD_DIM, (h + 1) * HEAD_DIM)
                units.append((a, cols))
                q.append(q_ref[a * GRID_W:(a + 1) * GRID_W, cols])
                k.append(kbuf[pl.ds(start, NA_KEYS), cols])
                v.append(vbuf[pl.ds(start, NA_KEYS), cols])
                bias.append(bias_ref[h, variant])
        s = _each(lambda q_, k_, b_: _mm_nt(q_, k_) * scale + b_, q, k, bias)
        p = _each(lambda s_: jnp.exp(s_ - jnp.max(s_, axis=-1, keepdims=True)), s)
        l = _each(lambda p_: jnp.sum(p_, axis=-1, keepdims=True), p)
        o = _each(lambda p_, v_, l_: _mm(p_, v_) / l_, p, v, l)
        for (a, cols), o_ in zip(units, o):
            o_ref[a * GRID_W:(a + 1) * GRID_W, cols] = o_.astype(o_ref.dtype)


def _na(na_qkv, bias, tables):
    t = na_qkv.shape[0]
    nb = t // MIX_BLOCK
    blk = (MIX_BLOCK, NA_W)
    cur = lambda col: pl.BlockSpec(blk, lambda g, f, l, p, n: (g, col))
    prv = lambda col: pl.BlockSpec(blk, lambda g, f, l, p, n: (p[g], col))
    nxt = lambda col: pl.BlockSpec(blk, lambda g, f, l, p, n: (n[g], col))
    grid_spec = pltpu.PrefetchScalarGridSpec(
        num_scalar_prefetch=4,
        grid=(nb,),
        in_specs=[cur(0), prv(1), cur(1), nxt(1), prv(2), cur(2), nxt(2),
                  pl.BlockSpec((NA_HEADS, NA_WIN_ROWS, GRID_W, NA_KEYS), lambda g, f, l, p, n: (0, 0, 0, 0))],
        out_specs=pl.BlockSpec(blk, lambda g, f, l, p, n: (g, 0)),
        scratch_shapes=[pltpu.VMEM((3 * MIX_BLOCK, NA_W), BF16), pltpu.VMEM((3 * MIX_BLOCK, NA_W), BF16)],
    )
    return pl.pallas_call(
        _na_kernel,
        grid_spec=grid_spec,
        out_shape=jax.ShapeDtypeStruct((t, NA_W), BF16),
        compiler_params=_params("arbitrary"),
        name="na",
    )(*tables, na_qkv, na_qkv, na_qkv, na_qkv, na_qkv, na_qkv, na_qkv, bias)


HALO = 8


def _head_sum_matrix(width):
    r = lax.broadcasted_iota(jnp.int32, (width, width), 0) // HEAD_DIM
    c = lax.broadcasted_iota(jnp.int32, (width, width), 1) // HEAD_DIM
    return jnp.where(r == c, 1.0, 0.0).astype(BF16)


def _gdn_prep_kernel(first_ref, last_ref, x_ref, xp_ref, xn_ref, w_ref, o_ref):
    g = pl.program_id(0)
    tb = x_ref.shape[0]
    prev = jnp.where(first_ref[g] == 1, 0.0, xp_ref[...])
    nxt = jnp.where(last_ref[g] == 1, 0.0, xn_ref[...])
    xe = jnp.concatenate([prev, x_ref[...], nxt], axis=0)
    n = tb + 2 * HALO
    acc = None
    for j in range(GDN_CONV):
        shift = (GDN_CONV // 2 - j) % n
        tap = xe if shift == 0 else pltpu.roll(xe, shift, 0)
        term = tap[HALO:HALO + tb] * w_ref[j:j + 1, :]
        acc = term if acc is None else acc + term
    y = acc * _sigmoid(acc)
    ones_bd = _head_sum_matrix(GDN_W)
    q = y[:, 0:GDN_W]
    k = y[:, GDN_W:2 * GDN_W]
    qn = q * lax.rsqrt(_split_mm(q * q, ones_bd, 2) + EPS) * (HEAD_DIM ** -0.5)
    kn = k * lax.rsqrt(_split_mm(k * k, ones_bd, 2) + EPS)
    o_ref[:, 0:GDN_W] = qn
    o_ref[:, GDN_W:2 * GDN_W] = kn
    o_ref[:, 2 * GDN_W:3 * GDN_W] = y[:, 2 * GDN_W:3 * GDN_W]


def _gdn_prep(gdn_qkv, conv_w, tables):
    t, c = gdn_qkv.shape
    nb = t // MIX_BLOCK
    per = MIX_BLOCK // HALO
    n_halo = t // HALO
    first, last = tables[0], tables[1]
    grid_spec = pltpu.PrefetchScalarGridSpec(
        num_scalar_prefetch=2,
        grid=(nb,),
        in_specs=[
            pl.BlockSpec((MIX_BLOCK, c), lambda g, f, l: (g, 0)),
            pl.BlockSpec((HALO, c), lambda g, f, l: (jnp.maximum(g * per - 1, 0), 0)),
            pl.BlockSpec((HALO, c), lambda g, f, l: (jnp.minimum((g + 1) * per, n_halo - 1), 0)),
            pl.BlockSpec((8, c), lambda g, f, l: (0, 0)),
        ],
        out_specs=pl.BlockSpec((MIX_BLOCK, c), lambda g, f, l: (g, 0)),
    )
    w8 = jnp.concatenate([conv_w.astype(F32), jnp.zeros((8 - GDN_CONV, c), F32)], axis=0)
    return pl.pallas_call(
        _gdn_prep_kernel,
        grid_spec=grid_spec,
        out_shape=jax.ShapeDtypeStruct((t, c), F32),
        compiler_params=_params("arbitrary"),
        name="gdn_prep",
    )(first, last, gdn_qkv, gdn_qkv, gdn_qkv, w8)


STK = 2 * CHUNK


class _Masks:
    def __init__(self, backward):
        r = lax.broadcasted_iota(jnp.int32, (STK, STK), 0)
        c = lax.broadcasted_iota(jnp.int32, (STK, STK), 1)
        same = (r // CHUNK) == (c // CHUNK)
        if backward:
            self.incl = jnp.logical_and(same, c >= r)
            self.strict = jnp.logical_and(same, c > r)
        else:
            self.incl = jnp.logical_and(same, c <= r)
            self.strict = jnp.logical_and(same, c < r)
        self.same = same
        self.blk16 = (r // 16) == (c // 16)
        self.eye = jnp.where(r == c, 1.0, 0.0).astype(F32)
        self.last_row = 0 if backward else CHUNK - 1


def _cumsum_matrix(backward):
    r = lax.broadcasted_iota(jnp.int32, (CHUNK, CHUNK), 0)
    c = lax.broadcasted_iota(jnp.int32, (CHUNK, CHUNK), 1)
    keep = (c >= r) if backward else (c <= r)
    return jnp.where(keep, 1.0, 0.0).astype(BF16)


def _lane_half():
    return lax.broadcasted_iota(jnp.int32, (CHUNK, PAIR_W), 1) < HEAD_DIM


def _expand(tile, c0, c1, m0):
    return jnp.where(m0, tile[:, c0:c0 + 1], tile[:, c1:c1 + 1])


def _stack_cols(tile, c0, c1):
    top = jnp.broadcast_to(tile[:, c0:c0 + 1], (CHUNK, STK))
    bot = jnp.broadcast_to(tile[:, c1:c1 + 1], (CHUNK, STK))
    return jnp.concatenate([top, bot], axis=0)


def _stack(x, m0):
    return jnp.concatenate([jnp.where(m0, x, 0.0), jnp.where(m0, 0.0, x)], axis=0)


def _unstack(x):
    return x[:CHUNK] + x[CHUNK:]


def _each(fn, *lists):
    return [fn(*args) for args in zip(*lists)]


def _unit_tri_inverse(a_list, mk_list):
    eye = mk_list[0].eye
    d = _each(lambda a, mk: jnp.where(mk.blk16, a, 0.0), a_list, mk_list)
    rest = _each(lambda a, x: a - x, a_list, d)
    d2 = _each(lambda x: _mm(x, x), d)
    d4 = _each(lambda x: _mm(x, x), d2)
    d8 = _each(lambda x: _mm(x, x), d4)
    td = _each(lambda x, y: _mm(eye - x, eye + y), d, d2)
    td = _each(lambda x, y: _mm(x, eye + y), td, d4)
    td = _each(lambda x, y: _mm(x, eye + y), td, d8)
    n = _each(_mm, td, rest)
    n2 = _each(lambda x: _mm(x, x), n)
    t = _each(lambda x, y: _mm(eye - x, eye + y), n, n2)
    return _each(_mm, t, td)


LOCAL_GROUP = 2
DIRS = (0, 1)
CHAINS = [(d, p) for d in DIRS for p in range(N_PAIRS)]


def _pair_lanes(p, group=0, width=GDN_W):
    return slice(group * width + p * PAIR_W, group * width + (p + 1) * PAIR_W)


def _chunk_rows(c):
    return pl.ds(pl.multiple_of(c * CHUNK, CHUNK), CHUNK)


def _gdn_local(mks, m0, x_refs, g_refs, gp_ref, cs, chunk_ids, u_ref, wq_ref, kd_ref, p_ref, egl_ref):
    bias_row = gp_ref[0:1, :]
    a_row = -jnp.exp(gp_ref[1:2, :])
    tiles = {}
    for d in DIRS:
        for j, c in enumerate(chunk_ids):
            gates = g_refs[d][_chunk_rows(c), :]
            log_decay = a_row * _softplus(gates + bias_row)
            tiles[d, j] = (_split_mm_left(cs[d], log_decay, 3), _sigmoid(gates))
    order = [(d, j, p) for j in range(len(chunk_ids)) for d in DIRS for p in range(N_PAIRS)]
    ch = [(d, chunk_ids[j], p) for d, j, p in order]
    mk = [mks[d] for d, j, p in order]
    col = [d * GDN_HEADS + 2 * p for d, j, p in order]
    gc = [tiles[d, j][0] for d, j, p in order]
    beta = [tiles[d, j][1] for d, j, p in order]
    q2 = [x_refs[d][_chunk_rows(c), _pair_lanes(p, 0)] for d, c, p in ch]
    k2 = [x_refs[d][_chunk_rows(c), _pair_lanes(p, 1)] for d, c, p in ch]
    v2 = [x_refs[d][_chunk_rows(c), _pair_lanes(p, 2)] for d, c, p in ch]

    gc2 = _each(lambda t, c0: _expand(t, c0, c0 + 1, m0), gc, col)
    b2 = _each(lambda t, c0: _expand(t, GATE_GROUP + c0, GATE_GROUP + c0 + 1, m0), beta, col)
    gl = _each(lambda g, m: g[m.last_row:m.last_row + 1, :], gc2, mk)
    eg2 = _each(jnp.exp, gc2)
    kb2 = _each(lambda k, b: k * b, k2, b2)
    vb2 = _each(lambda v, b: v * b, v2, b2)
    kbg2 = _each(lambda k, e: k * e, kb2, eg2)
    qg2 = _each(lambda q, e: q * e, q2, eg2)
    kdec2 = _each(lambda k, l, g: k * jnp.exp(l - g), k2, gl, gc2)
    cst = _each(lambda t, c0: _stack_cols(t, c0, c0 + 1), gc, col)
    decay = _each(lambda c_, m: jnp.exp(jnp.where(m.incl, c_ - c_.T, NEG)), cst, mk)
    aq = _each(lambda kb, q, k: _mm_nt(jnp.concatenate([kb, kb, q, q], axis=0), _stack(k, m0)), kb2, q2, k2)
    a = _each(lambda x, dc, m: jnp.where(m.strict, x[:STK] * dc, 0.0), aq, decay, mk)
    pm = _each(lambda x, dc: x[STK:] * dc, aq, decay)
    t = _unit_tri_inverse(a, mk)
    uw = _each(lambda t_, vb, kbg: _mm(t_, jnp.concatenate([_stack(vb, m0), _stack(kbg, m0)], axis=1)),
               t, vb2, kbg2)
    for i, (d, c, p) in enumerate(ch):
        lanes = _pair_lanes(p)
        u_ref[d, c, :, lanes] = _unstack(uw[i][:, :PAIR_W])
        wq_ref[d, c, 0:CHUNK, lanes] = _bf(_unstack(uw[i][:, PAIR_W:]))
        wq_ref[d, c, CHUNK:STK, lanes] = _bf(qg2[i])
        kd_ref[d, c, :, lanes] = _bf(kdec2[i])
        p_ref[d, c, p] = _bf(pm[i])
        egl_ref[d, c, :, lanes] = jnp.exp(gl[i])


def _gdn_step(mks, m0, cidx, o_refs, s_ref, u_ref, wq_ref, kd_ref, p_ref, egl_ref):
    s = [s_ref[d, p] for d, p in CHAINS]
    wq = [_mm(wq_ref[d, cidx[d], :, _pair_lanes(p)], s_) for (d, p), s_ in zip(CHAINS, s)]
    vn = [u_ref[d, cidx[d], :, _pair_lanes(p)] - x[:CHUNK] for (d, p), x in zip(CHAINS, wq)]
    pv = [_mm(p_ref[d, cidx[d], p], _stack(v, m0)) for (d, p), v in zip(CHAINS, vn)]
    kv = [_mm_tn(kd_ref[d, cidx[d], :, _pair_lanes(p)], v) for (d, p), v in zip(CHAINS, vn)]
    for i, (d, p) in enumerate(CHAINS):
        o_refs[d][_chunk_rows(cidx[d]), _pair_lanes(p)] = wq[i][CHUNK:] + _unstack(pv[i])
        egl = egl_ref[d, cidx[d], :, _pair_lanes(p)]
        s_ref[d, p] = egl * s[i] + jnp.where(mks[d].same, kv[i], 0.0)


def _gdn_kernel(first_ref, last_ref, xf_ref, xb_ref, gf_ref, gb_ref, gp_ref, of_ref, ob_ref,
                s_ref, u_ref, wq_ref, kd_ref, p_ref, egl_ref):
    i = pl.program_id(0)
    nb = pl.num_programs(0)
    n_chunks = xf_ref.shape[0] // CHUNK

    @pl.when(first_ref[i] == 1)
    def _():
        s_ref[0] = jnp.zeros_like(s_ref[0])

    @pl.when(last_ref[nb - 1 - i] == 1)
    def _():
        s_ref[1] = jnp.zeros_like(s_ref[1])

    m0 = _lane_half()
    mks = [_Masks(backward=False), _Masks(backward=True)]
    cs = [_cumsum_matrix(backward=False), _cumsum_matrix(backward=True)]
    scratch = (u_ref, wq_ref, kd_ref, p_ref, egl_ref)

    def local_body(g, carry):
        chunk_ids = [g * LOCAL_GROUP + j for j in range(LOCAL_GROUP)]
        _gdn_local(mks, m0, (xf_ref, xb_ref), (gf_ref, gb_ref), gp_ref, cs, chunk_ids, *scratch)
        return carry

    lax.fori_loop(0, n_chunks // LOCAL_GROUP, local_body, 0)

    def step_body(c, carry):
        _gdn_step(mks, m0, (c, n_chunks - 1 - c), (of_ref, ob_ref), s_ref, *scratch)
        return carry

    lax.fori_loop(0, n_chunks, step_body, 0)


def _gdn(gqkv, gates, gate_params, tables):
    t = gqkv.shape[0]
    nb = t // MIX_BLOCK
    nc = MIX_BLOCK // CHUNK
    first, last = tables[0], tables[1]
    fwd = lambda w: pl.BlockSpec((MIX_BLOCK, w), lambda i, f, l: (i, 0))
    bwd = lambda w: pl.BlockSpec((MIX_BLOCK, w), lambda i, f, l: (nb - 1 - i, 0))
    grid_spec = pltpu.PrefetchScalarGridSpec(
        num_scalar_prefetch=2,
        grid=(nb,),
        in_specs=[fwd(3 * GDN_W), bwd(3 * GDN_W), fwd(GATE_W), bwd(GATE_W),
                  pl.BlockSpec((8, GATE_W), lambda i, f, l: (0, 0))],
        out_specs=[fwd(GDN_W), bwd(GDN_W)],
        scratch_shapes=[pltpu.VMEM((2, N_PAIRS, STK, STK), F32),
                        pltpu.VMEM((2, nc, CHUNK, GDN_W), F32),
                        pltpu.VMEM((2, nc, STK, GDN_W), BF16),
                        pltpu.VMEM((2, nc, CHUNK, GDN_W), BF16),
                        pltpu.VMEM((2, nc, N_PAIRS, STK, STK), BF16),
                        pltpu.VMEM((2, nc, 1, GDN_W), F32)],
    )
    return pl.pallas_call(
        _gdn_kernel,
        grid_spec=grid_spec,
        out_shape=[jax.ShapeDtypeStruct((t, GDN_W), F32)] * 2,
        compiler_params=_params("arbitrary"),
        name="gdn",
    )(first, last, gqkv, gqkv, gates, gates, gate_params)


def _mlstm_local(mks, m0, bd2, x_refs, g_refs, gp_ref, cs, chunk_ids, upd_ref, intra_ref, mi_ref, fc_ref, fl_ref,
                 ml_ref):
    bias_row = gp_ref[0:1, :]
    tiles = {}
    for d in DIRS:
        for j, c in enumerate(chunk_ids):
            li = g_refs[d][_chunk_rows(c), :] + bias_row
            tiles[d, j] = (_split_mm_left(cs[d], -_softplus(-li), 3), li)
    order = [(d, j, p) for j in range(len(chunk_ids)) for d in DIRS for p in range(N_PAIRS)]
    ch = [(d, chunk_ids[j], p) for d, j, p in order]
    mk = [mks[d] for d, j, p in order]
    col_i = [2 * GATE_GROUP + d * MLSTM_HEADS + 2 * p for d, j, p in order]
    col_f = [3 * GATE_GROUP + d * MLSTM_HEADS + 2 * p for d, j, p in order]
    fc = [tiles[d, j][0] for d, j, p in order]
    li = [tiles[d, j][1] for d, j, p in order]
    q2 = [x_refs[d][_chunk_rows(c), _pair_lanes(p, 0, MLSTM_W)] for d, c, p in ch]
    k2 = [x_refs[d][_chunk_rows(c), _pair_lanes(p, 1, MLSTM_W)] * (HEAD_DIM ** -0.5) for d, c, p in ch]
    v2 = [x_refs[d][_chunk_rows(c), _pair_lanes(p, 2, MLSTM_W)] for d, c, p in ch]
    ones = jnp.ones((CHUNK, PAIR_W), F32)
    ones_st = _stack(ones, m0)

    fc2 = _each(lambda t, c0: _expand(t, c0, c0 + 1, m0), fc, col_f)
    li2 = _each(lambda t, c0: _expand(t, c0, c0 + 1, m0), li, col_i)
    fl = _each(lambda f, m: f[m.last_row:m.last_row + 1, :], fc2, mk)
    g_end = _each(lambda l, f, i_: l - f + i_, fl, fc2, li2)
    ml = _each(lambda g: jnp.max(g, axis=0, keepdims=True), g_end)
    kw2 = _each(lambda k, g, m: k * jnp.exp(g - m), k2, g_end, ml)
    upd = _each(lambda kw, v: jnp.where(bd2, _mm_tn(kw, jnp.concatenate([v, ones], axis=1)), 0.0), kw2, v2)
    cst = _each(lambda t, c0: _stack_cols(t, c0, c0 + 1), fc, col_f)
    est = _each(lambda c_, t, c0: c_ - _stack_cols(t, c0, c0 + 1), cst, li, col_i)
    dmat = _each(lambda c_, e, m: jnp.where(m.incl, c_ - e.T, NEG), cst, est, mk)
    m_intra = _each(lambda x: jnp.max(x, axis=1, keepdims=True), dmat)
    qk = _each(lambda q, k: _mm_nt(jnp.concatenate([q, q], axis=0), _stack(k, m0)), q2, k2)
    s_loc = _each(lambda x, dm, mi: x * jnp.exp(dm - mi), qk, dmat, m_intra)
    intra = _each(lambda s, v: _unstack(_mm(s, jnp.concatenate([_stack(v, m0), ones_st], axis=1))), s_loc, v2)
    mi2 = _each(lambda mi: jnp.where(m0, jnp.broadcast_to(mi[:CHUNK], (CHUNK, PAIR_W)),
                                     jnp.broadcast_to(mi[CHUNK:], (CHUNK, PAIR_W))), m_intra)
    for i, (d, c, p) in enumerate(ch):
        lanes = _pair_lanes(p)
        upd_ref[d, c, p] = upd[i]
        intra_ref[d, c, p] = intra[i]
        mi_ref[d, c, :, lanes] = mi2[i]
        fc_ref[d, c, :, lanes] = fc2[i]
        fl_ref[d, c, :, lanes] = fl[i]
        ml_ref[d, c, :, lanes] = ml[i]


def _mlstm_step(cidx, x_refs, o_refs, c_ref, m_ref, upd_ref, intra_ref, mi_ref, fc_ref, fl_ref, ml_ref):
    for d, p in CHAINS:
        c = cidx[d]
        lanes = _pair_lanes(p)
        m_st = m_ref[d, p]
        cn = c_ref[d, p]
        fl = fl_ref[d, c, :, lanes]
        ml = ml_ref[d, c, :, lanes]
        m_new = jnp.maximum(fl + m_st, ml)
        a_row = jnp.exp(fl + m_st - m_new)
        b_row = jnp.exp(ml - m_new)
        m_inter = fc_ref[d, c, :, lanes] + m_st
        mi2 = mi_ref[d, c, :, lanes]
        m_t = jnp.maximum(m_inter, mi2)
        w_intra = jnp.exp(mi2 - m_t)
        w_inter = jnp.exp(m_inter - m_t)
        inter = _mm(x_refs[d][_chunk_rows(c), _pair_lanes(p, 0, MLSTM_W)], cn)
        intra = intra_ref[d, c, p]
        num = w_intra * intra[:, :PAIR_W] + w_inter * inter[:, :PAIR_W]
        den = w_intra * intra[:, PAIR_W:] + w_inter * inter[:, PAIR_W:]
        o_refs[d][_chunk_rows(c), lanes] = num / jnp.maximum(jnp.abs(den), jnp.exp(-m_t))
        c_ref[d, p] = (jnp.concatenate([a_row, a_row], axis=1) * cn
                       + jnp.concatenate([b_row, b_row], axis=1) * upd_ref[d, c, p])
        m_ref[d, p] = m_new


def _mlstm_kernel(first_ref, last_ref, xf_ref, xb_ref, gf_ref, gb_ref, gp_ref, of_ref, ob_ref,
                  c_ref, m_ref, upd_ref, intra_ref, mi_ref, fc_ref, fl_ref, ml_ref):
    i = pl.program_id(0)
    nb = pl.num_programs(0)
    n_chunks = xf_ref.shape[0] // CHUNK

    @pl.when(first_ref[i] == 1)
    def _():
        c_ref[0] = jnp.zeros_like(c_ref[0])
        m_ref[0] = jnp.full_like(m_ref[0], M_INIT)

    @pl.when(last_ref[nb - 1 - i] == 1)
    def _():
        c_ref[1] = jnp.zeros_like(c_ref[1])
        m_ref[1] = jnp.full_like(m_ref[1], M_INIT)

    m0 = _lane_half()
    r = lax.broadcasted_iota(jnp.int32, (STK, 2 * STK), 0) // HEAD_DIM
    c = (lax.broadcasted_iota(jnp.int32, (STK, 2 * STK), 1) % STK) // HEAD_DIM
    bd2 = r == c
    mks = [_Masks(backward=False), _Masks(backward=True)]
    cs = [_cumsum_matrix(backward=False), _cumsum_matrix(backward=True)]
    scratch = (upd_ref, intra_ref, mi_ref, fc_ref, fl_ref, ml_ref)

    def local_body(g, carry):
        chunk_ids = [g * LOCAL_GROUP + j for j in range(LOCAL_GROUP)]
        _mlstm_local(mks, m0, bd2, (xf_ref, xb_ref), (gf_ref, gb_ref), gp_ref, cs, chunk_ids, *scratch)
        return carry

    lax.fori_loop(0, n_chunks // LOCAL_GROUP, local_body, 0)

    def step_body(cc, carry):
        _mlstm_step((cc, n_chunks - 1 - cc), (xf_ref, xb_ref), (of_ref, ob_ref), c_ref, m_ref, *scratch)
        return carry

    lax.fori_loop(0, n_chunks, step_body, 0)


def _mlstm(ml_qkv, gates, gate_params, tables):
    t = ml_qkv.shape[0]
    nb = t // MIX_BLOCK
    nc = MIX_BLOCK // CHUNK
    first, last = tables[0], tables[1]
    fwd = lambda w: pl.BlockSpec((MIX_BLOCK, w), lambda i, f, l: (i, 0))
    bwd = lambda w: pl.BlockSpec((MIX_BLOCK, w), lambda i, f, l: (nb - 1 - i, 0))
    grid_spec = pltpu.PrefetchScalarGridSpec(
        num_scalar_prefetch=2,
        grid=(nb,),
        in_specs=[fwd(3 * MLSTM_W), bwd(3 * MLSTM_W), fwd(GATE_W), bwd(GATE_W),
                  pl.BlockSpec((8, GATE_W), lambda i, f, l: (0, 0))],
        out_specs=[fwd(MLSTM_W), bwd(MLSTM_W)],
        scratch_shapes=[pltpu.VMEM((2, N_PAIRS, STK, 2 * STK), F32),
                        pltpu.VMEM((2, N_PAIRS, 1, PAIR_W), F32),
                        pltpu.VMEM((2, nc, N_PAIRS, STK, 2 * STK), F32),
                        pltpu.VMEM((2, nc, N_PAIRS, CHUNK, 2 * PAIR_W), F32),
                        pltpu.VMEM((2, nc, CHUNK, MLSTM_W), F32),
                        pltpu.VMEM((2, nc, CHUNK, MLSTM_W), F32),
                        pltpu.VMEM((2, nc, 1, MLSTM_W), F32),
                        pltpu.VMEM((2, nc, 1, MLSTM_W), F32)],
    )
    return pl.pallas_call(
        _mlstm_kernel,
        grid_spec=grid_spec,
        out_shape=[jax.ShapeDtypeStruct((t, MLSTM_W), F32)] * 2,
        compiler_params=_params("arbitrary"),
        name="mlstm",
    )(first, last, ml_qkv, ml_qkv, gates, gates, gate_params)


OUT_TM = 512


def _outproj_kernel(x_ref, na_ref, gf_ref, gb_ref, z_ref, hf_ref, hb_ref, o_ref, gn_ref, mn_ref,
                    w_ref, y_ref):
    ones_bd = _head_sum_matrix(GDN_W)

    def head_rms(v, gain):
        ms = _split_mm(v * v, ones_bd, 2) * (1.0 / HEAD_DIM)
        return v * lax.rsqrt(ms + EPS) * gain

    go = head_rms(gf_ref[...] + gb_ref[...], gn_ref[...])
    z = z_ref[...]
    gdn_out = go * (z * _sigmoid(z))
    mh = head_rms(hf_ref[...] + hb_ref[...], mn_ref[...])
    ml_out = _sigmoid(o_ref[...]) * mh
    y = x_ref[...]
    y = y + jnp.dot(na_ref[...], w_ref[0:NA_W, :], preferred_element_type=F32)
    y = y + jnp.dot(_bf(gdn_out), w_ref[NA_W:NA_W + GDN_W, :], preferred_element_type=F32)
    y = y + jnp.dot(_bf(ml_out), w_ref[NA_W + GDN_W:, :], preferred_element_type=F32)
    y_ref[...] = y


def _outproj(x, na_out, gf, gb, z, hf, hb, o, gdn_norm, mlstm_norm, w_out):
    t = x.shape[0]
    tm = min(OUT_TM, t)
    tok = lambda w: pl.BlockSpec((tm, w), lambda i: (i, 0))
    const = lambda r, w: pl.BlockSpec((r, w), lambda i: (0, 0))
    gn = jnp.tile(gdn_norm.astype(F32), GDN_HEADS).reshape(1, GDN_W)
    mn = mlstm_norm.astype(F32).reshape(1, MLSTM_W)
    return pl.pallas_call(
        _outproj_kernel,
        grid=(t // tm,),
        in_specs=[tok(D_MODEL), tok(NA_W), tok(GDN_W), tok(GDN_W), tok(GDN_W), tok(MLSTM_W), tok(MLSTM_W),
                  tok(MLSTM_W), const(1, GDN_W), const(1, MLSTM_W), const(D_MODEL, D_MODEL)],
        out_specs=tok(D_MODEL),
        out_shape=jax.ShapeDtypeStruct((t, D_MODEL), F32),
        compiler_params=_params("parallel"),
        name="outproj",
    )(x, na_out, gf, gb, z, hf, hb, o, gn, mn, _bf(w_out))


def _gate_params(bias_rows, a_log):
    pad = GATE_W - 4 * GATE_GROUP
    row0 = jnp.concatenate([b.astype(F32).reshape(-1) for b in bias_rows] + [jnp.zeros((pad,), F32)])
    row1 = jnp.concatenate([a_log.astype(F32).reshape(-1), jnp.zeros((GATE_W - GATE_GROUP,), F32)])
    return jnp.concatenate([row0[None], row1[None], jnp.zeros((6, GATE_W), F32)], axis=0)


def _trunk(x, seq_lens, p):
    depth = p["ffn1_norm"].shape[0]
    tables = tuple(jnp.asarray(a) for a in _block_tables(seq_lens, MIX_BLOCK))
    zeros_g = jnp.zeros((2, GDN_HEADS), F32)
    for i in range(depth):
        x = _ffn(x, p["ffn1_norm"][i], _bf(p["ffn1_w_in"][i]), _bf(p["ffn1_w_out"][i]), p["final_norm"], False)
        na_qkv, gdn_qkv, gdn_z, ml_qkv, ml_o, gates = _inproj(x, p["mix_norm"][i], _permute_w_in(p["w_in"][i]))
        na_out = _na(na_qkv, _na_bias_table(p["na_rpb"][i]), tables)
        gqkv = _gdn_prep(gdn_qkv, p["gdn_conv"][i], tables)
        gdn_gp = _gate_params([p["gdn_dt_bias"][i], zeros_g, zeros_g, zeros_g], p["gdn_a_log"][i])
        gf, gb = _gdn(gqkv, gates, gdn_gp, tables)
        ml_gp = _gate_params([zeros_g, zeros_g, p["mlstm_i_bias"][i], p["mlstm_f_bias"][i]], zeros_g)
        hf, hb = _mlstm(ml_qkv, gates, ml_gp, tables)
        x = _outproj(x, na_out, gf, gb, gdn_z, hf, hb, ml_o, p["gdn_norm"][i], p["mlstm_norm"][i], p["w_out"][i])
        x = _ffn(x, p["ffn2_norm"][i], _bf(p["ffn2_w_in"][i]), _bf(p["ffn2_w_out"][i]), p["final_norm"],
                 i == depth - 1)
    return x


def kernel(x_prompt, x_sample, ffn1_norm, ffn1_w_in, ffn1_w_out, mix_norm, w_in, w_out, na_rpb, gdn_conv,
           gdn_a_log, gdn_dt_bias, gdn_norm, mlstm_i_bias, mlstm_f_bias, mlstm_norm, ffn2_norm, ffn2_w_in,
           ffn2_w_out, final_norm):
    p = dict(ffn1_norm=ffn1_norm, ffn1_w_in=ffn1_w_in, ffn1_w_out=ffn1_w_out, mix_norm=mix_norm, w_in=w_in,
             w_out=w_out, na_rpb=na_rpb, gdn_conv=gdn_conv, gdn_a_log=gdn_a_log, gdn_dt_bias=gdn_dt_bias,
             gdn_norm=gdn_norm, mlstm_i_bias=mlstm_i_bias, mlstm_f_bias=mlstm_f_bias, mlstm_norm=mlstm_norm,
             ffn2_norm=ffn2_norm, ffn2_w_in=ffn2_w_in, ffn2_w_out=ffn2_w_out, final_norm=final_norm)
    bp, lp, _ = x_prompt.shape
    bs, ls, _ = x_sample.shape
    x = jnp.concatenate([x_prompt.reshape(bp * lp, D_MODEL), x_sample.reshape(bs * ls, D_MODEL)], axis=0)
    y = _trunk(x, (lp,) * bp + (ls,) * bs, p)
    return (y[:bp * lp].reshape(bp, lp, D_MODEL), y[bp * lp:].reshape(bs, ls, D_MODEL))
```

```python
import functools

import jax
import jax.numpy as jnp
import numpy as np
from jax import lax
from jax.experimental import pallas as pl
from jax.experimental.pallas import tpu as pltpu

D_MODEL = 1024
D_FF = 2816
HEAD_DIM = 64
NA_HEADS = 4
GDN_HEADS = 6
MLSTM_HEADS = 6
NA_W = NA_HEADS * HEAD_DIM
GDN_W = GDN_HEADS * HEAD_DIM
MLSTM_W = MLSTM_HEADS * HEAD_DIM
GRID_W = 64
NA_WIN_ROWS = 8
NA_WIN_COLS = 16
GDN_CONV = 5
CHUNK = 64
M_INIT = -1e30
EPS = 1e-6
NEG = -1e30

LANES = 128
PAIR_W = 2 * HEAD_DIM
N_PAIRS = GDN_HEADS // 2
GATE_W = LANES
GATE_GROUP = 2 * GDN_HEADS
MIX_BLOCK = 512
VMEM_LIMIT = 56 * 1024 * 1024

F32 = jnp.float32
BF16 = jnp.bfloat16


def _bf(x):
    return x.astype(BF16)


def _mm(a, b):
    return jnp.dot(_bf(a), _bf(b), preferred_element_type=F32)


def _mm_nt(a, b):
    return lax.dot_general(_bf(a), _bf(b), (((1,), (1,)), ((), ())), preferred_element_type=F32)


def _mm_tn(a, b):
    return lax.dot_general(_bf(a), _bf(b), (((0,), (0,)), ((), ())), preferred_element_type=F32)


def _split_mm(a, b_exact, parts):
    acc = None
    rem = a
    for _ in range(parts):
        piece = _bf(rem)
        term = jnp.dot(piece, b_exact, preferred_element_type=F32)
        acc = term if acc is None else acc + term
        rem = rem - piece.astype(F32)
    return acc


def _split_mm_left(a_exact, b, parts):
    acc = None
    rem = b
    for _ in range(parts):
        piece = _bf(rem)
        term = jnp.dot(a_exact, piece, preferred_element_type=F32)
        acc = term if acc is None else acc + term
        rem = rem - piece.astype(F32)
    return acc


def _sigmoid(x):
    return 1.0 / (1.0 + jnp.exp(-x))


def _softplus(x):
    return jnp.maximum(x, 0.0) + jnp.log(1.0 + jnp.exp(-jnp.abs(x)))


def _rms(x, gain):
    return x * lax.rsqrt(jnp.mean(x * x, axis=-1, keepdims=True) + EPS) * gain


def _params(*sem):
    return pltpu.CompilerParams(dimension_semantics=sem, vmem_limit_bytes=VMEM_LIMIT)


FFN_TM = 1024
FFN_SUB = 512
FFN_TF = 256


def _ffn_kernel(*refs, final, sub, in_blocks, out_blocks):
    n_in, n_out = len(in_blocks), len(out_blocks)
    x_refs = refs[:n_in]
    g_ref, wi_ref, wo_ref, fn_ref = refs[n_in:n_in + 4]
    o_refs = refs[n_in + 4:n_in + 4 + n_out]
    act_ref = refs[-1]
    i = pl.program_id(0)
    in_starts = np.cumsum((0,) + tuple(in_blocks))
    out_starts = np.cumsum((0,) + tuple(out_blocks))
    for s in range(x_refs[0].shape[0] // sub):
        rows = slice(s * sub, (s + 1) * sub)
        x = x_refs[-1][rows, :]
        for k in range(n_in - 2, -1, -1):
            x = jnp.where(i < in_starts[k + 1], x_refs[k][rows, :], x)
        xn = _bf(_rms(x, g_ref[...]))
        for j in range(D_FF // FFN_TF):
            gate = jnp.dot(xn, wi_ref[:, j * FFN_TF:(j + 1) * FFN_TF], preferred_element_type=F32)
            up = jnp.dot(xn, wi_ref[:, D_FF + j * FFN_TF:D_FF + (j + 1) * FFN_TF], preferred_element_type=F32)
            act_ref[rows, j * FFN_TF:(j + 1) * FFN_TF] = _bf(gate * _sigmoid(gate) * up)
        y = x + 0.5 * jnp.dot(act_ref[rows, :], wo_ref[...], preferred_element_type=F32)
        if final:
            y = _rms(y, fn_ref[...])
        if n_out == 1:
            o_refs[0][rows, :] = y
        else:
            for k in range(n_out):
                @pl.when(jnp.logical_and(i >= out_starts[k], i < out_starts[k + 1]))
                def _(k=k, y=y):
                    o_refs[k][rows, :] = y


def _ffn(xs, gain, w_in, w_out, final_gain, final, out_sizes=None):
    sizes = [x.shape[0] for x in xs]
    t = sum(sizes)
    out_sizes = [t] if out_sizes is None else list(out_sizes)
    tm = min([FFN_TM] + sizes + out_sizes)
    sub = min(FFN_SUB, tm)
    in_blocks = [s // tm for s in sizes]
    out_blocks = [s // tm for s in out_sizes]

    def part_spec(blocks, k):
        start = int(np.sum(blocks[:k]))
        return pl.BlockSpec((tm, D_MODEL), lambda i: (jnp.clip(i - start, 0, blocks[k] - 1), 0))

    resident = lambda shape: pl.BlockSpec(shape, lambda i: (0, 0), pipeline_mode=pl.Buffered(1))
    outs = pl.pallas_call(
        functools.partial(_ffn_kernel, final=final, sub=sub, in_blocks=tuple(in_blocks),
                          out_blocks=tuple(out_blocks)),
        grid=(t // tm,),
        in_specs=[part_spec(in_blocks, k) for k in range(len(xs))] + [
            resident((1, D_MODEL)),
            resident((D_MODEL, 2 * D_FF)),
            resident((D_FF, D_MODEL)),
            resident((1, D_MODEL)),
        ],
        out_specs=[part_spec(out_blocks, k) for k in range(len(out_sizes))],
        out_shape=[jax.ShapeDtypeStruct((s, D_MODEL), F32) for s in out_sizes],
        scratch_shapes=[pltpu.VMEM((tm, D_FF), BF16)],
        compiler_params=_params("arbitrary"),
        name="ffn",
    )(*xs, gain.reshape(1, D_MODEL), w_in, w_out, final_gain.reshape(1, D_MODEL))
    return outs[0] if len(outs) == 1 else outs


PROJ_TM = 512
PROJ_WIDTHS = (3 * NA_W, 3 * GDN_W, GDN_W, 3 * MLSTM_W, MLSTM_W, GATE_W)
PROJ_DTYPES = (BF16, F32, F32, F32, F32, F32)
P_PERM = sum(PROJ_WIDTHS)


def _permute_w_in(w_in):
    o = np.cumsum([0, 3 * NA_W, 3 * GDN_W, GDN_W, 2 * GDN_HEADS, 2 * GDN_HEADS,
                   3 * MLSTM_W, MLSTM_W, 2 * MLSTM_HEADS, 2 * MLSTM_HEADS])
    n_gates = 4 * 2 * GDN_HEADS
    cols = [w_in[:, o[0]:o[3]], w_in[:, o[5]:o[7]], w_in[:, o[3]:o[5]], w_in[:, o[7]:o[9]],
            jnp.zeros((D_MODEL, GATE_W - n_gates), w_in.dtype)]
    return _bf(jnp.concatenate(cols, axis=1))


PROJ_DOT_GROUPS = ((0,), (1, 2), (3, 4), (5,))


def _inproj_kernel(x_ref, g_ref, w_ref, *o_refs):
    xn = _bf(_rms(x_ref[...], g_ref[...]))
    off = 0
    for group in PROJ_DOT_GROUPS:
        width = sum(PROJ_WIDTHS[k] for k in group)
        res = jnp.dot(xn, w_ref[:, off:off + width], preferred_element_type=F32)
        off += width
        col = 0
        for k in group:
            o_refs[k][...] = res[:, col:col + PROJ_WIDTHS[k]].astype(o_refs[k].dtype)
            col += PROJ_WIDTHS[k]


def _inproj(x, gain, w_perm):
    t = x.shape[0]
    tm = min(PROJ_TM, t)
    return pl.pallas_call(
        _inproj_kernel,
        grid=(t // tm,),
        in_specs=[
            pl.BlockSpec((tm, D_MODEL), lambda i: (i, 0)),
            pl.BlockSpec((1, D_MODEL), lambda i: (0, 0)),
            pl.BlockSpec((D_MODEL, P_PERM), lambda i: (0, 0)),
        ],
        out_specs=[pl.BlockSpec((tm, w), lambda i: (i, 0)) for w in PROJ_WIDTHS],
        out_shape=[jax.ShapeDtypeStruct((t, w), d) for w, d in zip(PROJ_WIDTHS, PROJ_DTYPES)],
        compiler_params=_params("parallel"),
        name="inproj",
    )(x, gain.reshape(1, D_MODEL), w_perm)


def _block_tables(seq_lens, block):
    first, last = [], []
    for n in seq_lens:
        nb = n // block
        first += [1] + [0] * (nb - 1)
        last += [0] * (nb - 1) + [1]
    first = np.asarray(first, np.int32)
    last = np.asarray(last, np.int32)
    idx = np.arange(len(first), dtype=np.int32)
    prev = np.where(first == 1, idx, idx - 1).astype(np.int32)
    nxt = np.where(last == 1, idx, idx + 1).astype(np.int32)
    return first, last, prev, nxt


NA_ROWS = MIX_BLOCK // GRID_W
NA_KEYS = NA_WIN_ROWS * GRID_W
NA_ROW_GROUP = 2


def _na_bias_table(rpb):
    c = np.arange(GRID_W)
    c_start = np.clip(c - NA_WIN_COLS // 2, 0, GRID_W - NA_WIN_COLS)
    in_win = (c[None, :] >= c_start[:, None]) & (c[None, :] < c_start[:, None] + NA_WIN_COLS)
    dc = np.clip(c[None, :] - c[:, None], 1 - NA_WIN_COLS, NA_WIN_COLS - 1) + NA_WIN_COLS - 1
    one_hot = jnp.asarray(dc[None] == np.arange(2 * NA_WIN_COLS - 1)[:, None, None], F32)
    by_col = jnp.einsum("hrc,cqk->hrqk", rpb.astype(F32), one_hot, precision=lax.Precision.HIGHEST)
    tab = jnp.stack([by_col[:, NA_WIN_ROWS - 1 - v:2 * NA_WIN_ROWS - 1 - v] for v in range(NA_WIN_ROWS)],
                    axis=1)
    tab = jnp.where(in_win[None, None, None], tab, NEG).transpose(0, 1, 3, 2, 4)
    return tab.reshape(NA_HEADS, NA_WIN_ROWS, GRID_W, NA_KEYS)


def _na_kernel(first_ref, last_ref, prev_ref, next_ref,
               q_ref, kp_ref, kc_ref, kn_ref, vp_ref, vc_ref, vn_ref, bias_ref, o_ref, kbuf, vbuf):
    del prev_ref, next_ref
    g = pl.program_id(0)
    kbuf[0:MIX_BLOCK] = kp_ref[...]
    kbuf[MIX_BLOCK:2 * MIX_BLOCK] = kc_ref[...]
    kbuf[2 * MIX_BLOCK:3 * MIX_BLOCK] = kn_ref[...]
    vbuf[0:MIX_BLOCK] = vp_ref[...]
    vbuf[MIX_BLOCK:2 * MIX_BLOCK] = vc_ref[...]
    vbuf[2 * MIX_BLOCK:3 * MIX_BLOCK] = vn_ref[...]
    scale = HEAD_DIM ** -0.5
    half = NA_WIN_ROWS // 2
    m0 = lax.broadcasted_iota(jnp.int32, (GRID_W, PAIR_W), 1) < HEAD_DIM
    for a0 in range(0, NA_ROWS, NA_ROW_GROUP):
        q, k, v, bias = [], [], [], []
        for a in range(a0, a0 + NA_ROW_GROUP):
            edge = (first_ref[g] if a < half else last_ref[g]) == 1
            start = pl.multiple_of(jnp.where(edge, NA_ROWS, a + half) * GRID_W, GRID_W)
            variant = jnp.where(edge, a, half)
            for h in range(NA_HEADS):
                lanes = slice((h // 2) * PAIR_W, (h // 2 + 1) * PAIR_W)
                q2 = q_ref[a * GRID_W:(a + 1) * GRID_W, lanes]
                q.append(jnp.where(m0 if h % 2 == 0 else jnp.logical_not(m0), q2, jnp.zeros_like(q2)))
                k.append(kbuf[pl.ds(start, NA_KEYS), lanes])
                v.append(vbuf[pl.ds(start, NA_KEYS), lanes])
                bias.append(bias_ref[h, variant])
        s = _each(lambda q_, k_, b_: _mm_nt(q_, k_) * scale + b_, q, k, bias)
        p = _each(lambda s_: jnp.exp(s_ - jnp.max(s_, axis=-1, keepdims=True)), s)
        l = _each(lambda p_: jnp.sum(p_, axis=-1, keepdims=True), p)
        o = _each(lambda p_, v_, l_: _mm(p_, v_) / l_, p, v, l)
        for j in range(0, len(o), 2):
            a = a0 + j // NA_HEADS
            lanes = slice(((j % NA_HEADS) // 2) * PAIR_W, ((j % NA_HEADS) // 2 + 1) * PAIR_W)
            o_ref[a * GRID_W:(a + 1) * GRID_W, lanes] = jnp.where(m0, o[j], o[j + 1]).astype(o_ref.dtype)


def _na(na_qkv, bias, tables):
    t = na_qkv.shape[0]
    nb = t // MIX_BLOCK
    blk = (MIX_BLOCK, NA_W)
    cur = lambda col: pl.BlockSpec(blk, lambda g, f, l, p, n: (g, col))
    prv = lambda col: pl.BlockSpec(blk, lambda g, f, l, p, n: (p[g], col))
    nxt = lambda col: pl.BlockSpec(blk, lambda g, f, l, p, n: (n[g], col))
    grid_spec = pltpu.PrefetchScalarGridSpec(
        num_scalar_prefetch=4,
        grid=(nb,),
        in_specs=[cur(0), prv(1), cur(1), nxt(1), prv(2), cur(2), nxt(2),
                  pl.BlockSpec((NA_HEADS, NA_WIN_ROWS, GRID_W, NA_KEYS), lambda g, f, l, p, n: (0, 0, 0, 0))],
        out_specs=pl.BlockSpec(blk, lambda g, f, l, p, n: (g, 0)),
        scratch_shapes=[pltpu.VMEM((3 * MIX_BLOCK, NA_W), BF16), pltpu.VMEM((3 * MIX_BLOCK, NA_W), BF16)],
    )
    return pl.pallas_call(
        _na_kernel,
        grid_spec=grid_spec,
        out_shape=jax.ShapeDtypeStruct((t, NA_W), BF16),
        compiler_params=_params("arbitrary"),
        name="na",
    )(*tables, na_qkv, na_qkv, na_qkv, na_qkv, na_qkv, na_qkv, na_qkv, bias)


HALO = 8


def _head_sum_matrix(width):
    r = lax.broadcasted_iota(jnp.int32, (width, width), 0) // HEAD_DIM
    c = lax.broadcasted_iota(jnp.int32, (width, width), 1) // HEAD_DIM
    return jnp.where(r == c, 1.0, 0.0).astype(BF16)


def _gdn_prep_kernel(first_ref, last_ref, x_ref, xp_ref, xn_ref, w_ref, o_ref):
    g = pl.program_id(0)
    tb = x_ref.shape[0]
    prev = jnp.where(first_ref[g] == 1, 0.0, xp_ref[...])
    nxt = jnp.where(last_ref[g] == 1, 0.0, xn_ref[...])
    xe = jnp.concatenate([prev, x_ref[...], nxt], axis=0)
    n = tb + 2 * HALO
    acc = None
    for j in range(GDN_CONV):
        shift = (GDN_CONV // 2 - j) % n
        tap = xe if shift == 0 else pltpu.roll(xe, shift, 0)
        term = tap[HALO:HALO + tb] * w_ref[j:j + 1, :]
        acc = term if acc is None else acc + term
    y = acc * _sigmoid(acc)
    ones_bd = _head_sum_matrix(GDN_W)
    q = y[:, 0:GDN_W]
    k = y[:, GDN_W:2 * GDN_W]
    qn = q * lax.rsqrt(_split_mm(q * q, ones_bd, 2) + EPS) * (HEAD_DIM ** -0.5)
    kn = k * lax.rsqrt(_split_mm(k * k, ones_bd, 2) + EPS)
    o_ref[:, 0:GDN_W] = qn
    o_ref[:, GDN_W:2 * GDN_W] = kn
    o_ref[:, 2 * GDN_W:3 * GDN_W] = y[:, 2 * GDN_W:3 * GDN_W]


def _gdn_prep(gdn_qkv, conv_w, tables):
    t, c = gdn_qkv.shape
    nb = t // MIX_BLOCK
    per = MIX_BLOCK // HALO
    n_halo = t // HALO
    first, last = tables[0], tables[1]
    grid_spec = pltpu.PrefetchScalarGridSpec(
        num_scalar_prefetch=2,
        grid=(nb,),
        in_specs=[
            pl.BlockSpec((MIX_BLOCK, c), lambda g, f, l: (g, 0)),
            pl.BlockSpec((HALO, c), lambda g, f, l: (jnp.maximum(g * per - 1, 0), 0)),
            pl.BlockSpec((HALO, c), lambda g, f, l: (jnp.minimum((g + 1) * per, n_halo - 1), 0)),
            pl.BlockSpec((8, c), lambda g, f, l: (0, 0)),
        ],
        out_specs=pl.BlockSpec((MIX_BLOCK, c), lambda g, f, l: (g, 0)),
    )
    w8 = jnp.concatenate([conv_w.astype(F32), jnp.zeros((8 - GDN_CONV, c), F32)], axis=0)
    return pl.pallas_call(
        _gdn_prep_kernel,
        grid_spec=grid_spec,
        out_shape=jax.ShapeDtypeStruct((t, c), F32),
        compiler_params=_params("arbitrary"),
        name="gdn_prep",
    )(first, last, gdn_qkv, gdn_qkv, gdn_qkv, w8)


STK = 2 * CHUNK


class _Masks:
    def __init__(self, backward):
        r = lax.broadcasted_iota(jnp.int32, (STK, STK), 0)
        c = lax.broadcasted_iota(jnp.int32, (STK, STK), 1)
        same = (r // CHUNK) == (c // CHUNK)
        if backward:
            self.incl = jnp.logical_and(same, c >= r)
            self.strict = jnp.logical_and(same, c > r)
        else:
            self.incl = jnp.logical_and(same, c <= r)
            self.strict = jnp.logical_and(same, c < r)
        self.same = same
        self.blk16 = (r // 16) == (c // 16)
        self.eye = jnp.where(r == c, 1.0, 0.0).astype(F32)
        self.last_row = 0 if backward else CHUNK - 1


def _cumsum_matrix(backward):
    r = lax.broadcasted_iota(jnp.int32, (CHUNK, CHUNK), 0)
    c = lax.broadcasted_iota(jnp.int32, (CHUNK, CHUNK), 1)
    keep = (c >= r) if backward else (c <= r)
    return jnp.where(keep, 1.0, 0.0).astype(BF16)


def _lane_half():
    return lax.broadcasted_iota(jnp.int32, (CHUNK, PAIR_W), 1) < HEAD_DIM


def _expand(tile, c0, c1, m0):
    return jnp.where(m0, tile[:, c0:c0 + 1], tile[:, c1:c1 + 1])


def _stack_cols(tile, c0, c1):
    top = jnp.broadcast_to(tile[:, c0:c0 + 1], (CHUNK, STK))
    bot = jnp.broadcast_to(tile[:, c1:c1 + 1], (CHUNK, STK))
    return jnp.concatenate([top, bot], axis=0)


def _stack(x, m0):
    return jnp.concatenate([jnp.where(m0, x, 0.0), jnp.where(m0, 0.0, x)], axis=0)


def _unstack(x):
    return x[:CHUNK] + x[CHUNK:]


def _each(fn, *lists):
    return [fn(*args) for args in zip(*lists)]


def _unit_tri_solve(a_list, mk_list, rhs_list):
    eye = mk_list[0].eye
    d = _each(lambda a, mk: jnp.where(mk.blk16, a, 0.0), a_list, mk_list)
    rest = _each(lambda a, x: a - x, a_list, d)
    d2 = _each(lambda x: _mm(x, x), d)
    d4 = _each(lambda x: _mm(x, x), d2)
    d8 = _each(lambda x: _mm(x, x), d4)
    td = _each(lambda x, y: _mm(eye - x, eye + y), d, d2)
    td = _each(lambda x, y: _mm(x, eye + y), td, d4)
    td = _each(lambda x, y: _mm(x, eye + y), td, d8)
    n = _each(_mm, td, rest)
    n2 = _each(lambda x: _mm(x, x), n)
    t = _each(lambda x, y: _mm(eye - x, eye + y), n, n2)
    t = _each(_mm, t, td)
    return _each(_mm, t, rhs_list)


LOCAL_GROUP = 4
DIRS = (0, 1)
CHAINS = [(d, p) for d in DIRS for p in range(N_PAIRS)]


def _pair_lanes(p, group=0, width=GDN_W):
    return slice(group * width + p * PAIR_W, group * width + (p + 1) * PAIR_W)


def _chunk_rows(c):
    return pl.ds(pl.multiple_of(c * CHUNK, CHUNK), CHUNK)


def _gdn_local(mks, m0, x_refs, g_refs, gp_ref, cs, chunk_ids, u_ref, wq_ref, kd_ref, p_ref, egl_ref):
    bias_row = gp_ref[0:1, :]
    a_row = -jnp.exp(gp_ref[1:2, :])
    tiles = {}
    for d in DIRS:
        for j, c in enumerate(chunk_ids):
            gates = g_refs[d][_chunk_rows(c), :]
            log_decay = a_row * _softplus(gates + bias_row)
            tiles[d, j] = (_split_mm_left(cs[d], log_decay, 3), _sigmoid(gates))
    order = [(d, j, p) for j in range(len(chunk_ids)) for d in DIRS for p in range(N_PAIRS)]
    ch = [(d, chunk_ids[j], p) for d, j, p in order]
    mk = [mks[d] for d, j, p in order]
    col = [d * GDN_HEADS + 2 * p for d, j, p in order]
    gc = [tiles[d, j][0] for d, j, p in order]
    beta = [tiles[d, j][1] for d, j, p in order]
    q2 = [x_refs[d][_chunk_rows(c), _pair_lanes(p, 0)] for d, c, p in ch]
    k2 = [x_refs[d][_chunk_rows(c), _pair_lanes(p, 1)] for d, c, p in ch]
    v2 = [x_refs[d][_chunk_rows(c), _pair_lanes(p, 2)] for d, c, p in ch]

    gc2 = _each(lambda t, c0: _expand(t, c0, c0 + 1, m0), gc, col)
    b2 = _each(lambda t, c0: _expand(t, GATE_GROUP + c0, GATE_GROUP + c0 + 1, m0), beta, col)
    gl = _each(lambda g, m: g[m.last_row:m.last_row + 1, :], gc2, mk)
    eg2 = _each(jnp.exp, gc2)
    kb2 = _each(lambda k, b: k * b, k2, b2)
    vb2 = _each(lambda v, b: v * b, v2, b2)
    kbg2 = _each(lambda k, e: k * e, kb2, eg2)
    qg2 = _each(lambda q, e: q * e, q2, eg2)
    kdec2 = _each(lambda k, l, g: k * jnp.exp(l - g), k2, gl, gc2)
    cst = _each(lambda t, c0: _stack_cols(t, c0, c0 + 1), gc, col)
    decay = _each(lambda c_, m: jnp.exp(jnp.where(m.incl, c_ - c_.T, NEG)), cst, mk)
    aq = _each(lambda kb, q, k: _mm_nt(jnp.concatenate([kb, kb, q, q], axis=0), _stack(k, m0)), kb2, q2, k2)
    a = _each(lambda x, dc, m: jnp.where(m.strict, x[:STK] * dc, 0.0), aq, decay, mk)
    pm = _each(lambda x, dc: x[STK:] * dc, aq, decay)
    rhs = _each(lambda vb, kbg: jnp.concatenate([_stack(vb, m0), _stack(kbg, m0)], axis=1), vb2, kbg2)
    uw = _unit_tri_solve(a, mk, rhs)
    for i, (d, c, p) in enumerate(ch):
        lanes = _pair_lanes(p)
        u_ref[d, c, :, lanes] = _unstack(uw[i][:, :PAIR_W])
        wq_ref[d, c, 0:CHUNK, lanes] = _bf(_unstack(uw[i][:, PAIR_W:]))
        wq_ref[d, c, CHUNK:STK, lanes] = _bf(qg2[i])
        kd_ref[d, c, :, lanes] = _bf(kdec2[i])
        p_ref[d, c, p] = _bf(pm[i])
        egl_ref[d, c, :, lanes] = jnp.exp(gl[i])


def _gdn_step(mks, m0, cidx, o_refs, s_ref, u_ref, wq_ref, kd_ref, p_ref, egl_ref):
    s = [s_ref[d, p] for d, p in CHAINS]
    wq = [_mm(wq_ref[d, cidx[d], :, _pair_lanes(p)], s_) for (d, p), s_ in zip(CHAINS, s)]
    vn = [u_ref[d, cidx[d], :, _pair_lanes(p)] - x[:CHUNK] for (d, p), x in zip(CHAINS, wq)]
    pv = [_mm(p_ref[d, cidx[d], p], _stack(v, m0)) for (d, p), v in zip(CHAINS, vn)]
    kv = [_mm_tn(kd_ref[d, cidx[d], :, _pair_lanes(p)], v) for (d, p), v in zip(CHAINS, vn)]
    for i, (d, p) in enumerate(CHAINS):
        o_refs[d][_chunk_rows(cidx[d]), _pair_lanes(p)] = wq[i][CHUNK:] + _unstack(pv[i])
        egl = egl_ref[d, cidx[d], :, _pair_lanes(p)]
        s_ref[d, p] = egl * s[i] + jnp.where(mks[d].same, kv[i], 0.0)


def _gdn_kernel(first_ref, last_ref, xf_ref, xb_ref, gf_ref, gb_ref, gp_ref, of_ref, ob_ref,
                s_ref, u_ref, wq_ref, kd_ref, p_ref, egl_ref):
    i = pl.program_id(0)
    nb = pl.num_programs(0)
    n_chunks = xf_ref.shape[0] // CHUNK

    @pl.when(first_ref[i] == 1)
    def _():
        s_ref[0] = jnp.zeros_like(s_ref[0])

    @pl.when(last_ref[nb - 1 - i] == 1)
    def _():
        s_ref[1] = jnp.zeros_like(s_ref[1])

    m0 = _lane_half()
    mks = [_Masks(backward=False), _Masks(backward=True)]
    cs = [_cumsum_matrix(backward=False), _cumsum_matrix(backward=True)]
    scratch = (u_ref, wq_ref, kd_ref, p_ref, egl_ref)

    def local_body(g, carry):
        chunk_ids = [g * LOCAL_GROUP + j for j in range(LOCAL_GROUP)]
        _gdn_local(mks, m0, (xf_ref, xb_ref), (gf_ref, gb_ref), gp_ref, cs, chunk_ids, *scratch)
        return carry

    lax.fori_loop(0, n_chunks // LOCAL_GROUP, local_body, 0)

    def step_body(c, carry):
        _gdn_step(mks, m0, (c, n_chunks - 1 - c), (of_ref, ob_ref), s_ref, *scratch)
        return carry

    lax.fori_loop(0, n_chunks, step_body, 0)


def _gdn(gqkv, gates, gate_params, tables):
    t = gqkv.shape[0]
    nb = t // MIX_BLOCK
    nc = MIX_BLOCK // CHUNK
    first, last = tables[0], tables[1]
    fwd = lambda w: pl.BlockSpec((MIX_BLOCK, w), lambda i, f, l: (i, 0))
    bwd = lambda w: pl.BlockSpec((MIX_BLOCK, w), lambda i, f, l: (nb - 1 - i, 0))
    grid_spec = pltpu.PrefetchScalarGridSpec(
        num_scalar_prefetch=2,
        grid=(nb,),
        in_specs=[fwd(3 * GDN_W), bwd(3 * GDN_W), fwd(GATE_W), bwd(GATE_W),
                  pl.BlockSpec((8, GATE_W), lambda i, f, l: (0, 0))],
        out_specs=[fwd(GDN_W), bwd(GDN_W)],
        scratch_shapes=[pltpu.VMEM((2, N_PAIRS, STK, STK), F32),
                        pltpu.VMEM((2, nc, CHUNK, GDN_W), F32),
                        pltpu.VMEM((2, nc, STK, GDN_W), BF16),
                        pltpu.VMEM((2, nc, CHUNK, GDN_W), BF16),
                        pltpu.VMEM((2, nc, N_PAIRS, STK, STK), BF16),
                        pltpu.VMEM((2, nc, 1, GDN_W), F32)],
    )
    return pl.pallas_call(
        _gdn_kernel,
        grid_spec=grid_spec,
        out_shape=[jax.ShapeDtypeStruct((t, GDN_W), F32)] * 2,
        compiler_params=_params("arbitrary"),
        name="gdn",
    )(first, last, gqkv, gqkv, gates, gates, gate_params)


def _mlstm_local(mks, m0, bd2, x_refs, g_refs, gp_ref, cs, chunk_ids, upd_ref, intra_ref, mi_ref, fc_ref, fl_ref,
                 ml_ref):
    bias_row = gp_ref[0:1, :]
    tiles = {}
    for d in DIRS:
        for j, c in enumerate(chunk_ids):
            li = g_refs[d][_chunk_rows(c), :] + bias_row
            tiles[d, j] = (_split_mm_left(cs[d], -_softplus(-li), 3), li)
    order = [(d, j, p) for j in range(len(chunk_ids)) for d in DIRS for p in range(N_PAIRS)]
    ch = [(d, chunk_ids[j], p) for d, j, p in order]
    mk = [mks[d] for d, j, p in order]
    col_i = [2 * GATE_GROUP + d * MLSTM_HEADS + 2 * p for d, j, p in order]
    col_f = [3 * GATE_GROUP + d * MLSTM_HEADS + 2 * p for d, j, p in order]
    fc = [tiles[d, j][0] for d, j, p in order]
    li = [tiles[d, j][1] for d, j, p in order]
    q2 = [x_refs[d][_chunk_rows(c), _pair_lanes(p, 0, MLSTM_W)] for d, c, p in ch]
    k2 = [x_refs[d][_chunk_rows(c), _pair_lanes(p, 1, MLSTM_W)] * (HEAD_DIM ** -0.5) for d, c, p in ch]
    v2 = [x_refs[d][_chunk_rows(c), _pair_lanes(p, 2, MLSTM_W)] for d, c, p in ch]
    ones = jnp.ones((CHUNK, PAIR_W), F32)
    ones_st = _stack(ones, m0)

    fc2 = _each(lambda t, c0: _expand(t, c0, c0 + 1, m0), fc, col_f)
    li2 = _each(lambda t, c0: _expand(t, c0, c0 + 1, m0), li, col_i)
    fl = _each(lambda f, m: f[m.last_row:m.last_row + 1, :], fc2, mk)
    g_end = _each(lambda l, f, i_: l - f + i_, fl, fc2, li2)
    ml = _each(lambda g: jnp.max(g, axis=0, keepdims=True), g_end)
    kw2 = _each(lambda k, g, m: k * jnp.exp(g - m), k2, g_end, ml)
    upd = _each(lambda kw, v: jnp.where(bd2, _mm_tn(kw, jnp.concatenate([v, ones], axis=1)), 0.0), kw2, v2)
    cst = _each(lambda t, c0: _stack_cols(t, c0, c0 + 1), fc, col_f)
    est = _each(lambda c_, t, c0: c_ - _stack_cols(t, c0, c0 + 1), cst, li, col_i)
    dmat = _each(lambda c_, e, m: jnp.where(m.incl, c_ - e.T, NEG), cst, est, mk)
    m_intra = _each(lambda x: jnp.max(x, axis=1, keepdims=True), dmat)
    qk = _each(lambda q, k: _mm_nt(jnp.concatenate([q, q], axis=0), _stack(k, m0)), q2, k2)
    s_loc = _each(lambda x, dm, mi: x * jnp.exp(dm - mi), qk, dmat, m_intra)
    intra = _each(lambda s, v: _unstack(_mm(s, jnp.concatenate([_stack(v, m0), ones_st], axis=1))), s_loc, v2)
    mi2 = _each(lambda mi: jnp.where(m0, jnp.broadcast_to(mi[:CHUNK], (CHUNK, PAIR_W)),
                                     jnp.broadcast_to(mi[CHUNK:], (CHUNK, PAIR_W))), m_intra)
    for i, (d, c, p) in enumerate(ch):
        lanes = _pair_lanes(p)
        upd_ref[d, c, p] = upd[i]
        intra_ref[d, c, p] = intra[i]
        mi_ref[d, c, :, lanes] = mi2[i]
        fc_ref[d, c, :, lanes] = fc2[i]
        fl_ref[d, c, :, lanes] = fl[i]
        ml_ref[d, c, :, lanes] = ml[i]


def _mlstm_step(cidx, x_refs, o_refs, c_ref, m_ref, upd_ref, intra_ref, mi_ref, fc_ref, fl_ref, ml_ref):
    for d, p in CHAINS:
        c = cidx[d]
        lanes = _pair_lanes(p)
        m_st = m_ref[d, p]
        cn = c_ref[d, p]
        fl = fl_ref[d, c, :, lanes]
        ml = ml_ref[d, c, :, lanes]
        m_new = jnp.maximum(fl + m_st, ml)
        a_row = jnp.exp(fl + m_st - m_new)
        b_row = jnp.exp(ml - m_new)
        m_inter = fc_ref[d, c, :, lanes] + m_st
        mi2 = mi_ref[d, c, :, lanes]
        m_t = jnp.maximum(m_inter, mi2)
        w_intra = jnp.exp(mi2 - m_t)
        w_inter = jnp.exp(m_inter - m_t)
        inter = _mm(x_refs[d][_chunk_rows(c), _pair_lanes(p, 0, MLSTM_W)], cn)
        intra = intra_ref[d, c, p]
        num = w_intra * intra[:, :PAIR_W] + w_inter * inter[:, :PAIR_W]
        den = w_intra * intra[:, PAIR_W:] + w_inter * inter[:, PAIR_W:]
        o_refs[d][_chunk_rows(c), lanes] = num / jnp.maximum(jnp.abs(den), jnp.exp(-m_t))
        c_ref[d, p] = (jnp.concatenate([a_row, a_row], axis=1) * cn
                       + jnp.concatenate([b_row, b_row], axis=1) * upd_ref[d, c, p])
        m_ref[d, p] = m_new


def _mlstm_kernel(first_ref, last_ref, xf_ref, xb_ref, gf_ref, gb_ref, gp_ref, of_ref, ob_ref,
                  c_ref, m_ref, upd_ref, intra_ref, mi_ref, fc_ref, fl_ref, ml_ref):
    i = pl.program_id(0)
    nb = pl.num_programs(0)
    n_chunks = xf_ref.shape[0] // CHUNK

    @pl.when(first_ref[i] == 1)
    def _():
        c_ref[0] = jnp.zeros_like(c_ref[0])
        m_ref[0] = jnp.full_like(m_ref[0], M_INIT)

    @pl.when(last_ref[nb - 1 - i] == 1)
    def _():
        c_ref[1] = jnp.zeros_like(c_ref[1])
        m_ref[1] = jnp.full_like(m_ref[1], M_INIT)

    m0 = _lane_half()
    r = lax.broadcasted_iota(jnp.int32, (STK, 2 * STK), 0) // HEAD_DIM
    c = (lax.broadcasted_iota(jnp.int32, (STK, 2 * STK), 1) % STK) // HEAD_DIM
    bd2 = r == c
    mks = [_Masks(backward=False), _Masks(backward=True)]
    cs = [_cumsum_matrix(backward=False), _cumsum_matrix(backward=True)]
    scratch = (upd_ref, intra_ref, mi_ref, fc_ref, fl_ref, ml_ref)

    def local_body(g, carry):
        chunk_ids = [g * LOCAL_GROUP + j for j in range(LOCAL_GROUP)]
        _mlstm_local(mks, m0, bd2, (xf_ref, xb_ref), (gf_ref, gb_ref), gp_ref, cs, chunk_ids, *scratch)
        return carry

    lax.fori_loop(0, n_chunks // LOCAL_GROUP, local_body, 0)

    def step_body(cc, carry):
        _mlstm_step((cc, n_chunks - 1 - cc), (xf_ref, xb_ref), (of_ref, ob_ref), c_ref, m_ref, *scratch)
        return carry

    lax.fori_loop(0, n_chunks, step_body, 0)


def _mlstm(ml_qkv, gates, gate_params, tables):
    t = ml_qkv.shape[0]
    nb = t // MIX_BLOCK
    nc = MIX_BLOCK // CHUNK
    first, last = tables[0], tables[1]
    fwd = lambda w: pl.BlockSpec((MIX_BLOCK, w), lambda i, f, l: (i, 0))
    bwd = lambda w: pl.BlockSpec((MIX_BLOCK, w), lambda i, f, l: (nb - 1 - i, 0))
    grid_spec = pltpu.PrefetchScalarGridSpec(
        num_scalar_prefetch=2,
        grid=(nb,),
        in_specs=[fwd(3 * MLSTM_W), bwd(3 * MLSTM_W), fwd(GATE_W), bwd(GATE_W),
                  pl.BlockSpec((8, GATE_W), lambda i, f, l: (0, 0))],
        out_specs=[fwd(MLSTM_W), bwd(MLSTM_W)],
        scratch_shapes=[pltpu.VMEM((2, N_PAIRS, STK, 2 * STK), F32),
                        pltpu.VMEM((2, N_PAIRS, 1, PAIR_W), F32),
                        pltpu.VMEM((2, nc, N_PAIRS, STK, 2 * STK), F32),
                        pltpu.VMEM((2, nc, N_PAIRS, CHUNK, 2 * PAIR_W), F32),
                        pltpu.VMEM((2, nc, CHUNK, MLSTM_W), F32),
                        pltpu.VMEM((2, nc, CHUNK, MLSTM_W), F32),
                        pltpu.VMEM((2, nc, 1, MLSTM_W), F32),
                        pltpu.VMEM((2, nc, 1, MLSTM_W), F32)],
    )
    return pl.pallas_call(
        _mlstm_kernel,
        grid_spec=grid_spec,
        out_shape=[jax.ShapeDtypeStruct((t, MLSTM_W), F32)] * 2,
        compiler_params=_params("arbitrary"),
        name="mlstm",
    )(first, last, ml_qkv, ml_qkv, gates, gates, gate_params)


OUT_TM = 512


def _outproj_kernel(x_ref, na_ref, gf_ref, gb_ref, z_ref, hf_ref, hb_ref, o_ref, gn_ref, mn_ref,
                    w_ref, y_ref):
    ones_bd = _head_sum_matrix(GDN_W)

    def head_rms(v, gain):
        ms = _split_mm(v * v, ones_bd, 2) * (1.0 / HEAD_DIM)
        return v * lax.rsqrt(ms + EPS) * gain

    go = head_rms(gf_ref[...] + gb_ref[...], gn_ref[...])
    z = z_ref[...]
    gdn_out = go * (z * _sigmoid(z))
    mh = head_rms(hf_ref[...] + hb_ref[...], mn_ref[...])
    ml_out = _sigmoid(o_ref[...]) * mh
    y = x_ref[...]
    y = y + jnp.dot(na_ref[...], w_ref[0:NA_W, :], preferred_element_type=F32)
    y = y + jnp.dot(_bf(gdn_out), w_ref[NA_W:NA_W + GDN_W, :], preferred_element_type=F32)
    y = y + jnp.dot(_bf(ml_out), w_ref[NA_W + GDN_W:, :], preferred_element_type=F32)
    y_ref[...] = y


def _outproj(x, na_out, gf, gb, z, hf, hb, o, gdn_norm, mlstm_norm, w_out):
    t = x.shape[0]
    tm = min(OUT_TM, t)
    tok = lambda w: pl.BlockSpec((tm, w), lambda i: (i, 0))
    const = lambda r, w: pl.BlockSpec((r, w), lambda i: (0, 0))
    gn = jnp.tile(gdn_norm.astype(F32), GDN_HEADS).reshape(1, GDN_W)
    mn = mlstm_norm.astype(F32).reshape(1, MLSTM_W)
    return pl.pallas_call(
        _outproj_kernel,
        grid=(t // tm,),
        in_specs=[tok(D_MODEL), tok(NA_W), tok(GDN_W), tok(GDN_W), tok(GDN_W), tok(MLSTM_W), tok(MLSTM_W),
                  tok(MLSTM_W), const(1, GDN_W), const(1, MLSTM_W), const(D_MODEL, D_MODEL)],
        out_specs=tok(D_MODEL),
        out_shape=jax.ShapeDtypeStruct((t, D_MODEL), F32),
        compiler_params=_params("parallel"),
        name="outproj",
    )(x, na_out, gf, gb, z, hf, hb, o, gn, mn, _bf(w_out))


def _gate_params(bias_rows, a_log):
    pad = GATE_W - 4 * GATE_GROUP
    row0 = jnp.concatenate([b.astype(F32).reshape(-1) for b in bias_rows] + [jnp.zeros((pad,), F32)])
    row1 = jnp.concatenate([a_log.astype(F32).reshape(-1), jnp.zeros((GATE_W - GATE_GROUP,), F32)])
    return jnp.concatenate([row0[None], row1[None], jnp.zeros((6, GATE_W), F32)], axis=0)


def _trunk(xs, seq_lens, p):
    depth = p["ffn1_norm"].shape[0]
    tables = tuple(jnp.asarray(a) for a in _block_tables(seq_lens, MIX_BLOCK))
    zeros_g = jnp.zeros((2, GDN_HEADS), F32)
    out_sizes = [x.shape[0] for x in xs]
    x = None
    for i in range(depth):
        x = _ffn(xs if i == 0 else [x], p["ffn1_norm"][i], _bf(p["ffn1_w_in"][i]), _bf(p["ffn1_w_out"][i]),
                 p["final_norm"], False)
        na_qkv, gdn_qkv, gdn_z, ml_qkv, ml_o, gates = _inproj(x, p["mix_norm"][i], _permute_w_in(p["w_in"][i]))
        na_out = _na(na_qkv, _na_bias_table(p["na_rpb"][i]), tables)
        gqkv = _gdn_prep(gdn_qkv, p["gdn_conv"][i], tables)
        gdn_gp = _gate_params([p["gdn_dt_bias"][i], zeros_g, zeros_g, zeros_g], p["gdn_a_log"][i])
        gf, gb = _gdn(gqkv, gates, gdn_gp, tables)
        ml_gp = _gate_params([zeros_g, zeros_g, p["mlstm_i_bias"][i], p["mlstm_f_bias"][i]], zeros_g)
        hf, hb = _mlstm(ml_qkv, gates, ml_gp, tables)
        x = _outproj(x, na_out, gf, gb, gdn_z, hf, hb, ml_o, p["gdn_norm"][i], p["mlstm_norm"][i], p["w_out"][i])
        last = i == depth - 1
        x = _ffn([x], p["ffn2_norm"][i], _bf(p["ffn2_w_in"][i]), _bf(p["ffn2_w_out"][i]), p["final_norm"],
                 last, out_sizes if last else None)
    return x if len(out_sizes) > 1 else [x]


def kernel(x_prompt, x_sample, ffn1_norm, ffn1_w_in, ffn1_w_out, mix_norm, w_in, w_out, na_rpb, gdn_conv,
           gdn_a_log, gdn_dt_bias, gdn_norm, mlstm_i_bias, mlstm_f_bias, mlstm_norm, ffn2_norm, ffn2_w_in,
           ffn2_w_out, final_norm):
    p = dict(ffn1_norm=ffn1_norm, ffn1_w_in=ffn1_w_in, ffn1_w_out=ffn1_w_out, mix_norm=mix_norm, w_in=w_in,
             w_out=w_out, na_rpb=na_rpb, gdn_conv=gdn_conv, gdn_a_log=gdn_a_log, gdn_dt_bias=gdn_dt_bias,
             gdn_norm=gdn_norm, mlstm_i_bias=mlstm_i_bias, mlstm_f_bias=mlstm_f_bias, mlstm_norm=mlstm_norm,
             ffn2_norm=ffn2_norm, ffn2_w_in=ffn2_w_in, ffn2_w_out=ffn2_w_out, final_norm=final_norm)
    bp, lp, _ = x_prompt.shape
    bs, ls, _ = x_sample.shape
    y_prompt, y_sample = _trunk([x_prompt.reshape(bp * lp, D_MODEL), x_sample.reshape(bs * ls, D_MODEL)],
                                (lp,) * bp + (ls,) * bs, p)
    return (y_prompt.reshape(bp, lp, D_MODEL), y_sample.reshape(bs, ls, D_MODEL))
```

```python
import functools

import jax
import jax.numpy as jnp
import numpy as np
from jax import lax
from jax.experimental import pallas as pl
from jax.experimental.pallas import tpu as pltpu

D_MODEL = 1024
D_FF = 2816
HEAD_DIM = 64
NA_HEADS = 4
GDN_HEADS = 6
MLSTM_HEADS = 6
NA_W = NA_HEADS * HEAD_DIM
GDN_W = GDN_HEADS * HEAD_DIM
MLSTM_W = MLSTM_HEADS * HEAD_DIM
GRID_W = 64
NA_WIN_ROWS = 8
NA_WIN_COLS = 16
GDN_CONV = 5
CHUNK = 64
M_INIT = -1e30
EPS = 1e-6
NEG = -1e30

LANES = 128
PAIR_W = 2 * HEAD_DIM
N_PAIRS = GDN_HEADS // 2
GATE_W = LANES
GATE_GROUP = 2 * GDN_HEADS
MIX_BLOCK = 512
VMEM_LIMIT = 56 * 1024 * 1024

F32 = jnp.float32
BF16 = jnp.bfloat16


def _bf(x):
    return x.astype(BF16)


def _mm(a, b):
    return jnp.dot(_bf(a), _bf(b), preferred_element_type=F32)


def _mm_nt(a, b):
    return lax.dot_general(_bf(a), _bf(b), (((1,), (1,)), ((), ())), preferred_element_type=F32)


def _mm_tn(a, b):
    return lax.dot_general(_bf(a), _bf(b), (((0,), (0,)), ((), ())), preferred_element_type=F32)


def _split_mm(a, b_exact, parts):
    acc = None
    rem = a
    for _ in range(parts):
        piece = _bf(rem)
        term = jnp.dot(piece, b_exact, preferred_element_type=F32)
        acc = term if acc is None else acc + term
        rem = rem - piece.astype(F32)
    return acc


def _split_mm_left(a_exact, b, parts):
    acc = None
    rem = b
    for _ in range(parts):
        piece = _bf(rem)
        term = jnp.dot(a_exact, piece, preferred_element_type=F32)
        acc = term if acc is None else acc + term
        rem = rem - piece.astype(F32)
    return acc


def _sigmoid(x):
    return 1.0 / (1.0 + jnp.exp(-x))


def _softplus(x):
    return jnp.maximum(x, 0.0) + jnp.log(1.0 + jnp.exp(-jnp.abs(x)))


def _rms(x, gain):
    return x * lax.rsqrt(jnp.mean(x * x, axis=-1, keepdims=True) + EPS) * gain


def _params(*sem):
    return pltpu.CompilerParams(dimension_semantics=sem, vmem_limit_bytes=VMEM_LIMIT)


FFN_TM = 1024
FFN_SUB = 512
FFN_TF = 256


def _ffn_kernel(*refs, final, sub, in_blocks, out_blocks):
    n_in, n_out = len(in_blocks), len(out_blocks)
    x_refs = refs[:n_in]
    g_ref, wi_ref, wo_ref, fn_ref = refs[n_in:n_in + 4]
    o_refs = refs[n_in + 4:n_in + 4 + n_out]
    act_ref = refs[-1]
    i = pl.program_id(0)
    in_starts = np.cumsum((0,) + tuple(in_blocks))
    out_starts = np.cumsum((0,) + tuple(out_blocks))
    for s in range(x_refs[0].shape[0] // sub):
        rows = slice(s * sub, (s + 1) * sub)
        x = x_refs[-1][rows, :]
        for k in range(n_in - 2, -1, -1):
            x = jnp.where(i < in_starts[k + 1], x_refs[k][rows, :], x)
        xn = _bf(_rms(x, g_ref[...]))
        for j in range(D_FF // FFN_TF):
            gate = jnp.dot(xn, wi_ref[:, j * FFN_TF:(j + 1) * FFN_TF], preferred_element_type=F32)
            up = jnp.dot(xn, wi_ref[:, D_FF + j * FFN_TF:D_FF + (j + 1) * FFN_TF], preferred_element_type=F32)
            act_ref[rows, j * FFN_TF:(j + 1) * FFN_TF] = _bf(gate * _sigmoid(gate) * up)
        y = x + 0.5 * jnp.dot(act_ref[rows, :], wo_ref[...], preferred_element_type=F32)
        if final:
            y = _rms(y, fn_ref[...])
        if n_out == 1:
            o_refs[0][rows, :] = y
        else:
            for k in range(n_out):
                @pl.when(jnp.logical_and(i >= out_starts[k], i < out_starts[k + 1]))
                def _(k=k, y=y):
                    o_refs[k][rows, :] = y


def _ffn(xs, gain, w_in, w_out, final_gain, final, out_sizes=None):
    sizes = [x.shape[0] for x in xs]
    t = sum(sizes)
    out_sizes = [t] if out_sizes is None else list(out_sizes)
    tm = min([FFN_TM] + sizes + out_sizes)
    sub = min(FFN_SUB, tm)
    in_blocks = [s // tm for s in sizes]
    out_blocks = [s // tm for s in out_sizes]

    def part_spec(blocks, k):
        start = int(np.sum(blocks[:k]))
        return pl.BlockSpec((tm, D_MODEL), lambda i: (jnp.clip(i - start, 0, blocks[k] - 1), 0))

    resident = lambda shape: pl.BlockSpec(shape, lambda i: (0, 0), pipeline_mode=pl.Buffered(1))
    outs = pl.pallas_call(
        functools.partial(_ffn_kernel, final=final, sub=sub, in_blocks=tuple(in_blocks),
                          out_blocks=tuple(out_blocks)),
        grid=(t // tm,),
        in_specs=[part_spec(in_blocks, k) for k in range(len(xs))] + [
            resident((1, D_MODEL)),
            resident((D_MODEL, 2 * D_FF)),
            resident((D_FF, D_MODEL)),
            resident((1, D_MODEL)),
        ],
        out_specs=[part_spec(out_blocks, k) for k in range(len(out_sizes))],
        out_shape=[jax.ShapeDtypeStruct((s, D_MODEL), F32) for s in out_sizes],
        scratch_shapes=[pltpu.VMEM((tm, D_FF), BF16)],
        compiler_params=_params("arbitrary"),
        name="ffn",
    )(*xs, gain.reshape(1, D_MODEL), w_in, w_out, final_gain.reshape(1, D_MODEL))
    return outs[0] if len(outs) == 1 else outs


PROJ_TM = 512
PROJ_WIDTHS = (3 * NA_W, 3 * GDN_W, GDN_W, 3 * MLSTM_W, MLSTM_W, GATE_W)
PROJ_DTYPES = (BF16, F32, F32, F32, F32, F32)
P_PERM = sum(PROJ_WIDTHS)


def _permute_w_in(w_in):
    o = np.cumsum([0, 3 * NA_W, 3 * GDN_W, GDN_W, 2 * GDN_HEADS, 2 * GDN_HEADS,
                   3 * MLSTM_W, MLSTM_W, 2 * MLSTM_HEADS, 2 * MLSTM_HEADS])
    n_gates = 4 * 2 * GDN_HEADS
    cols = [w_in[:, o[0]:o[3]], w_in[:, o[5]:o[7]], w_in[:, o[3]:o[5]], w_in[:, o[7]:o[9]],
            jnp.zeros((D_MODEL, GATE_W - n_gates), w_in.dtype)]
    return _bf(jnp.concatenate(cols, axis=1))


PROJ_DOT_GROUPS = ((1, 2), (0,), (3, 4), (5,))
PROJ_OFFSETS = tuple(int(v) for v in np.cumsum((0,) + PROJ_WIDTHS[:-1]))
PROJ_HALO = 16


def _head_sumsq(x):
    r = lax.broadcasted_iota(jnp.int32, (PAIR_W, PAIR_W), 0) // HEAD_DIM
    c = lax.broadcasted_iota(jnp.int32, (PAIR_W, PAIR_W), 1) // HEAD_DIM
    ones_bd = jnp.where(r == c, 1.0, 0.0).astype(BF16)
    sq = x * x
    parts = [_split_mm(sq[:, g:g + PAIR_W], ones_bd, 2) for g in range(0, x.shape[1], PAIR_W)]
    return jnp.concatenate(parts, axis=1)


def _gdn_preprocess(y, w_ref, tb):
    n = y.shape[0]
    acc = None
    for j in range(GDN_CONV):
        shift = (GDN_CONV // 2 - j) % n
        tap = y if shift == 0 else pltpu.roll(y, shift, 0)
        term = tap[PROJ_HALO:PROJ_HALO + tb] * w_ref[j:j + 1, :]
        acc = term if acc is None else acc + term
    act = acc * _sigmoid(acc)
    q = act[:, 0:GDN_W]
    k = act[:, GDN_W:2 * GDN_W]
    qn = q * lax.rsqrt(_head_sumsq(q) + EPS) * (HEAD_DIM ** -0.5)
    kn = k * lax.rsqrt(_head_sumsq(k) + EPS)
    return qn, kn, act[:, 2 * GDN_W:3 * GDN_W]


def _inproj_kernel(first_ref, last_ref, x_ref, xp_ref, xn_ref, g_ref, w_ref, cw_ref, *o_refs):
    i = pl.program_id(0)
    tb = x_ref.shape[0]
    prev = jnp.where(first_ref[i] == 1, 0.0, xp_ref[...])
    nxt = jnp.where(last_ref[i] == 1, 0.0, xn_ref[...])
    xe = jnp.concatenate([prev, x_ref[...], nxt], axis=0)
    xn = _bf(_rms(xe, g_ref[...]))
    centre = slice(PROJ_HALO, PROJ_HALO + tb)
    for group in PROJ_DOT_GROUPS:
        off = PROJ_OFFSETS[group[0]]
        width = sum(PROJ_WIDTHS[k] for k in group)
        res = jnp.dot(xn, w_ref[:, off:off + width], preferred_element_type=F32)
        col = 0
        for k in group:
            part = res[:, col:col + PROJ_WIDTHS[k]]
            col += PROJ_WIDTHS[k]
            if k == 1:
                qn, kn, v = _gdn_preprocess(part, cw_ref, tb)
                o_refs[k][:, 0:GDN_W] = qn
                o_refs[k][:, GDN_W:2 * GDN_W] = kn
                o_refs[k][:, 2 * GDN_W:3 * GDN_W] = v
            else:
                o_refs[k][...] = part[centre].astype(o_refs[k].dtype)


def _inproj(x, gain, w_perm, conv_w, tables):
    t = x.shape[0]
    tm = MIX_BLOCK
    per = tm // PROJ_HALO
    n_halo = t // PROJ_HALO
    first, last = tables[0], tables[1]
    const = lambda shape: pl.BlockSpec(shape, lambda i, f, l: (0, 0))
    grid_spec = pltpu.PrefetchScalarGridSpec(
        num_scalar_prefetch=2,
        grid=(t // tm,),
        in_specs=[
            pl.BlockSpec((tm, D_MODEL), lambda i, f, l: (i, 0)),
            pl.BlockSpec((PROJ_HALO, D_MODEL), lambda i, f, l: (jnp.maximum(i * per - 1, 0), 0)),
            pl.BlockSpec((PROJ_HALO, D_MODEL), lambda i, f, l: (jnp.minimum((i + 1) * per, n_halo - 1), 0)),
            const((1, D_MODEL)),
            const((D_MODEL, P_PERM)),
            const((8, 3 * GDN_W)),
        ],
        out_specs=[pl.BlockSpec((tm, w), lambda i, f, l: (i, 0)) for w in PROJ_WIDTHS],
    )
    w8 = jnp.concatenate([conv_w.astype(F32), jnp.zeros((8 - GDN_CONV, 3 * GDN_W), F32)], axis=0)
    return pl.pallas_call(
        _inproj_kernel,
        grid_spec=grid_spec,
        out_shape=[jax.ShapeDtypeStruct((t, w), d) for w, d in zip(PROJ_WIDTHS, PROJ_DTYPES)],
        compiler_params=_params("arbitrary"),
        name="inproj",
    )(first, last, x, x, x, gain.reshape(1, D_MODEL), w_perm, w8)


def _block_tables(seq_lens, block):
    first, last = [], []
    for n in seq_lens:
        nb = n // block
        first += [1] + [0] * (nb - 1)
        last += [0] * (nb - 1) + [1]
    first = np.asarray(first, np.int32)
    last = np.asarray(last, np.int32)
    idx = np.arange(len(first), dtype=np.int32)
    prev = np.where(first == 1, idx, idx - 1).astype(np.int32)
    nxt = np.where(last == 1, idx, idx + 1).astype(np.int32)
    return first, last, prev, nxt


NA_ROWS = MIX_BLOCK // GRID_W
NA_KEYS = NA_WIN_ROWS * GRID_W
NA_ROW_GROUP = 2


def _na_bias_table(rpb):
    c = np.arange(GRID_W)
    c_start = np.clip(c - NA_WIN_COLS // 2, 0, GRID_W - NA_WIN_COLS)
    in_win = (c[None, :] >= c_start[:, None]) & (c[None, :] < c_start[:, None] + NA_WIN_COLS)
    dc = np.clip(c[None, :] - c[:, None], 1 - NA_WIN_COLS, NA_WIN_COLS - 1) + NA_WIN_COLS - 1
    one_hot = jnp.asarray(dc[None] == np.arange(2 * NA_WIN_COLS - 1)[:, None, None], F32)
    by_col = jnp.einsum("hrc,cqk->hrqk", rpb.astype(F32), one_hot, precision=lax.Precision.HIGHEST)
    tab = jnp.stack([by_col[:, NA_WIN_ROWS - 1 - v:2 * NA_WIN_ROWS - 1 - v] for v in range(NA_WIN_ROWS)],
                    axis=1)
    tab = jnp.where(in_win[None, None, None], tab, NEG).transpose(0, 1, 3, 2, 4)
    return tab.reshape(NA_HEADS, NA_WIN_ROWS, GRID_W, NA_KEYS)


def _na_kernel(first_ref, last_ref, prev_ref, next_ref,
               q_ref, kp_ref, kc_ref, kn_ref, vp_ref, vc_ref, vn_ref, bias_ref, o_ref, kbuf, vbuf):
    del prev_ref, next_ref
    g = pl.program_id(0)
    kbuf[0:MIX_BLOCK] = kp_ref[...]
    kbuf[MIX_BLOCK:2 * MIX_BLOCK] = kc_ref[...]
    kbuf[2 * MIX_BLOCK:3 * MIX_BLOCK] = kn_ref[...]
    vbuf[0:MIX_BLOCK] = vp_ref[...]
    vbuf[MIX_BLOCK:2 * MIX_BLOCK] = vc_ref[...]
    vbuf[2 * MIX_BLOCK:3 * MIX_BLOCK] = vn_ref[...]
    scale = HEAD_DIM ** -0.5
    half = NA_WIN_ROWS // 2
    m0 = lax.broadcasted_iota(jnp.int32, (GRID_W, PAIR_W), 1) < HEAD_DIM
    for a0 in range(0, NA_ROWS, NA_ROW_GROUP):
        q, k, v, bias = [], [], [], []
        for a in range(a0, a0 + NA_ROW_GROUP):
            edge = (first_ref[g] if a < half else last_ref[g]) == 1
            start = pl.multiple_of(jnp.where(edge, NA_ROWS, a + half) * GRID_W, GRID_W)
            variant = jnp.where(edge, a, half)
            for h in range(NA_HEADS):
                lanes = slice((h // 2) * PAIR_W, (h // 2 + 1) * PAIR_W)
                q2 = q_ref[a * GRID_W:(a + 1) * GRID_W, lanes]
                q.append(jnp.where(m0 if h % 2 == 0 else jnp.logical_not(m0), q2, jnp.zeros_like(q2)))
                k.append(kbuf[pl.ds(start, NA_KEYS), lanes])
                v.append(vbuf[pl.ds(start, NA_KEYS), lanes])
                bias.append(bias_ref[h, variant])
        s = _each(lambda q_, k_, b_: _mm_nt(q_, k_) * scale + b_, q, k, bias)
        p = _each(lambda s_: jnp.exp(s_ - jnp.max(s_, axis=-1, keepdims=True)), s)
        l = _each(lambda p_: jnp.sum(p_, axis=-1, keepdims=True), p)
        o = _each(lambda p_, v_, l_: _mm(p_, v_) / l_, p, v, l)
        for j in range(0, len(o), 2):
            a = a0 + j // NA_HEADS
            lanes = slice(((j % NA_HEADS) // 2) * PAIR_W, ((j % NA_HEADS) // 2 + 1) * PAIR_W)
            o_ref[a * GRID_W:(a + 1) * GRID_W, lanes] = jnp.where(m0, o[j], o[j + 1]).astype(o_ref.dtype)


def _na(na_qkv, bias, tables):
    t = na_qkv.shape[0]
    nb = t // MIX_BLOCK
    blk = (MIX_BLOCK, NA_W)
    cur = lambda col: pl.BlockSpec(blk, lambda g, f, l, p, n: (g, col))
    prv = lambda col: pl.BlockSpec(blk, lambda g, f, l, p, n: (p[g], col))
    nxt = lambda col: pl.BlockSpec(blk, lambda g, f, l, p, n: (n[g], col))
    grid_spec = pltpu.PrefetchScalarGridSpec(
        num_scalar_prefetch=4,
        grid=(nb,),
        in_specs=[cur(0), prv(1), cur(1), nxt(1), prv(2), cur(2), nxt(2),
                  pl.BlockSpec((NA_HEADS, NA_WIN_ROWS, GRID_W, NA_KEYS), lambda g, f, l, p, n: (0, 0, 0, 0))],
        out_specs=pl.BlockSpec(blk, lambda g, f, l, p, n: (g, 0)),
        scratch_shapes=[pltpu.VMEM((3 * MIX_BLOCK, NA_W), BF16), pltpu.VMEM((3 * MIX_BLOCK, NA_W), BF16)],
    )
    return pl.pallas_call(
        _na_kernel,
        grid_spec=grid_spec,
        out_shape=jax.ShapeDtypeStruct((t, NA_W), BF16),
        compiler_params=_params("arbitrary"),
        name="na",
    )(*tables, na_qkv, na_qkv, na_qkv, na_qkv, na_qkv, na_qkv, na_qkv, bias)


STK = 2 * CHUNK


class _Masks:
    def __init__(self, backward):
        r = lax.broadcasted_iota(jnp.int32, (STK, STK), 0)
        c = lax.broadcasted_iota(jnp.int32, (STK, STK), 1)
        same = (r // CHUNK) == (c // CHUNK)
        if backward:
            self.incl = jnp.logical_and(same, c >= r)
            self.strict = jnp.logical_and(same, c > r)
        else:
            self.incl = jnp.logical_and(same, c <= r)
            self.strict = jnp.logical_and(same, c < r)
        self.same = same
        self.blk16 = (r // 16) == (c // 16)
        self.eye = jnp.where(r == c, 1.0, 0.0).astype(F32)
        self.last_row = 0 if backward else CHUNK - 1


def _cumsum_matrix(backward):
    r = lax.broadcasted_iota(jnp.int32, (CHUNK, CHUNK), 0)
    c = lax.broadcasted_iota(jnp.int32, (CHUNK, CHUNK), 1)
    keep = (c >= r) if backward else (c <= r)
    return jnp.where(keep, 1.0, 0.0).astype(BF16)


def _lane_half():
    return lax.broadcasted_iota(jnp.int32, (CHUNK, PAIR_W), 1) < HEAD_DIM


def _expand(tile, c0, c1, m0):
    return jnp.where(m0, tile[:, c0:c0 + 1], tile[:, c1:c1 + 1])


def _stack_cols(tile, c0, c1):
    top = jnp.broadcast_to(tile[:, c0:c0 + 1], (CHUNK, STK))
    bot = jnp.broadcast_to(tile[:, c1:c1 + 1], (CHUNK, STK))
    return jnp.concatenate([top, bot], axis=0)


def _stack(x, m0):
    return jnp.concatenate([jnp.where(m0, x, 0.0), jnp.where(m0, 0.0, x)], axis=0)


def _unstack(x):
    return x[:CHUNK] + x[CHUNK:]


def _each(fn, *lists):
    return [fn(*args) for args in zip(*lists)]


def _no_tick():
    pass


def _unit_tri_solve(a_list, mk_list, rhs_list, tick=_no_tick):
    eye = mk_list[0].eye
    d = _each(lambda a, mk: jnp.where(mk.blk16, a, 0.0), a_list, mk_list)
    rest = _each(lambda a, x: a - x, a_list, d)
    d2 = _each(lambda x: _mm(x, x), d)
    tick()
    d4 = _each(lambda x: _mm(x, x), d2)
    td = _each(lambda x, y: _mm(eye - x, eye + y), d, d2)
    tick()
    d8 = _each(lambda x: _mm(x, x), d4)
    td = _each(lambda x, y: _mm(x, eye + y), td, d4)
    tick()
    td = _each(lambda x, y: _mm(x, eye + y), td, d8)
    tick()
    n = _each(_mm, td, rest)
    tick()
    n2 = _each(lambda x: _mm(x, x), n)
    tick()
    t = _each(lambda x, y: _mm(eye - x, eye + y), n, n2)
    tick()
    t = _each(_mm, t, td)
    tick()
    return _each(_mm, t, rhs_list)


DIRS = (0, 1)
CHAINS = [(d, p) for d in DIRS for p in range(N_PAIRS)]


def _pair_lanes(p, group=0, width=GDN_W):
    return slice(group * width + p * PAIR_W, group * width + (p + 1) * PAIR_W)


def _chunk_rows(c):
    return slice(c * CHUNK, (c + 1) * CHUNK)


def _interleaved(local_fn, seq_gen):
    if seq_gen is None:
        local_fn(_no_tick)
        return
    local_fn(lambda: next(seq_gen, None))
    for _ in seq_gen:
        pass


def _half_block_chunks(n_chunks, half):
    m = n_chunks // 2
    fwd = list(range(half * m, (half + 1) * m))
    bwd = list(range(n_chunks - 1 - half * m, n_chunks - 1 - (half + 1) * m, -1))
    return (fwd, bwd)


def _gdn_local(mks, m0, x_refs, g_refs, gp_ref, cs, chunk_ids, u_ref, wq_ref, kd_ref, p_ref, egl_ref,
               tick=_no_tick):
    bias_row = gp_ref[0:1, :]
    a_row = -jnp.exp(gp_ref[1:2, :])
    tiles = {}
    for d in DIRS:
        for j, c in enumerate(chunk_ids[d]):
            gates = g_refs[d][_chunk_rows(c), :]
            log_decay = a_row * _softplus(gates + bias_row)
            tiles[d, j] = (_split_mm_left(cs[d], log_decay, 3), _sigmoid(gates))
    tick()
    order = [(d, j, p) for j in range(len(chunk_ids[0])) for d in DIRS for p in range(N_PAIRS)]
    ch = [(d, chunk_ids[d][j], p) for d, j, p in order]
    mk = [mks[d] for d, j, p in order]
    col = [d * GDN_HEADS + 2 * p for d, j, p in order]
    gc = [tiles[d, j][0] for d, j, p in order]
    beta = [tiles[d, j][1] for d, j, p in order]
    q2 = [x_refs[d][_chunk_rows(c), _pair_lanes(p, 0)] for d, c, p in ch]
    k2 = [x_refs[d][_chunk_rows(c), _pair_lanes(p, 1)] for d, c, p in ch]
    v2 = [x_refs[d][_chunk_rows(c), _pair_lanes(p, 2)] for d, c, p in ch]

    gc2 = _each(lambda t, c0: _expand(t, c0, c0 + 1, m0), gc, col)
    b2 = _each(lambda t, c0: _expand(t, GATE_GROUP + c0, GATE_GROUP + c0 + 1, m0), beta, col)
    gl = _each(lambda g, m: g[m.last_row:m.last_row + 1, :], gc2, mk)
    eg2 = _each(jnp.exp, gc2)
    kb2 = _each(lambda k, b: k * b, k2, b2)
    vb2 = _each(lambda v, b: v * b, v2, b2)
    kbg2 = _each(lambda k, e: k * e, kb2, eg2)
    qg2 = _each(lambda q, e: q * e, q2, eg2)
    kdec2 = _each(lambda k, l, g: k * jnp.exp(l - g), k2, gl, gc2)
    cst = _each(lambda t, c0: _stack_cols(t, c0, c0 + 1), gc, col)
    decay = _each(lambda c_, m: jnp.exp(jnp.where(m.incl, c_ - c_.T, NEG)), cst, mk)
    aq = _each(lambda kb, q, k: _mm_nt(jnp.concatenate([kb, kb, q, q], axis=0), _stack(k, m0)), kb2, q2, k2)
    tick()
    a = _each(lambda x, dc, m: jnp.where(m.strict, x[:STK] * dc, 0.0), aq, decay, mk)
    pm = _each(lambda x, dc: x[STK:] * dc, aq, decay)
    rhs = _each(lambda vb, kbg: jnp.concatenate([_stack(vb, m0), _stack(kbg, m0)], axis=1), vb2, kbg2)
    uw = _unit_tri_solve(a, mk, rhs, tick)
    tick()
    for i, (d, c, p) in enumerate(ch):
        lanes = _pair_lanes(p)
        u_ref[d, c, :, lanes] = _unstack(uw[i][:, :PAIR_W])
        wq_ref[d, c, 0:CHUNK, lanes] = _bf(_unstack(uw[i][:, PAIR_W:]))
        wq_ref[d, c, CHUNK:STK, lanes] = _bf(qg2[i])
        kd_ref[d, c, :, lanes] = _bf(kdec2[i])
        p_ref[d, c, p] = _bf(pm[i])
        egl_ref[d, c, :, lanes] = jnp.exp(gl[i])


def _gdn_seq(mks, m0, chunk_ids, o_refs, s_ref, u_ref, wq_ref, kd_ref, p_ref, egl_ref):
    for j in range(len(chunk_ids[0])):
        cidx = [chunk_ids[d][j] for d in DIRS]
        s = [s_ref[d, p] for d, p in CHAINS]
        wq = [_mm(wq_ref[d, cidx[d], :, _pair_lanes(p)], s_) for (d, p), s_ in zip(CHAINS, s)]
        yield
        vn = [u_ref[d, cidx[d], :, _pair_lanes(p)] - x[:CHUNK] for (d, p), x in zip(CHAINS, wq)]
        pv = [_mm(p_ref[d, cidx[d], p], _stack(v, m0)) for (d, p), v in zip(CHAINS, vn)]
        kv = [_mm_tn(kd_ref[d, cidx[d], :, _pair_lanes(p)], v) for (d, p), v in zip(CHAINS, vn)]
        yield
        for i, (d, p) in enumerate(CHAINS):
            o_refs[d][_chunk_rows(cidx[d]), _pair_lanes(p)] = wq[i][CHUNK:] + _unstack(pv[i])
            egl = egl_ref[d, cidx[d], :, _pair_lanes(p)]
            s_ref[d, p] = egl * s[i] + jnp.where(mks[d].same, kv[i], 0.0)


def _gdn_kernel(first_ref, last_ref, xf_ref, xb_ref, gf_ref, gb_ref, gp_ref, of_ref, ob_ref,
                s_ref, u_ref, wq_ref, kd_ref, p_ref, egl_ref):
    i = pl.program_id(0)
    nb = pl.num_programs(0)
    n_chunks = xf_ref.shape[0] // CHUNK

    @pl.when(first_ref[i] == 1)
    def _():
        s_ref[0] = jnp.zeros_like(s_ref[0])

    @pl.when(last_ref[nb - 1 - i] == 1)
    def _():
        s_ref[1] = jnp.zeros_like(s_ref[1])

    m0 = _lane_half()
    mks = [_Masks(backward=False), _Masks(backward=True)]
    cs = [_cumsum_matrix(backward=False), _cumsum_matrix(backward=True)]
    scratch = (u_ref, wq_ref, kd_ref, p_ref, egl_ref)
    halves = [_half_block_chunks(n_chunks, h) for h in range(2)]

    def local(h):
        return lambda tick: _gdn_local(mks, m0, (xf_ref, xb_ref), (gf_ref, gb_ref), gp_ref, cs, halves[h],
                                       *scratch, tick=tick)

    def seq(h):
        return _gdn_seq(mks, m0, halves[h], (of_ref, ob_ref), s_ref, *scratch)

    _interleaved(local(0), None)
    _interleaved(local(1), seq(0))
    for _ in seq(1):
        pass


def _gdn(gqkv, gates, gate_params, tables):
    t = gqkv.shape[0]
    nb = t // MIX_BLOCK
    nc = MIX_BLOCK // CHUNK
    first, last = tables[0], tables[1]
    fwd = lambda w: pl.BlockSpec((MIX_BLOCK, w), lambda i, f, l: (i, 0))
    bwd = lambda w: pl.BlockSpec((MIX_BLOCK, w), lambda i, f, l: (nb - 1 - i, 0))
    grid_spec = pltpu.PrefetchScalarGridSpec(
        num_scalar_prefetch=2,
        grid=(nb,),
        in_specs=[fwd(3 * GDN_W), bwd(3 * GDN_W), fwd(GATE_W), bwd(GATE_W),
                  pl.BlockSpec((8, GATE_W), lambda i, f, l: (0, 0))],
        out_specs=[fwd(GDN_W), bwd(GDN_W)],
        scratch_shapes=[pltpu.VMEM((2, N_PAIRS, STK, STK), F32),
                        pltpu.VMEM((2, nc, CHUNK, GDN_W), F32),
                        pltpu.VMEM((2, nc, STK, GDN_W), BF16),
                        pltpu.VMEM((2, nc, CHUNK, GDN_W), BF16),
                        pltpu.VMEM((2, nc, N_PAIRS, STK, STK), BF16),
                        pltpu.VMEM((2, nc, 1, GDN_W), F32)],
    )
    return pl.pallas_call(
        _gdn_kernel,
        grid_spec=grid_spec,
        out_shape=[jax.ShapeDtypeStruct((t, GDN_W), F32)] * 2,
        compiler_params=_params("arbitrary"),
        name="gdn",
    )(first, last, gqkv, gqkv, gates, gates, gate_params)


def _mlstm_local(mks, m0, bd2, x_refs, g_refs, gp_ref, cs, chunk_ids, upd_ref, intra_ref, mi_ref, fc_ref, fl_ref,
                 ml_ref, tick=_no_tick):
    bias_row = gp_ref[0:1, :]
    tiles = {}
    for d in DIRS:
        for j, c in enumerate(chunk_ids[d]):
            li = g_refs[d][_chunk_rows(c), :] + bias_row
            tiles[d, j] = (_split_mm_left(cs[d], -_softplus(-li), 3), li)
    tick()
    order = [(d, j, p) for j in range(len(chunk_ids[0])) for d in DIRS for p in range(N_PAIRS)]
    ch = [(d, chunk_ids[d][j], p) for d, j, p in order]
    mk = [mks[d] for d, j, p in order]
    col_i = [2 * GATE_GROUP + d * MLSTM_HEADS + 2 * p for d, j, p in order]
    col_f = [3 * GATE_GROUP + d * MLSTM_HEADS + 2 * p for d, j, p in order]
    fc = [tiles[d, j][0] for d, j, p in order]
    li = [tiles[d, j][1] for d, j, p in order]
    q2 = [x_refs[d][_chunk_rows(c), _pair_lanes(p, 0, MLSTM_W)] for d, c, p in ch]
    k2 = [x_refs[d][_chunk_rows(c), _pair_lanes(p, 1, MLSTM_W)] * (HEAD_DIM ** -0.5) for d, c, p in ch]
    v2 = [x_refs[d][_chunk_rows(c), _pair_lanes(p, 2, MLSTM_W)] for d, c, p in ch]
    ones = jnp.ones((CHUNK, PAIR_W), F32)
    ones_st = _stack(ones, m0)

    fc2 = _each(lambda t, c0: _expand(t, c0, c0 + 1, m0), fc, col_f)
    li2 = _each(lambda t, c0: _expand(t, c0, c0 + 1, m0), li, col_i)
    fl = _each(lambda f, m: f[m.last_row:m.last_row + 1, :], fc2, mk)
    g_end = _each(lambda l, f, i_: l - f + i_, fl, fc2, li2)
    ml = _each(lambda g: jnp.max(g, axis=0, keepdims=True), g_end)
    kw2 = _each(lambda k, g, m: k * jnp.exp(g - m), k2, g_end, ml)
    upd = _each(lambda kw, v: jnp.where(bd2, _mm_tn(kw, jnp.concatenate([v, ones], axis=1)), 0.0), kw2, v2)
    tick()
    cst = _each(lambda t, c0: _stack_cols(t, c0, c0 + 1), fc, col_f)
    est = _each(lambda c_, t, c0: c_ - _stack_cols(t, c0, c0 + 1), cst, li, col_i)
    dmat = _each(lambda c_, e, m: jnp.where(m.incl, c_ - e.T, NEG), cst, est, mk)
    m_intra = _each(lambda x: jnp.max(x, axis=1, keepdims=True), dmat)
    tick()
    qk = _each(lambda q, k: _mm_nt(jnp.concatenate([q, q], axis=0), _stack(k, m0)), q2, k2)
    tick()
    s_loc = _each(lambda x, dm, mi: x * jnp.exp(dm - mi), qk, dmat, m_intra)
    tick()
    intra = _each(lambda s, v: _unstack(_mm(s, jnp.concatenate([_stack(v, m0), ones_st], axis=1))), s_loc, v2)
    tick()
    mi2 = _each(lambda mi: jnp.where(m0, jnp.broadcast_to(mi[:CHUNK], (CHUNK, PAIR_W)),
                                     jnp.broadcast_to(mi[CHUNK:], (CHUNK, PAIR_W))), m_intra)
    for i, (d, c, p) in enumerate(ch):
        lanes = _pair_lanes(p)
        upd_ref[d, c, p] = upd[i]
        intra_ref[d, c, p] = intra[i]
        mi_ref[d, c, :, lanes] = mi2[i]
        fc_ref[d, c, :, lanes] = fc2[i]
        fl_ref[d, c, :, lanes] = fl[i]
        ml_ref[d, c, :, lanes] = ml[i]
        if i % N_PAIRS == N_PAIRS - 1:
            tick()


def _mlstm_seq(chunk_ids, x_refs, o_refs, c_ref, m_ref, upd_ref, intra_ref, mi_ref, fc_ref, fl_ref, ml_ref):
    for j, d, p in [(j, d, p) for j in range(len(chunk_ids[0])) for d, p in CHAINS]:
        c = chunk_ids[d][j]
        lanes = _pair_lanes(p)
        m_st = m_ref[d, p]
        cn = c_ref[d, p]
        fl = fl_ref[d, c, :, lanes]
        ml = ml_ref[d, c, :, lanes]
        m_new = jnp.maximum(fl + m_st, ml)
        a_row = jnp.exp(fl + m_st - m_new)
        b_row = jnp.exp(ml - m_new)
        m_inter = fc_ref[d, c, :, lanes] + m_st
        mi2 = mi_ref[d, c, :, lanes]
        m_t = jnp.maximum(m_inter, mi2)
        w_intra = jnp.exp(mi2 - m_t)
        w_inter = jnp.exp(m_inter - m_t)
        inter = _mm(x_refs[d][_chunk_rows(c), _pair_lanes(p, 0, MLSTM_W)], cn)
        intra = intra_ref[d, c, p]
        num = w_intra * intra[:, :PAIR_W] + w_inter * inter[:, :PAIR_W]
        den = w_intra * intra[:, PAIR_W:] + w_inter * inter[:, PAIR_W:]
        o_refs[d][_chunk_rows(c), lanes] = num / jnp.maximum(jnp.abs(den), jnp.exp(-m_t))
        c_ref[d, p] = (jnp.concatenate([a_row, a_row], axis=1) * cn
                       + jnp.concatenate([b_row, b_row], axis=1) * upd_ref[d, c, p])
        m_ref[d, p] = m_new
        if p == N_PAIRS - 1:
            yield


def _mlstm_kernel(first_ref, last_ref, xf_ref, xb_ref, gf_ref, gb_ref, gp_ref, of_ref, ob_ref,
                  c_ref, m_ref, upd_ref, intra_ref, mi_ref, fc_ref, fl_ref, ml_ref):
    i = pl.program_id(0)
    nb = pl.num_programs(0)
    n_chunks = xf_ref.shape[0] // CHUNK

    @pl.when(first_ref[i] == 1)
    def _():
        c_ref[0] = jnp.zeros_like(c_ref[0])
        m_ref[0] = jnp.full_like(m_ref[0], M_INIT)

    @pl.when(last_ref[nb - 1 - i] == 1)
    def _():
        c_ref[1] = jnp.zeros_like(c_ref[1])
        m_ref[1] = jnp.full_like(m_ref[1], M_INIT)

    m0 = _lane_half()
    r = lax.broadcasted_iota(jnp.int32, (STK, 2 * STK), 0) // HEAD_DIM
    c = (lax.broadcasted_iota(jnp.int32, (STK, 2 * STK), 1) % STK) // HEAD_DIM
    bd2 = r == c
    mks = [_Masks(backward=False), _Masks(backward=True)]
    cs = [_cumsum_matrix(backward=False), _cumsum_matrix(backward=True)]
    scratch = (upd_ref, intra_ref, mi_ref, fc_ref, fl_ref, ml_ref)

    halves = [_half_block_chunks(n_chunks, h) for h in range(2)]

    def local(h):
        return lambda tick: _mlstm_local(mks, m0, bd2, (xf_ref, xb_ref), (gf_ref, gb_ref), gp_ref, cs, halves[h],
                                         *scratch, tick=tick)

    def seq(h):
        return _mlstm_seq(halves[h], (xf_ref, xb_ref), (of_ref, ob_ref), c_ref, m_ref, *scratch)

    _interleaved(local(0), None)
    _interleaved(local(1), seq(0))
    for _ in seq(1):
        pass


def _mlstm(ml_qkv, gates, gate_params, tables):
    t = ml_qkv.shape[0]
    nb = t // MIX_BLOCK
    nc = MIX_BLOCK // CHUNK
    first, last = tables[0], tables[1]
    fwd = lambda w: pl.BlockSpec((MIX_BLOCK, w), lambda i, f, l: (i, 0))
    bwd = lambda w: pl.BlockSpec((MIX_BLOCK, w), lambda i, f, l: (nb - 1 - i, 0))
    grid_spec = pltpu.PrefetchScalarGridSpec(
        num_scalar_prefetch=2,
        grid=(nb,),
        in_specs=[fwd(3 * MLSTM_W), bwd(3 * MLSTM_W), fwd(GATE_W), bwd(GATE_W),
                  pl.BlockSpec((8, GATE_W), lambda i, f, l: (0, 0))],
        out_specs=[fwd(MLSTM_W), bwd(MLSTM_W)],
        scratch_shapes=[pltpu.VMEM((2, N_PAIRS, STK, 2 * STK), F32),
                        pltpu.VMEM((2, N_PAIRS, 1, PAIR_W), F32),
                        pltpu.VMEM((2, nc, N_PAIRS, STK, 2 * STK), F32),
                        pltpu.VMEM((2, nc, N_PAIRS, CHUNK, 2 * PAIR_W), F32),
                        pltpu.VMEM((2, nc, CHUNK, MLSTM_W), F32),
                        pltpu.VMEM((2, nc, CHUNK, MLSTM_W), F32),
                        pltpu.VMEM((2, nc, 1, MLSTM_W), F32),
                        pltpu.VMEM((2, nc, 1, MLSTM_W), F32)],
    )
    return pl.pallas_call(
        _mlstm_kernel,
        grid_spec=grid_spec,
        out_shape=[jax.ShapeDtypeStruct((t, MLSTM_W), F32)] * 2,
        compiler_params=_params("arbitrary"),
        name="mlstm",
    )(first, last, ml_qkv, ml_qkv, gates, gates, gate_params)


OUT_TM = 512


def _outproj_kernel(x_ref, na_ref, gf_ref, gb_ref, z_ref, hf_ref, hb_ref, o_ref, gn_ref, mn_ref,
                    w_ref, y_ref):
    def head_rms(v, gain):
        ms = _head_sumsq(v) * (1.0 / HEAD_DIM)
        return v * lax.rsqrt(ms + EPS) * gain

    go = head_rms(gf_ref[...] + gb_ref[...], gn_ref[...])
    z = z_ref[...]
    gdn_out = go * (z * _sigmoid(z))
    mh = head_rms(hf_ref[...] + hb_ref[...], mn_ref[...])
    ml_out = _sigmoid(o_ref[...]) * mh
    y = x_ref[...]
    y = y + jnp.dot(na_ref[...], w_ref[0:NA_W, :], preferred_element_type=F32)
    y = y + jnp.dot(_bf(gdn_out), w_ref[NA_W:NA_W + GDN_W, :], preferred_element_type=F32)
    y = y + jnp.dot(_bf(ml_out), w_ref[NA_W + GDN_W:, :], preferred_element_type=F32)
    y_ref[...] = y


def _outproj(x, na_out, gf, gb, z, hf, hb, o, gdn_norm, mlstm_norm, w_out):
    t = x.shape[0]
    tm = min(OUT_TM, t)
    tok = lambda w: pl.BlockSpec((tm, w), lambda i: (i, 0))
    const = lambda r, w: pl.BlockSpec((r, w), lambda i: (0, 0))
    gn = jnp.tile(gdn_norm.astype(F32), GDN_HEADS).reshape(1, GDN_W)
    mn = mlstm_norm.astype(F32).reshape(1, MLSTM_W)
    return pl.pallas_call(
        _outproj_kernel,
        grid=(t // tm,),
        in_specs=[tok(D_MODEL), tok(NA_W), tok(GDN_W), tok(GDN_W), tok(GDN_W), tok(MLSTM_W), tok(MLSTM_W),
                  tok(MLSTM_W), const(1, GDN_W), const(1, MLSTM_W), const(D_MODEL, D_MODEL)],
        out_specs=tok(D_MODEL),
        out_shape=jax.ShapeDtypeStruct((t, D_MODEL), F32),
        compiler_params=_params("parallel"),
        name="outproj",
    )(x, na_out, gf, gb, z, hf, hb, o, gn, mn, _bf(w_out))


def _gate_params(bias_rows, a_log):
    pad = GATE_W - 4 * GATE_GROUP
    row0 = jnp.concatenate([b.astype(F32).reshape(-1) for b in bias_rows] + [jnp.zeros((pad,), F32)])
    row1 = jnp.concatenate([a_log.astype(F32).reshape(-1), jnp.zeros((GATE_W - GATE_GROUP,), F32)])
    return jnp.concatenate([row0[None], row1[None], jnp.zeros((6, GATE_W), F32)], axis=0)


def _trunk(xs, seq_lens, p):
    depth = p["ffn1_norm"].shape[0]
    tables = tuple(jnp.asarray(a) for a in _block_tables(seq_lens, MIX_BLOCK))
    zeros_g = jnp.zeros((2, GDN_HEADS), F32)
    out_sizes = [x.shape[0] for x in xs]
    x = None
    for i in range(depth):
        x = _ffn(xs if i == 0 else [x], p["ffn1_norm"][i], _bf(p["ffn1_w_in"][i]), _bf(p["ffn1_w_out"][i]),
                 p["final_norm"], False)
        na_qkv, gqkv, gdn_z, ml_qkv, ml_o, gates = _inproj(x, p["mix_norm"][i], _permute_w_in(p["w_in"][i]),
                                                           p["gdn_conv"][i], tables)
        na_out = _na(na_qkv, _na_bias_table(p["na_rpb"][i]), tables)
        gdn_gp = _gate_params([p["gdn_dt_bias"][i], zeros_g, zeros_g, zeros_g], p["gdn_a_log"][i])
        gf, gb = _gdn(gqkv, gates, gdn_gp, tables)
        ml_gp = _gate_params([zeros_g, zeros_g, p["mlstm_i_bias"][i], p["mlstm_f_bias"][i]], zeros_g)
        hf, hb = _mlstm(ml_qkv, gates, ml_gp, tables)
        x = _outproj(x, na_out, gf, gb, gdn_z, hf, hb, ml_o, p["gdn_norm"][i], p["mlstm_norm"][i], p["w_out"][i])
        last = i == depth - 1
        x = _ffn([x], p["ffn2_norm"][i], _bf(p["ffn2_w_in"][i]), _bf(p["ffn2_w_out"][i]), p["final_norm"],
                 last, out_sizes if last else None)
    return x if len(out_sizes) > 1 else [x]


def kernel(x_prompt, x_sample, ffn1_norm, ffn1_w_in, ffn1_w_out, mix_norm, w_in, w_out, na_rpb, gdn_conv,
           gdn_a_log, gdn_dt_bias, gdn_norm, mlstm_i_bias, mlstm_f_bias, mlstm_norm, ffn2_norm, ffn2_w_in,
           ffn2_w_out, final_norm):
    p = dict(ffn1_norm=ffn1_norm, ffn1_w_in=ffn1_w_in, ffn1_w_out=ffn1_w_out, mix_norm=mix_norm, w_in=w_in,
             w_out=w_out, na_rpb=na_rpb, gdn_conv=gdn_conv, gdn_a_log=gdn_a_log, gdn_dt_bias=gdn_dt_bias,
             gdn_norm=gdn_norm, mlstm_i_bias=mlstm_i_bias, mlstm_f_bias=mlstm_f_bias, mlstm_norm=mlstm_norm,
             ffn2_norm=ffn2_norm, ffn2_w_in=ffn2_w_in, ffn2_w_out=ffn2_w_out, final_norm=final_norm)
    bp, lp, _ = x_prompt.shape
    bs, ls, _ = x_sample.shape
    y_prompt, y_sample = _trunk([x_prompt.reshape(bp * lp, D_MODEL), x_sample.reshape(bs * ls, D_MODEL)],
                                (lp,) * bp + (ls,) * bs, p)
    return (y_prompt.reshape(bp, lp, D_MODEL), y_sample.reshape(bs, ls, D_MODEL))
```

```python
import functools

import jax
import jax.numpy as jnp
import numpy as np
from jax import lax
from jax.experimental import pallas as pl
from jax.experimental.pallas import tpu as pltpu

D_MODEL = 1024
D_FF = 2816
HEAD_DIM = 64
NA_HEADS = 4
GDN_HEADS = 6
MLSTM_HEADS = 6
NA_W = NA_HEADS * HEAD_DIM
GDN_W = GDN_HEADS * HEAD_DIM
MLSTM_W = MLSTM_HEADS * HEAD_DIM
GRID_W = 64
NA_WIN_ROWS = 8
NA_WIN_COLS = 16
GDN_CONV = 5
CHUNK = 64
M_INIT = -1e30
EPS = 1e-6
NEG = -1e30

LANES = 128
PAIR_W = 2 * HEAD_DIM
N_PAIRS = GDN_HEADS // 2
GATE_W = LANES
GATE_GROUP = 2 * GDN_HEADS
MIX_BLOCK = 512
REC_BLOCK = 1024
PART_CHUNKS = 4
VMEM_LIMIT = 56 * 1024 * 1024

F32 = jnp.float32
BF16 = jnp.bfloat16


def _bf(x):
    return x.astype(BF16)


def _mm(a, b):
    return jnp.dot(_bf(a), _bf(b), preferred_element_type=F32)


def _mm_nt(a, b):
    return lax.dot_general(_bf(a), _bf(b), (((1,), (1,)), ((), ())), preferred_element_type=F32)


def _mm_tn(a, b):
    return lax.dot_general(_bf(a), _bf(b), (((0,), (0,)), ((), ())), preferred_element_type=F32)


def _split_mm(a, b_exact, parts):
    acc = None
    rem = a
    for _ in range(parts):
        piece = _bf(rem)
        term = jnp.dot(piece, b_exact, preferred_element_type=F32)
        acc = term if acc is None else acc + term
        rem = rem - piece.astype(F32)
    return acc


def _split_mm_left(a_exact, b, parts):
    acc = None
    rem = b
    for _ in range(parts):
        piece = _bf(rem)
        term = jnp.dot(a_exact, piece, preferred_element_type=F32)
        acc = term if acc is None else acc + term
        rem = rem - piece.astype(F32)
    return acc


def _sigmoid(x):
    return 1.0 / (1.0 + jnp.exp(-x))


def _softplus(x):
    return jnp.maximum(x, 0.0) + jnp.log(1.0 + jnp.exp(-jnp.abs(x)))


def _rms(x, gain):
    return x * lax.rsqrt(jnp.mean(x * x, axis=-1, keepdims=True) + EPS) * gain


def _params(*sem):
    return pltpu.CompilerParams(dimension_semantics=sem, vmem_limit_bytes=VMEM_LIMIT)


FFN_TM = 1024
FFN_SUB = 512
FFN_TF = 256


def _ffn_kernel(*refs, final, sub, in_blocks, out_blocks):
    n_in, n_out = len(in_blocks), len(out_blocks)
    x_refs = refs[:n_in]
    g_ref, wi_ref, wo_ref, fn_ref = refs[n_in:n_in + 4]
    o_refs = refs[n_in + 4:n_in + 4 + n_out]
    act_ref = refs[-1]
    i = pl.program_id(0)
    in_starts = np.cumsum((0,) + tuple(in_blocks))
    out_starts = np.cumsum((0,) + tuple(out_blocks))
    for s in range(x_refs[0].shape[0] // sub):
        rows = slice(s * sub, (s + 1) * sub)
        x = x_refs[-1][rows, :]
        for k in range(n_in - 2, -1, -1):
            x = jnp.where(i < in_starts[k + 1], x_refs[k][rows, :], x)
        xn = _bf(_rms(x, g_ref[...]))
        for j in range(D_FF // FFN_TF):
            gate = jnp.dot(xn, wi_ref[:, j * FFN_TF:(j + 1) * FFN_TF], preferred_element_type=F32)
            up = jnp.dot(xn, wi_ref[:, D_FF + j * FFN_TF:D_FF + (j + 1) * FFN_TF], preferred_element_type=F32)
            act_ref[rows, j * FFN_TF:(j + 1) * FFN_TF] = _bf(gate * _sigmoid(gate) * up)
        y = x + 0.5 * jnp.dot(act_ref[rows, :], wo_ref[...], preferred_element_type=F32)
        if final:
            y = _rms(y, fn_ref[...])
        if n_out == 1:
            o_refs[0][rows, :] = y
        else:
            for k in range(n_out):
                @pl.when(jnp.logical_and(i >= out_starts[k], i < out_starts[k + 1]))
                def _(k=k, y=y):
                    o_refs[k][rows, :] = y


def _ffn(xs, gain, w_in, w_out, final_gain, final, out_sizes=None):
    sizes = [x.shape[0] for x in xs]
    t = sum(sizes)
    out_sizes = [t] if out_sizes is None else list(out_sizes)
    tm = min([FFN_TM] + sizes + out_sizes)
    sub = min(FFN_SUB, tm)
    in_blocks = [s // tm for s in sizes]
    out_blocks = [s // tm for s in out_sizes]

    def part_spec(blocks, k):
        start = int(np.sum(blocks[:k]))
        return pl.BlockSpec((tm, D_MODEL), lambda i: (jnp.clip(i - start, 0, blocks[k] - 1), 0))

    resident = lambda shape: pl.BlockSpec(shape, lambda i: (0, 0), pipeline_mode=pl.Buffered(1))
    outs = pl.pallas_call(
        functools.partial(_ffn_kernel, final=final, sub=sub, in_blocks=tuple(in_blocks),
                          out_blocks=tuple(out_blocks)),
        grid=(t // tm,),
        in_specs=[part_spec(in_blocks, k) for k in range(len(xs))] + [
            resident((1, D_MODEL)),
            resident((D_MODEL, 2 * D_FF)),
            resident((D_FF, D_MODEL)),
            resident((1, D_MODEL)),
        ],
        out_specs=[part_spec(out_blocks, k) for k in range(len(out_sizes))],
        out_shape=[jax.ShapeDtypeStruct((s, D_MODEL), F32) for s in out_sizes],
        scratch_shapes=[pltpu.VMEM((tm, D_FF), BF16)],
        compiler_params=_params("arbitrary"),
        name="ffn",
    )(*xs, gain.reshape(1, D_MODEL), w_in, w_out, final_gain.reshape(1, D_MODEL))
    return outs[0] if len(outs) == 1 else outs


PROJ_TM = 512
PROJ_WIDTHS = (3 * NA_W, 3 * GDN_W, GDN_W, 3 * MLSTM_W, MLSTM_W, GATE_W)
PROJ_DTYPES = (BF16, F32, BF16, F32, BF16, F32)
P_PERM = sum(PROJ_WIDTHS)


def _permute_w_in(w_in):
    o = np.cumsum([0, 3 * NA_W, 3 * GDN_W, GDN_W, 2 * GDN_HEADS, 2 * GDN_HEADS,
                   3 * MLSTM_W, MLSTM_W, 2 * MLSTM_HEADS, 2 * MLSTM_HEADS])
    n_gates = 4 * 2 * GDN_HEADS
    cols = [w_in[:, o[0]:o[3]], w_in[:, o[5]:o[7]], w_in[:, o[3]:o[5]], w_in[:, o[7]:o[9]],
            jnp.zeros((D_MODEL, GATE_W - n_gates), w_in.dtype)]
    return _bf(jnp.concatenate(cols, axis=1))


PROJ_DOT_GROUPS = ((1, 2), (0,), (3, 4), (5,))
PROJ_OFFSETS = tuple(int(v) for v in np.cumsum((0,) + PROJ_WIDTHS[:-1]))
PROJ_HALO = 16


def _head_sumsq(x):
    r = lax.broadcasted_iota(jnp.int32, (PAIR_W, PAIR_W), 0) // HEAD_DIM
    c = lax.broadcasted_iota(jnp.int32, (PAIR_W, PAIR_W), 1) // HEAD_DIM
    ones_bd = jnp.where(r == c, 1.0, 0.0).astype(BF16)
    sq = x * x
    parts = [_split_mm(sq[:, g:g + PAIR_W], ones_bd, 2) for g in range(0, x.shape[1], PAIR_W)]
    return jnp.concatenate(parts, axis=1)


def _gdn_preprocess(y, w_ref, tb):
    n = y.shape[0]
    acc = None
    for j in range(GDN_CONV):
        shift = (GDN_CONV // 2 - j) % n
        tap = y if shift == 0 else pltpu.roll(y, shift, 0)
        term = tap[PROJ_HALO:PROJ_HALO + tb] * w_ref[j:j + 1, :]
        acc = term if acc is None else acc + term
    act = acc * _sigmoid(acc)
    q = act[:, 0:GDN_W]
    k = act[:, GDN_W:2 * GDN_W]
    qn = q * lax.rsqrt(_head_sumsq(q) + EPS) * (HEAD_DIM ** -0.5)
    kn = k * lax.rsqrt(_head_sumsq(k) + EPS)
    return qn, kn, act[:, 2 * GDN_W:3 * GDN_W]


def _inproj_kernel(first_ref, last_ref, x_ref, xp_ref, xn_ref, g_ref, w_ref, cw_ref, *o_refs):
    i = pl.program_id(0)
    tb = x_ref.shape[0]
    prev = jnp.where(first_ref[i] == 1, 0.0, xp_ref[...])
    nxt = jnp.where(last_ref[i] == 1, 0.0, xn_ref[...])
    xe = jnp.concatenate([prev, x_ref[...], nxt], axis=0)
    xn = _bf(_rms(xe, g_ref[...]))
    centre = slice(PROJ_HALO, PROJ_HALO + tb)
    gdn_part = None
    for group in PROJ_DOT_GROUPS:
        off = PROJ_OFFSETS[group[0]]
        width = sum(PROJ_WIDTHS[k] for k in group)
        res = jnp.dot(xn, w_ref[:, off:off + width], preferred_element_type=F32)
        col = 0
        for k in group:
            part = res[:, col:col + PROJ_WIDTHS[k]]
            col += PROJ_WIDTHS[k]
            if k == 1:
                gdn_part = part
            else:
                o_refs[k][...] = part[centre].astype(o_refs[k].dtype)
    qn, kn, v = _gdn_preprocess(gdn_part, cw_ref, tb)
    o_refs[1][:, 0:GDN_W] = qn
    o_refs[1][:, GDN_W:2 * GDN_W] = kn
    o_refs[1][:, 2 * GDN_W:3 * GDN_W] = v


def _inproj(x, gain, w_perm, conv_w, tables):
    t = x.shape[0]
    tm = MIX_BLOCK
    per = tm // PROJ_HALO
    n_halo = t // PROJ_HALO
    first, last = tables[0], tables[1]
    const = lambda shape: pl.BlockSpec(shape, lambda i, f, l: (0, 0))
    grid_spec = pltpu.PrefetchScalarGridSpec(
        num_scalar_prefetch=2,
        grid=(t // tm,),
        in_specs=[
            pl.BlockSpec((tm, D_MODEL), lambda i, f, l: (i, 0)),
            pl.BlockSpec((PROJ_HALO, D_MODEL), lambda i, f, l: (jnp.maximum(i * per - 1, 0), 0)),
            pl.BlockSpec((PROJ_HALO, D_MODEL), lambda i, f, l: (jnp.minimum((i + 1) * per, n_halo - 1), 0)),
            const((1, D_MODEL)),
            const((D_MODEL, P_PERM)),
            const((8, 3 * GDN_W)),
        ],
        out_specs=[pl.BlockSpec((tm, w), lambda i, f, l: (i, 0)) for w in PROJ_WIDTHS],
    )
    w8 = jnp.concatenate([conv_w.astype(F32), jnp.zeros((8 - GDN_CONV, 3 * GDN_W), F32)], axis=0)
    return pl.pallas_call(
        _inproj_kernel,
        grid_spec=grid_spec,
        out_shape=[jax.ShapeDtypeStruct((t, w), d) for w, d in zip(PROJ_WIDTHS, PROJ_DTYPES)],
        compiler_params=_params("arbitrary"),
        name="inproj",
    )(first, last, x, x, x, gain.reshape(1, D_MODEL), w_perm, w8)


def _block_tables(seq_lens, block):
    first, last = [], []
    for n in seq_lens:
        nb = n // block
        first += [1] + [0] * (nb - 1)
        last += [0] * (nb - 1) + [1]
    first = np.asarray(first, np.int32)
    last = np.asarray(last, np.int32)
    idx = np.arange(len(first), dtype=np.int32)
    prev = np.where(first == 1, idx, idx - 1).astype(np.int32)
    nxt = np.where(last == 1, idx, idx + 1).astype(np.int32)
    return first, last, prev, nxt


NA_ROWS = MIX_BLOCK // GRID_W
NA_KEYS = NA_WIN_ROWS * GRID_W
NA_ROW_GROUP = 2


def _na_bias_table(rpb):
    c = np.arange(GRID_W)
    c_start = np.clip(c - NA_WIN_COLS // 2, 0, GRID_W - NA_WIN_COLS)
    in_win = (c[None, :] >= c_start[:, None]) & (c[None, :] < c_start[:, None] + NA_WIN_COLS)
    dc = np.clip(c[None, :] - c[:, None], 1 - NA_WIN_COLS, NA_WIN_COLS - 1) + NA_WIN_COLS - 1
    one_hot = jnp.asarray(dc[None] == np.arange(2 * NA_WIN_COLS - 1)[:, None, None], F32)
    by_col = jnp.einsum("hrc,cqk->hrqk", rpb.astype(F32), one_hot, precision=lax.Precision.HIGHEST)
    tab = jnp.stack([by_col[:, NA_WIN_ROWS - 1 - v:2 * NA_WIN_ROWS - 1 - v] for v in range(NA_WIN_ROWS)],
                    axis=1)
    tab = jnp.where(in_win[None, None, None], tab, NEG).transpose(0, 1, 3, 2, 4)
    return tab.reshape(NA_HEADS, NA_WIN_ROWS, GRID_W, NA_KEYS)


def _na_kernel(first_ref, last_ref, prev_ref, next_ref,
               q_ref, kp_ref, kc_ref, kn_ref, vp_ref, vc_ref, vn_ref, bias_ref, o_ref, kbuf, vbuf):
    del prev_ref, next_ref
    g = pl.program_id(0)
    kbuf[0:MIX_BLOCK] = kp_ref[...]
    kbuf[MIX_BLOCK:2 * MIX_BLOCK] = kc_ref[...]
    kbuf[2 * MIX_BLOCK:3 * MIX_BLOCK] = kn_ref[...]
    vbuf[0:MIX_BLOCK] = vp_ref[...]
    vbuf[MIX_BLOCK:2 * MIX_BLOCK] = vc_ref[...]
    vbuf[2 * MIX_BLOCK:3 * MIX_BLOCK] = vn_ref[...]
    scale = HEAD_DIM ** -0.5
    half = NA_WIN_ROWS // 2
    m0 = lax.broadcasted_iota(jnp.int32, (GRID_W, PAIR_W), 1) < HEAD_DIM
    for a0 in range(0, NA_ROWS, NA_ROW_GROUP):
        q, k, v, bias = [], [], [], []
        for a in range(a0, a0 + NA_ROW_GROUP):
            edge = (first_ref[g] if a < half else last_ref[g]) == 1
            start = pl.multiple_of(jnp.where(edge, NA_ROWS, a + half) * GRID_W, GRID_W)
            variant = jnp.where(edge, a, half)
            for h in range(NA_HEADS):
                lanes = slice((h // 2) * PAIR_W, (h // 2 + 1) * PAIR_W)
                q2 = q_ref[a * GRID_W:(a + 1) * GRID_W, lanes]
                q.append(jnp.where(m0 if h % 2 == 0 else jnp.logical_not(m0), q2, jnp.zeros_like(q2)))
                k.append(kbuf[pl.ds(start, NA_KEYS), lanes])
                v.append(vbuf[pl.ds(start, NA_KEYS), lanes])
                bias.append(bias_ref[h, variant])
        s = _each(lambda q_, k_, b_: _mm_nt(q_, k_) * scale + b_, q, k, bias)
        p = _each(lambda s_: jnp.exp(s_ - jnp.max(s_, axis=-1, keepdims=True)), s)
        l = _each(lambda p_: jnp.sum(p_, axis=-1, keepdims=True), p)
        o = _each(lambda p_, v_, l_: _mm(p_, v_) / l_, p, v, l)
        for j in range(0, len(o), 2):
            a = a0 + j // NA_HEADS
            lanes = slice(((j % NA_HEADS) // 2) * PAIR_W, ((j % NA_HEADS) // 2 + 1) * PAIR_W)
            o_ref[a * GRID_W:(a + 1) * GRID_W, lanes] = jnp.where(m0, o[j], o[j + 1]).astype(o_ref.dtype)


def _na(na_qkv, bias, tables):
    t = na_qkv.shape[0]
    nb = t // MIX_BLOCK
    blk = (MIX_BLOCK, NA_W)
    cur = lambda col: pl.BlockSpec(blk, lambda g, f, l, p, n: (g, col))
    prv = lambda col: pl.BlockSpec(blk, lambda g, f, l, p, n: (p[g], col))
    nxt = lambda col: pl.BlockSpec(blk, lambda g, f, l, p, n: (n[g], col))
    grid_spec = pltpu.PrefetchScalarGridSpec(
        num_scalar_prefetch=4,
        grid=(nb,),
        in_specs=[cur(0), prv(1), cur(1), nxt(1), prv(2), cur(2), nxt(2),
                  pl.BlockSpec((NA_HEADS, NA_WIN_ROWS, GRID_W, NA_KEYS), lambda g, f, l, p, n: (0, 0, 0, 0))],
        out_specs=pl.BlockSpec(blk, lambda g, f, l, p, n: (g, 0)),
        scratch_shapes=[pltpu.VMEM((3 * MIX_BLOCK, NA_W), BF16), pltpu.VMEM((3 * MIX_BLOCK, NA_W), BF16)],
    )
    return pl.pallas_call(
        _na_kernel,
        grid_spec=grid_spec,
        out_shape=jax.ShapeDtypeStruct((t, NA_W), BF16),
        compiler_params=_params("arbitrary"),
        name="na",
    )(*tables, na_qkv, na_qkv, na_qkv, na_qkv, na_qkv, na_qkv, na_qkv, bias)


STK = 2 * CHUNK


class _Masks:
    def __init__(self, backward):
        r = lax.broadcasted_iota(jnp.int32, (STK, STK), 0)
        c = lax.broadcasted_iota(jnp.int32, (STK, STK), 1)
        same = (r // CHUNK) == (c // CHUNK)
        if backward:
            self.incl = jnp.logical_and(same, c >= r)
            self.strict = jnp.logical_and(same, c > r)
        else:
            self.incl = jnp.logical_and(same, c <= r)
            self.strict = jnp.logical_and(same, c < r)
        self.same = same
        self.blk16 = (r // 16) == (c // 16)
        self.eye = jnp.where(r == c, 1.0, 0.0).astype(F32)
        self.last_row = 0 if backward else CHUNK - 1


def _cumsum_matrix(backward):
    r = lax.broadcasted_iota(jnp.int32, (CHUNK, CHUNK), 0)
    c = lax.broadcasted_iota(jnp.int32, (CHUNK, CHUNK), 1)
    keep = (c >= r) if backward else (c <= r)
    return jnp.where(keep, 1.0, 0.0).astype(BF16)


def _lane_half():
    return lax.broadcasted_iota(jnp.int32, (CHUNK, PAIR_W), 1) < HEAD_DIM


def _expand(tile, c0, c1, m0):
    return jnp.where(m0, tile[:, c0:c0 + 1], tile[:, c1:c1 + 1])


def _stack_cols(tile, c0, c1):
    top = jnp.broadcast_to(tile[:, c0:c0 + 1], (CHUNK, STK))
    bot = jnp.broadcast_to(tile[:, c1:c1 + 1], (CHUNK, STK))
    return jnp.concatenate([top, bot], axis=0)


def _stack(x, m0):
    return jnp.concatenate([jnp.where(m0, x, 0.0), jnp.where(m0, 0.0, x)], axis=0)


def _unstack(x):
    return x[:CHUNK] + x[CHUNK:]


def _each(fn, *lists):
    return [fn(*args) for args in zip(*lists)]


def _no_tick():
    pass


def _unit_tri_solve(a_list, mk_list, rhs_list, tick=_no_tick):
    eye = mk_list[0].eye
    d = _each(lambda a, mk: jnp.where(mk.blk16, a, 0.0), a_list, mk_list)
    rest = _each(lambda a, x: a - x, a_list, d)
    d2 = _each(lambda x: _mm(x, x), d)
    tick()
    d4 = _each(lambda x: _mm(x, x), d2)
    td = _each(lambda x, y: _mm(eye - x, eye + y), d, d2)
    tick()
    d8 = _each(lambda x: _mm(x, x), d4)
    td = _each(lambda x, y: _mm(x, eye + y), td, d4)
    tick()
    td = _each(lambda x, y: _mm(x, eye + y), td, d8)
    tick()
    n = _each(_mm, td, rest)
    tick()
    n2 = _each(lambda x: _mm(x, x), n)
    tick()
    t = _each(lambda x, y: _mm(eye - x, eye + y), n, n2)
    tick()
    t = _each(_mm, t, td)
    tick()
    return _each(_mm, t, rhs_list)


DIRS = (0, 1)
CHAINS = [(d, p) for d in DIRS for p in range(N_PAIRS)]


def _pair_lanes(p, group=0, width=GDN_W):
    return slice(group * width + p * PAIR_W, group * width + (p + 1) * PAIR_W)


def _chunk_rows(c):
    return slice(c * CHUNK, (c + 1) * CHUNK)


def _interleaved(local_fn, seq_gen):
    if seq_gen is None:
        local_fn(_no_tick)
        return
    local_fn(lambda: next(seq_gen, None))
    for _ in seq_gen:
        pass


def _part_chunks(n_chunks, part):
    m = PART_CHUNKS
    fwd = list(range(part * m, (part + 1) * m))
    bwd = list(range(n_chunks - 1 - part * m, n_chunks - 1 - (part + 1) * m, -1))
    return (fwd, bwd)


def _run_parts(n_chunks, local, seq):
    n_parts = n_chunks // PART_CHUNKS
    _interleaved(local(0), None)
    for k in range(1, n_parts):
        _interleaved(local(k), seq(k - 1))
    for _ in seq(n_parts - 1):
        pass


def _gdn_local(mks, m0, x_refs, g_refs, gp_ref, cs, chunk_ids, u_ref, wq_ref, kd_ref, p_ref, egl_ref,
               tick=_no_tick):
    bias_row = gp_ref[0:1, :]
    a_row = -jnp.exp(gp_ref[1:2, :])
    tiles = {}
    for d in DIRS:
        for j, c in enumerate(chunk_ids[d]):
            gates = g_refs[d][_chunk_rows(c), :]
            log_decay = a_row * _softplus(gates + bias_row)
            tiles[d, j] = (_split_mm_left(cs[d], log_decay, 3), _sigmoid(gates))
    tick()
    order = [(d, j, p) for j in range(len(chunk_ids[0])) for d in DIRS for p in range(N_PAIRS)]
    ch = [(d, chunk_ids[d][j], p) for d, j, p in order]
    mk = [mks[d] for d, j, p in order]
    col = [d * GDN_HEADS + 2 * p for d, j, p in order]
    gc = [tiles[d, j][0] for d, j, p in order]
    beta = [tiles[d, j][1] for d, j, p in order]
    q2 = [x_refs[d][_chunk_rows(c), _pair_lanes(p, 0)] for d, c, p in ch]
    k2 = [x_refs[d][_chunk_rows(c), _pair_lanes(p, 1)] for d, c, p in ch]
    v2 = [x_refs[d][_chunk_rows(c), _pair_lanes(p, 2)] for d, c, p in ch]

    gc2 = _each(lambda t, c0: _expand(t, c0, c0 + 1, m0), gc, col)
    b2 = _each(lambda t, c0: _expand(t, GATE_GROUP + c0, GATE_GROUP + c0 + 1, m0), beta, col)
    gl = _each(lambda g, m: g[m.last_row:m.last_row + 1, :], gc2, mk)
    eg2 = _each(jnp.exp, gc2)
    kb2 = _each(lambda k, b: k * b, k2, b2)
    vb2 = _each(lambda v, b: v * b, v2, b2)
    kbg2 = _each(lambda k, e: k * e, kb2, eg2)
    qg2 = _each(lambda q, e: q * e, q2, eg2)
    kdec2 = _each(lambda k, l, g: k * jnp.exp(l - g), k2, gl, gc2)
    cst = _each(lambda t, c0: _stack_cols(t, c0, c0 + 1), gc, col)
    decay = _each(lambda c_, m: jnp.exp(jnp.where(m.incl, c_ - c_.T, NEG)), cst, mk)
    aq = _each(lambda kb, q, k: _mm_nt(jnp.concatenate([kb, kb, q, q], axis=0), _stack(k, m0)), kb2, q2, k2)
    tick()
    a = _each(lambda x, dc, m: jnp.where(m.strict, x[:STK] * dc, 0.0), aq, decay, mk)
    pm = _each(lambda x, dc: x[STK:] * dc, aq, decay)
    rhs = _each(lambda vb, kbg: jnp.concatenate([_stack(vb, m0), _stack(kbg, m0)], axis=1), vb2, kbg2)
    uw = _unit_tri_solve(a, mk, rhs, tick)
    tick()
    for i, (d, c, p) in enumerate(ch):
        lanes = _pair_lanes(p)
        u_ref[d, c, :, lanes] = _unstack(uw[i][:, :PAIR_W])
        wq_ref[d, c, 0:CHUNK, lanes] = _bf(_unstack(uw[i][:, PAIR_W:]))
        wq_ref[d, c, CHUNK:STK, lanes] = _bf(qg2[i])
        kd_ref[d, c, :, lanes] = _bf(kdec2[i])
        p_ref[d, c, p] = _bf(pm[i])
        egl_ref[d, c, :, lanes] = jnp.exp(gl[i])


def _gdn_seq(mks, m0, chunk_ids, o_refs, s_ref, u_ref, wq_ref, kd_ref, p_ref, egl_ref):
    for j in range(len(chunk_ids[0])):
        cidx = [chunk_ids[d][j] for d in DIRS]
        s = [s_ref[d, p] for d, p in CHAINS]
        wq = [_mm(wq_ref[d, cidx[d], :, _pair_lanes(p)], s_) for (d, p), s_ in zip(CHAINS, s)]
        yield
        vn = [u_ref[d, cidx[d], :, _pair_lanes(p)] - x[:CHUNK] for (d, p), x in zip(CHAINS, wq)]
        pv = [_mm(p_ref[d, cidx[d], p], _stack(v, m0)) for (d, p), v in zip(CHAINS, vn)]
        kv = [_mm_tn(kd_ref[d, cidx[d], :, _pair_lanes(p)], v) for (d, p), v in zip(CHAINS, vn)]
        yield
        for i, (d, p) in enumerate(CHAINS):
            o_refs[d][_chunk_rows(cidx[d]), _pair_lanes(p)] = (wq[i][CHUNK:] + _unstack(pv[i])).astype(o_refs[d].dtype)
            egl = egl_ref[d, cidx[d], :, _pair_lanes(p)]
            s_ref[d, p] = egl * s[i] + jnp.where(mks[d].same, kv[i], 0.0)


def _gdn_kernel(first_ref, last_ref, xf_ref, xb_ref, gf_ref, gb_ref, gp_ref, of_ref, ob_ref,
                s_ref, u_ref, wq_ref, kd_ref, p_ref, egl_ref):
    i = pl.program_id(0)
    nb = pl.num_programs(0)
    n_chunks = xf_ref.shape[0] // CHUNK

    @pl.when(first_ref[i] == 1)
    def _():
        s_ref[0] = jnp.zeros_like(s_ref[0])

    @pl.when(last_ref[nb - 1 - i] == 1)
    def _():
        s_ref[1] = jnp.zeros_like(s_ref[1])

    m0 = _lane_half()
    mks = [_Masks(backward=False), _Masks(backward=True)]
    cs = [_cumsum_matrix(backward=False), _cumsum_matrix(backward=True)]
    scratch = (u_ref, wq_ref, kd_ref, p_ref, egl_ref)
    def local(k):
        return lambda tick: _gdn_local(mks, m0, (xf_ref, xb_ref), (gf_ref, gb_ref), gp_ref, cs,
                                       _part_chunks(n_chunks, k), *scratch, tick=tick)

    def seq(k):
        return _gdn_seq(mks, m0, _part_chunks(n_chunks, k), (of_ref, ob_ref), s_ref, *scratch)

    _run_parts(n_chunks, local, seq)


def _gdn(gqkv, gates, gate_params, tables):
    t = gqkv.shape[0]
    nb = t // REC_BLOCK
    nc = REC_BLOCK // CHUNK
    first, last = tables[0], tables[1]
    fwd = lambda w: pl.BlockSpec((REC_BLOCK, w), lambda i, f, l: (i, 0))
    bwd = lambda w: pl.BlockSpec((REC_BLOCK, w), lambda i, f, l: (nb - 1 - i, 0))
    grid_spec = pltpu.PrefetchScalarGridSpec(
        num_scalar_prefetch=2,
        grid=(nb,),
        in_specs=[fwd(3 * GDN_W), bwd(3 * GDN_W), fwd(GATE_W), bwd(GATE_W),
                  pl.BlockSpec((8, GATE_W), lambda i, f, l: (0, 0))],
        out_specs=[fwd(GDN_W), bwd(GDN_W)],
        scratch_shapes=[pltpu.VMEM((2, N_PAIRS, STK, STK), F32),
                        pltpu.VMEM((2, nc, CHUNK, GDN_W), F32),
                        pltpu.VMEM((2, nc, STK, GDN_W), BF16),
                        pltpu.VMEM((2, nc, CHUNK, GDN_W), BF16),
                        pltpu.VMEM((2, nc, N_PAIRS, STK, STK), BF16),
                        pltpu.VMEM((2, nc, 1, GDN_W), F32)],
    )
    return pl.pallas_call(
        _gdn_kernel,
        grid_spec=grid_spec,
        out_shape=[jax.ShapeDtypeStruct((t, GDN_W), BF16)] * 2,
        compiler_params=_params("arbitrary"),
        name="gdn",
    )(first, last, gqkv, gqkv, gates, gates, gate_params)


def _mlstm_local(mks, m0, bd2, x_refs, g_refs, gp_ref, cs, chunk_ids, upd_ref, intra_ref, mi_ref, fc_ref, fl_ref,
                 ml_ref, tick=_no_tick):
    bias_row = gp_ref[0:1, :]
    tiles = {}
    for d in DIRS:
        for j, c in enumerate(chunk_ids[d]):
            li = g_refs[d][_chunk_rows(c), :] + bias_row
            tiles[d, j] = (_split_mm_left(cs[d], -_softplus(-li), 3), li)
    tick()
    order = [(d, j, p) for j in range(len(chunk_ids[0])) for d in DIRS for p in range(N_PAIRS)]
    ch = [(d, chunk_ids[d][j], p) for d, j, p in order]
    mk = [mks[d] for d, j, p in order]
    col_i = [2 * GATE_GROUP + d * MLSTM_HEADS + 2 * p for d, j, p in order]
    col_f = [3 * GATE_GROUP + d * MLSTM_HEADS + 2 * p for d, j, p in order]
    fc = [tiles[d, j][0] for d, j, p in order]
    li = [tiles[d, j][1] for d, j, p in order]
    q2 = [x_refs[d][_chunk_rows(c), _pair_lanes(p, 0, MLSTM_W)] for d, c, p in ch]
    k2 = [x_refs[d][_chunk_rows(c), _pair_lanes(p, 1, MLSTM_W)] * (HEAD_DIM ** -0.5) for d, c, p in ch]
    v2 = [x_refs[d][_chunk_rows(c), _pair_lanes(p, 2, MLSTM_W)] for d, c, p in ch]
    ones = jnp.ones((CHUNK, PAIR_W), F32)
    ones_st = _stack(ones, m0)

    fc2 = _each(lambda t, c0: _expand(t, c0, c0 + 1, m0), fc, col_f)
    li2 = _each(lambda t, c0: _expand(t, c0, c0 + 1, m0), li, col_i)
    fl = _each(lambda f, m: f[m.last_row:m.last_row + 1, :], fc2, mk)
    g_end = _each(lambda l, f, i_: l - f + i_, fl, fc2, li2)
    ml = _each(lambda g: jnp.max(g, axis=0, keepdims=True), g_end)
    kw2 = _each(lambda k, g, m: k * jnp.exp(g - m), k2, g_end, ml)
    upd = _each(lambda kw, v: jnp.where(bd2, _mm_tn(kw, jnp.concatenate([v, ones], axis=1)), 0.0), kw2, v2)
    tick()
    cst = _each(lambda t, c0: _stack_cols(t, c0, c0 + 1), fc, col_f)
    est = _each(lambda c_, t, c0: c_ - _stack_cols(t, c0, c0 + 1), cst, li, col_i)
    dmat = _each(lambda c_, e, m: jnp.where(m.incl, c_ - e.T, NEG), cst, est, mk)
    m_intra = _each(lambda x: jnp.max(x, axis=1, keepdims=True), dmat)
    tick()
    qk = _each(lambda q, k: _mm_nt(jnp.concatenate([q, q], axis=0), _stack(k, m0)), q2, k2)
    tick()
    s_loc = _each(lambda x, dm, mi: x * jnp.exp(dm - mi), qk, dmat, m_intra)
    tick()
    intra = _each(lambda s, v: _unstack(_mm(s, jnp.concatenate([_stack(v, m0), ones_st], axis=1))), s_loc, v2)
    tick()
    mi2 = _each(lambda mi: jnp.where(m0, jnp.broadcast_to(mi[:CHUNK], (CHUNK, PAIR_W)),
                                     jnp.broadcast_to(mi[CHUNK:], (CHUNK, PAIR_W))), m_intra)
    for i, (d, c, p) in enumerate(ch):
        lanes = _pair_lanes(p)
        upd_ref[d, c, p] = upd[i]
        intra_ref[d, c, p] = intra[i]
        mi_ref[d, c, :, lanes] = mi2[i]
        fc_ref[d, c, :, lanes] = fc2[i]
        fl_ref[d, c, :, lanes] = fl[i]
        ml_ref[d, c, :, lanes] = ml[i]
        if i % N_PAIRS == N_PAIRS - 1:
            tick()


def _mlstm_seq(chunk_ids, x_refs, o_refs, c_ref, m_ref, upd_ref, intra_ref, mi_ref, fc_ref, fl_ref, ml_ref):
    for j, d, p in [(j, d, p) for j in range(len(chunk_ids[0])) for d, p in CHAINS]:
        c = chunk_ids[d][j]
        lanes = _pair_lanes(p)
        m_st = m_ref[d, p]
        cn = c_ref[d, p]
        fl = fl_ref[d, c, :, lanes]
        ml = ml_ref[d, c, :, lanes]
        m_new = jnp.maximum(fl + m_st, ml)
        a_row = jnp.exp(fl + m_st - m_new)
        b_row = jnp.exp(ml - m_new)
        m_inter = fc_ref[d, c, :, lanes] + m_st
        mi2 = mi_ref[d, c, :, lanes]
        m_t = jnp.maximum(m_inter, mi2)
        w_intra = jnp.exp(mi2 - m_t)
        w_inter = jnp.exp(m_inter - m_t)
        inter = _mm(x_refs[d][_chunk_rows(c), _pair_lanes(p, 0, MLSTM_W)], cn)
        intra = intra_ref[d, c, p]
        num = w_intra * intra[:, :PAIR_W] + w_inter * inter[:, :PAIR_W]
        den = w_intra * intra[:, PAIR_W:] + w_inter * inter[:, PAIR_W:]
        out = num / jnp.maximum(jnp.abs(den), jnp.exp(-m_t))
        o_refs[d][_chunk_rows(c), lanes] = out.astype(o_refs[d].dtype)
        c_ref[d, p] = (jnp.concatenate([a_row, a_row], axis=1) * cn
                       + jnp.concatenate([b_row, b_row], axis=1) * upd_ref[d, c, p])
        m_ref[d, p] = m_new
        if p == N_PAIRS - 1:
            yield


def _mlstm_kernel(first_ref, last_ref, xf_ref, xb_ref, gf_ref, gb_ref, gp_ref, of_ref, ob_ref,
                  c_ref, m_ref, upd_ref, intra_ref, mi_ref, fc_ref, fl_ref, ml_ref):
    i = pl.program_id(0)
    nb = pl.num_programs(0)
    n_chunks = xf_ref.shape[0] // CHUNK

    @pl.when(first_ref[i] == 1)
    def _():
        c_ref[0] = jnp.zeros_like(c_ref[0])
        m_ref[0] = jnp.full_like(m_ref[0], M_INIT)

    @pl.when(last_ref[nb - 1 - i] == 1)
    def _():
        c_ref[1] = jnp.zeros_like(c_ref[1])
        m_ref[1] = jnp.full_like(m_ref[1], M_INIT)

    m0 = _lane_half()
    r = lax.broadcasted_iota(jnp.int32, (STK, 2 * STK), 0) // HEAD_DIM
    c = (lax.broadcasted_iota(jnp.int32, (STK, 2 * STK), 1) % STK) // HEAD_DIM
    bd2 = r == c
    mks = [_Masks(backward=False), _Masks(backward=True)]
    cs = [_cumsum_matrix(backward=False), _cumsum_matrix(backward=True)]
    scratch = (upd_ref, intra_ref, mi_ref, fc_ref, fl_ref, ml_ref)

    def local(k):
        return lambda tick: _mlstm_local(mks, m0, bd2, (xf_ref, xb_ref), (gf_ref, gb_ref), gp_ref, cs,
                                         _part_chunks(n_chunks, k), *scratch, tick=tick)

    def seq(k):
        return _mlstm_seq(_part_chunks(n_chunks, k), (xf_ref, xb_ref), (of_ref, ob_ref), c_ref, m_ref, *scratch)

    _run_parts(n_chunks, local, seq)


def _mlstm(ml_qkv, gates, gate_params, tables):
    t = ml_qkv.shape[0]
    nb = t // REC_BLOCK
    nc = REC_BLOCK // CHUNK
    first, last = tables[0], tables[1]
    fwd = lambda w: pl.BlockSpec((REC_BLOCK, w), lambda i, f, l: (i, 0))
    bwd = lambda w: pl.BlockSpec((REC_BLOCK, w), lambda i, f, l: (nb - 1 - i, 0))
    grid_spec = pltpu.PrefetchScalarGridSpec(
        num_scalar_prefetch=2,
        grid=(nb,),
        in_specs=[fwd(3 * MLSTM_W), bwd(3 * MLSTM_W), fwd(GATE_W), bwd(GATE_W),
                  pl.BlockSpec((8, GATE_W), lambda i, f, l: (0, 0))],
        out_specs=[fwd(MLSTM_W), bwd(MLSTM_W)],
        scratch_shapes=[pltpu.VMEM((2, N_PAIRS, STK, 2 * STK), F32),
                        pltpu.VMEM((2, N_PAIRS, 1, PAIR_W), F32),
                        pltpu.VMEM((2, nc, N_PAIRS, STK, 2 * STK), F32),
                        pltpu.VMEM((2, nc, N_PAIRS, CHUNK, 2 * PAIR_W), F32),
                        pltpu.VMEM((2, nc, CHUNK, MLSTM_W), F32),
                        pltpu.VMEM((2, nc, CHUNK, MLSTM_W), F32),
                        pltpu.VMEM((2, nc, 1, MLSTM_W), F32),
                        pltpu.VMEM((2, nc, 1, MLSTM_W), F32)],
    )
    return pl.pallas_call(
        _mlstm_kernel,
        grid_spec=grid_spec,
        out_shape=[jax.ShapeDtypeStruct((t, MLSTM_W), BF16)] * 2,
        compiler_params=_params("arbitrary"),
        name="mlstm",
    )(first, last, ml_qkv, ml_qkv, gates, gates, gate_params)


OUT_TM = 512


def _outproj_kernel(x_ref, na_ref, gf_ref, gb_ref, z_ref, hf_ref, hb_ref, o_ref, gn_ref, mn_ref,
                    w_ref, y_ref):
    def head_rms(v, gain):
        ms = _head_sumsq(v) * (1.0 / HEAD_DIM)
        return v * lax.rsqrt(ms + EPS) * gain

    go = head_rms(gf_ref[...].astype(F32) + gb_ref[...].astype(F32), gn_ref[...])
    z = z_ref[...].astype(F32)
    gdn_out = go * (z * _sigmoid(z))
    mh = head_rms(hf_ref[...].astype(F32) + hb_ref[...].astype(F32), mn_ref[...])
    ml_out = _sigmoid(o_ref[...].astype(F32)) * mh
    y = x_ref[...]
    y = y + jnp.dot(na_ref[...], w_ref[0:NA_W, :], preferred_element_type=F32)
    y = y + jnp.dot(_bf(gdn_out), w_ref[NA_W:NA_W + GDN_W, :], preferred_element_type=F32)
    y = y + jnp.dot(_bf(ml_out), w_ref[NA_W + GDN_W:, :], preferred_element_type=F32)
    y_ref[...] = y


def _outproj(x, na_out, gf, gb, z, hf, hb, o, gdn_norm, mlstm_norm, w_out):
    t = x.shape[0]
    tm = min(OUT_TM, t)
    tok = lambda w: pl.BlockSpec((tm, w), lambda i: (i, 0))
    const = lambda r, w: pl.BlockSpec((r, w), lambda i: (0, 0))
    gn = jnp.tile(gdn_norm.astype(F32), GDN_HEADS).reshape(1, GDN_W)
    mn = mlstm_norm.astype(F32).reshape(1, MLSTM_W)
    return pl.pallas_call(
        _outproj_kernel,
        grid=(t // tm,),
        in_specs=[tok(D_MODEL), tok(NA_W), tok(GDN_W), tok(GDN_W), tok(GDN_W), tok(MLSTM_W), tok(MLSTM_W),
                  tok(MLSTM_W), const(1, GDN_W), const(1, MLSTM_W), const(D_MODEL, D_MODEL)],
        out_specs=tok(D_MODEL),
        out_shape=jax.ShapeDtypeStruct((t, D_MODEL), F32),
        compiler_params=_params("parallel"),
        name="outproj",
    )(x, na_out, gf, gb, z, hf, hb, o, gn, mn, _bf(w_out))


def _gate_params(bias_rows, a_log):
    pad = GATE_W - 4 * GATE_GROUP
    row0 = jnp.concatenate([b.astype(F32).reshape(-1) for b in bias_rows] + [jnp.zeros((pad,), F32)])
    row1 = jnp.concatenate([a_log.astype(F32).reshape(-1), jnp.zeros((GATE_W - GATE_GROUP,), F32)])
    return jnp.concatenate([row0[None], row1[None], jnp.zeros((6, GATE_W), F32)], axis=0)


def _trunk(xs, seq_lens, p):
    depth = p["ffn1_norm"].shape[0]
    tables = tuple(jnp.asarray(a) for a in _block_tables(seq_lens, MIX_BLOCK))
    rec_tables = tuple(jnp.asarray(a) for a in _block_tables(seq_lens, REC_BLOCK))
    zeros_g = jnp.zeros((2, GDN_HEADS), F32)
    out_sizes = [x.shape[0] for x in xs]
    x = None
    for i in range(depth):
        x = _ffn(xs if i == 0 else [x], p["ffn1_norm"][i], _bf(p["ffn1_w_in"][i]), _bf(p["ffn1_w_out"][i]),
                 p["final_norm"], False)
        na_qkv, gqkv, gdn_z, ml_qkv, ml_o, gates = _inproj(x, p["mix_norm"][i], _permute_w_in(p["w_in"][i]),
                                                           p["gdn_conv"][i], tables)
        na_out = _na(na_qkv, _na_bias_table(p["na_rpb"][i]), tables)
        gdn_gp = _gate_params([p["gdn_dt_bias"][i], zeros_g, zeros_g, zeros_g], p["gdn_a_log"][i])
        gf, gb = _gdn(gqkv, gates, gdn_gp, rec_tables)
        ml_gp = _gate_params([zeros_g, zeros_g, p["mlstm_i_bias"][i], p["mlstm_f_bias"][i]], zeros_g)
        hf, hb = _mlstm(ml_qkv, gates, ml_gp, rec_tables)
        x = _outproj(x, na_out, gf, gb, gdn_z, hf, hb, ml_o, p["gdn_norm"][i], p["mlstm_norm"][i], p["w_out"][i])
        last = i == depth - 1
        x = _ffn([x], p["ffn2_norm"][i], _bf(p["ffn2_w_in"][i]), _bf(p["ffn2_w_out"][i]), p["final_norm"],
                 last, out_sizes if last else None)
    return x if len(out_sizes) > 1 else [x]


def kernel(x_prompt, x_sample, ffn1_norm, ffn1_w_in, ffn1_w_out, mix_norm, w_in, w_out, na_rpb, gdn_conv,
           gdn_a_log, gdn_dt_bias, gdn_norm, mlstm_i_bias, mlstm_f_bias, mlstm_norm, ffn2_norm, ffn2_w_in,
           ffn2_w_out, final_norm):
    p = dict(ffn1_norm=ffn1_norm, ffn1_w_in=ffn1_w_in, ffn1_w_out=ffn1_w_out, mix_norm=mix_norm, w_in=w_in,
             w_out=w_out, na_rpb=na_rpb, gdn_conv=gdn_conv, gdn_a_log=gdn_a_log, gdn_dt_bias=gdn_dt_bias,
             gdn_norm=gdn_norm, mlstm_i_bias=mlstm_i_bias, mlstm_f_bias=mlstm_f_bias, mlstm_norm=mlstm_norm,
             ffn2_norm=ffn2_norm, ffn2_w_in=ffn2_w_in, ffn2_w_out=ffn2_w_out, final_norm=final_norm)
    bp, lp, _ = x_prompt.shape
    bs, ls, _ = x_sample.shape
    y_prompt, y_sample = _trunk([x_prompt.reshape(bp * lp, D_MODEL), x_sample.reshape(bs * ls, D_MODEL)],
                                (lp,) * bp + (ls,) * bs, p)
    return (y_prompt.reshape(bp, lp, D_MODEL), y_sample.reshape(bs, ls, D_MODEL))
```

```python
import functools

import jax
import jax.numpy as jnp
import numpy as np
from jax import lax
from jax.experimental import pallas as pl
from jax.experimental.pallas import tpu as pltpu

D_MODEL = 1024
D_FF = 2816
HEAD_DIM = 64
NA_HEADS = 4
GDN_HEADS = 6
MLSTM_HEADS = 6
NA_W = NA_HEADS * HEAD_DIM
GDN_W = GDN_HEADS * HEAD_DIM
MLSTM_W = MLSTM_HEADS * HEAD_DIM
GRID_W = 64
NA_WIN_ROWS = 8
NA_WIN_COLS = 16
GDN_CONV = 5
CHUNK = 64
M_INIT = -1e30
EPS = 1e-6
NEG = -1e30

LANES = 128
PAIR_W = 2 * HEAD_DIM
N_PAIRS = GDN_HEADS // 2
GATE_W = LANES
GATE_GROUP = 2 * GDN_HEADS
MIX_BLOCK = 512
REC_BLOCK = 1024
PART_CHUNKS = 4
VMEM_LIMIT = 56 * 1024 * 1024

F32 = jnp.float32
BF16 = jnp.bfloat16


def _bf(x):
    return x.astype(BF16)


def _mm(a, b):
    return jnp.dot(_bf(a), _bf(b), preferred_element_type=F32)


def _mm_nt(a, b):
    return lax.dot_general(_bf(a), _bf(b), (((1,), (1,)), ((), ())), preferred_element_type=F32)


def _mm_tn(a, b):
    return lax.dot_general(_bf(a), _bf(b), (((0,), (0,)), ((), ())), preferred_element_type=F32)


def _split_mm(a, b_exact, parts):
    acc = None
    rem = a
    for _ in range(parts):
        piece = _bf(rem)
        term = jnp.dot(piece, b_exact, preferred_element_type=F32)
        acc = term if acc is None else acc + term
        rem = rem - piece.astype(F32)
    return acc


def _split_mm_left(a_exact, b, parts):
    acc = None
    rem = b
    for _ in range(parts):
        piece = _bf(rem)
        term = jnp.dot(a_exact, piece, preferred_element_type=F32)
        acc = term if acc is None else acc + term
        rem = rem - piece.astype(F32)
    return acc


def _sigmoid(x):
    return 1.0 / (1.0 + jnp.exp(-x))


def _softplus(x):
    return jnp.maximum(x, 0.0) + jnp.log(1.0 + jnp.exp(-jnp.abs(x)))


def _rms(x, gain):
    return x * lax.rsqrt(jnp.mean(x * x, axis=-1, keepdims=True) + EPS) * gain


def _params(*sem):
    return pltpu.CompilerParams(dimension_semantics=sem, vmem_limit_bytes=VMEM_LIMIT)


FFN_TM = 1024
FFN_SUB = 512
FFN_TF = 256


def _ffn_kernel(*refs, final, sub, in_blocks, out_blocks):
    n_in, n_out = len(in_blocks), len(out_blocks)
    x_refs = refs[:n_in]
    g_ref, wi_ref, wo_ref, fn_ref = refs[n_in:n_in + 4]
    o_refs = refs[n_in + 4:n_in + 4 + n_out]
    act_ref = refs[-1]
    i = pl.program_id(0)
    in_starts = np.cumsum((0,) + tuple(in_blocks))
    out_starts = np.cumsum((0,) + tuple(out_blocks))
    for s in range(x_refs[0].shape[0] // sub):
        rows = slice(s * sub, (s + 1) * sub)
        x = x_refs[-1][rows, :]
        for k in range(n_in - 2, -1, -1):
            x = jnp.where(i < in_starts[k + 1], x_refs[k][rows, :], x)
        xn = _bf(_rms(x, g_ref[...]))
        for j in range(D_FF // FFN_TF):
            gate = jnp.dot(xn, wi_ref[:, j * FFN_TF:(j + 1) * FFN_TF], preferred_element_type=F32)
            up = jnp.dot(xn, wi_ref[:, D_FF + j * FFN_TF:D_FF + (j + 1) * FFN_TF], preferred_element_type=F32)
            act_ref[rows, j * FFN_TF:(j + 1) * FFN_TF] = _bf(gate * _sigmoid(gate) * up)
        y = x + 0.5 * jnp.dot(act_ref[rows, :], wo_ref[...], preferred_element_type=F32)
        if final:
            y = _rms(y, fn_ref[...])
        if n_out == 1:
            o_refs[0][rows, :] = y
        else:
            for k in range(n_out):
                @pl.when(jnp.logical_and(i >= out_starts[k], i < out_starts[k + 1]))
                def _(k=k, y=y):
                    o_refs[k][rows, :] = y


def _ffn(xs, gain, w_in, w_out, final_gain, final, out_sizes=None):
    sizes = [x.shape[0] for x in xs]
    t = sum(sizes)
    out_sizes = [t] if out_sizes is None else list(out_sizes)
    tm = min([FFN_TM] + sizes + out_sizes)
    sub = min(FFN_SUB, tm)
    in_blocks = [s // tm for s in sizes]
    out_blocks = [s // tm for s in out_sizes]

    def part_spec(blocks, k):
        start = int(np.sum(blocks[:k]))
        return pl.BlockSpec((tm, D_MODEL), lambda i: (jnp.clip(i - start, 0, blocks[k] - 1), 0))

    resident = lambda shape: pl.BlockSpec(shape, lambda i: (0, 0), pipeline_mode=pl.Buffered(1))
    outs = pl.pallas_call(
        functools.partial(_ffn_kernel, final=final, sub=sub, in_blocks=tuple(in_blocks),
                          out_blocks=tuple(out_blocks)),
        grid=(t // tm,),
        in_specs=[part_spec(in_blocks, k) for k in range(len(xs))] + [
            resident((1, D_MODEL)),
            resident((D_MODEL, 2 * D_FF)),
            resident((D_FF, D_MODEL)),
            resident((1, D_MODEL)),
        ],
        out_specs=[part_spec(out_blocks, k) for k in range(len(out_sizes))],
        out_shape=[jax.ShapeDtypeStruct((s, D_MODEL), F32) for s in out_sizes],
        scratch_shapes=[pltpu.VMEM((tm, D_FF), BF16)],
        compiler_params=_params("arbitrary"),
        name="ffn",
    )(*xs, gain.reshape(1, D_MODEL), w_in, w_out, final_gain.reshape(1, D_MODEL))
    return outs[0] if len(outs) == 1 else outs


PROJ_TM = 512
PROJ_WIDTHS = (3 * NA_W, 3 * GDN_W, GDN_W, 3 * MLSTM_W, MLSTM_W, GATE_W)
PROJ_DTYPES = (BF16, F32, BF16, F32, BF16, F32)
P_PERM = sum(PROJ_WIDTHS)


def _permute_w_in(w_in):
    o = np.cumsum([0, 3 * NA_W, 3 * GDN_W, GDN_W, 2 * GDN_HEADS, 2 * GDN_HEADS,
                   3 * MLSTM_W, MLSTM_W, 2 * MLSTM_HEADS, 2 * MLSTM_HEADS])
    n_gates = 4 * 2 * GDN_HEADS
    cols = [w_in[:, o[0]:o[3]], w_in[:, o[5]:o[7]], w_in[:, o[3]:o[5]], w_in[:, o[7]:o[9]],
            jnp.zeros((D_MODEL, GATE_W - n_gates), w_in.dtype)]
    return _bf(jnp.concatenate(cols, axis=1))


PROJ_DOT_GROUPS = ((1, 2), (0,), (3, 4), (5,))
PROJ_OFFSETS = tuple(int(v) for v in np.cumsum((0,) + PROJ_WIDTHS[:-1]))
PROJ_HALO = 16


def _head_sumsq(x):
    r = lax.broadcasted_iota(jnp.int32, (PAIR_W, PAIR_W), 0) // HEAD_DIM
    c = lax.broadcasted_iota(jnp.int32, (PAIR_W, PAIR_W), 1) // HEAD_DIM
    ones_bd = jnp.where(r == c, 1.0, 0.0).astype(BF16)
    sq = x * x
    parts = [_split_mm(sq[:, g:g + PAIR_W], ones_bd, 2) for g in range(0, x.shape[1], PAIR_W)]
    return jnp.concatenate(parts, axis=1)


def _gdn_preprocess(y, w_ref, tb):
    n = y.shape[0]
    acc = None
    for j in range(GDN_CONV):
        shift = (GDN_CONV // 2 - j) % n
        tap = y if shift == 0 else pltpu.roll(y, shift, 0)
        term = tap[PROJ_HALO:PROJ_HALO + tb] * w_ref[j:j + 1, :]
        acc = term if acc is None else acc + term
    act = acc * _sigmoid(acc)
    q = act[:, 0:GDN_W]
    k = act[:, GDN_W:2 * GDN_W]
    qn = q * lax.rsqrt(_head_sumsq(q) + EPS) * (HEAD_DIM ** -0.5)
    kn = k * lax.rsqrt(_head_sumsq(k) + EPS)
    return qn, kn, act[:, 2 * GDN_W:3 * GDN_W]


def _inproj_kernel(first_ref, last_ref, x_ref, xp_ref, xn_ref, g_ref, w_ref, cw_ref, *o_refs):
    i = pl.program_id(0)
    tb = x_ref.shape[0]
    prev = jnp.where(first_ref[i] == 1, 0.0, xp_ref[...])
    nxt = jnp.where(last_ref[i] == 1, 0.0, xn_ref[...])
    xe = jnp.concatenate([prev, x_ref[...], nxt], axis=0)
    xn = _bf(_rms(xe, g_ref[...]))
    centre = slice(PROJ_HALO, PROJ_HALO + tb)
    gdn_part = None
    for group in PROJ_DOT_GROUPS:
        off = PROJ_OFFSETS[group[0]]
        width = sum(PROJ_WIDTHS[k] for k in group)
        res = jnp.dot(xn, w_ref[:, off:off + width], preferred_element_type=F32)
        col = 0
        for k in group:
            part = res[:, col:col + PROJ_WIDTHS[k]]
            col += PROJ_WIDTHS[k]
            if k == 1:
                gdn_part = part
            else:
                o_refs[k][...] = part[centre].astype(o_refs[k].dtype)
    qn, kn, v = _gdn_preprocess(gdn_part, cw_ref, tb)
    o_refs[1][:, 0:GDN_W] = qn
    o_refs[1][:, GDN_W:2 * GDN_W] = kn
    o_refs[1][:, 2 * GDN_W:3 * GDN_W] = v


def _inproj(x, gain, w_perm, conv_w, tables):
    t = x.shape[0]
    tm = MIX_BLOCK
    per = tm // PROJ_HALO
    n_halo = t // PROJ_HALO
    first, last = tables[0], tables[1]
    const = lambda shape: pl.BlockSpec(shape, lambda i, f, l: (0, 0))
    grid_spec = pltpu.PrefetchScalarGridSpec(
        num_scalar_prefetch=2,
        grid=(t // tm,),
        in_specs=[
            pl.BlockSpec((tm, D_MODEL), lambda i, f, l: (i, 0)),
            pl.BlockSpec((PROJ_HALO, D_MODEL), lambda i, f, l: (jnp.maximum(i * per - 1, 0), 0)),
            pl.BlockSpec((PROJ_HALO, D_MODEL), lambda i, f, l: (jnp.minimum((i + 1) * per, n_halo - 1), 0)),
            const((1, D_MODEL)),
            const((D_MODEL, P_PERM)),
            const((8, 3 * GDN_W)),
        ],
        out_specs=[pl.BlockSpec((tm, w), lambda i, f, l: (i, 0)) for w in PROJ_WIDTHS],
    )
    w8 = jnp.concatenate([conv_w.astype(F32), jnp.zeros((8 - GDN_CONV, 3 * GDN_W), F32)], axis=0)
    return pl.pallas_call(
        _inproj_kernel,
        grid_spec=grid_spec,
        out_shape=[jax.ShapeDtypeStruct((t, w), d) for w, d in zip(PROJ_WIDTHS, PROJ_DTYPES)],
        compiler_params=_params("arbitrary"),
        name="inproj",
    )(first, last, x, x, x, gain.reshape(1, D_MODEL), w_perm, w8)


def _block_tables(seq_lens, block):
    first, last = [], []
    for n in seq_lens:
        nb = n // block
        first += [1] + [0] * (nb - 1)
        last += [0] * (nb - 1) + [1]
    first = np.asarray(first, np.int32)
    last = np.asarray(last, np.int32)
    idx = np.arange(len(first), dtype=np.int32)
    prev = np.where(first == 1, idx, idx - 1).astype(np.int32)
    nxt = np.where(last == 1, idx, idx + 1).astype(np.int32)
    return first, last, prev, nxt


NA_ROWS = MIX_BLOCK // GRID_W
NA_KEYS = NA_WIN_ROWS * GRID_W
NA_ROW_GROUP = 2


def _na_bias_table(rpb):
    c = np.arange(GRID_W)
    c_start = np.clip(c - NA_WIN_COLS // 2, 0, GRID_W - NA_WIN_COLS)
    in_win = (c[None, :] >= c_start[:, None]) & (c[None, :] < c_start[:, None] + NA_WIN_COLS)
    dc = np.clip(c[None, :] - c[:, None], 1 - NA_WIN_COLS, NA_WIN_COLS - 1) + NA_WIN_COLS - 1
    one_hot = jnp.asarray(dc[None] == np.arange(2 * NA_WIN_COLS - 1)[:, None, None], F32)
    by_col = jnp.einsum("hrc,cqk->hrqk", rpb.astype(F32), one_hot, precision=lax.Precision.HIGHEST)
    tab = jnp.stack([by_col[:, NA_WIN_ROWS - 1 - v:2 * NA_WIN_ROWS - 1 - v] for v in range(NA_WIN_ROWS)],
                    axis=1)
    tab = jnp.where(in_win[None, None, None], tab, NEG).transpose(0, 1, 3, 2, 4)
    return tab.reshape(NA_HEADS, NA_WIN_ROWS, GRID_W, NA_KEYS)


def _na_kernel(first_ref, last_ref, prev_ref, next_ref,
               q_ref, kp_ref, kc_ref, kn_ref, vp_ref, vc_ref, vn_ref, bias_ref, o_ref, kbuf, vbuf):
    del prev_ref, next_ref
    g = pl.program_id(0)
    kbuf[0:MIX_BLOCK] = kp_ref[...]
    kbuf[MIX_BLOCK:2 * MIX_BLOCK] = kc_ref[...]
    kbuf[2 * MIX_BLOCK:3 * MIX_BLOCK] = kn_ref[...]
    vbuf[0:MIX_BLOCK] = vp_ref[...]
    vbuf[MIX_BLOCK:2 * MIX_BLOCK] = vc_ref[...]
    vbuf[2 * MIX_BLOCK:3 * MIX_BLOCK] = vn_ref[...]
    scale = HEAD_DIM ** -0.5
    half = NA_WIN_ROWS // 2
    m0 = lax.broadcasted_iota(jnp.int32, (GRID_W, PAIR_W), 1) < HEAD_DIM
    for a0 in range(0, NA_ROWS, NA_ROW_GROUP):
        q, k, v, bias = [], [], [], []
        for a in range(a0, a0 + NA_ROW_GROUP):
            edge = (first_ref[g] if a < half else last_ref[g]) == 1
            start = pl.multiple_of(jnp.where(edge, NA_ROWS, a + half) * GRID_W, GRID_W)
            variant = jnp.where(edge, a, half)
            for h in range(NA_HEADS):
                lanes = slice((h // 2) * PAIR_W, (h // 2 + 1) * PAIR_W)
                q2 = q_ref[a * GRID_W:(a + 1) * GRID_W, lanes]
                q.append(jnp.where(m0 if h % 2 == 0 else jnp.logical_not(m0), q2, jnp.zeros_like(q2)))
                k.append(kbuf[pl.ds(start, NA_KEYS), lanes])
                v.append(vbuf[pl.ds(start, NA_KEYS), lanes])
                bias.append(bias_ref[h, variant])
        s = _each(lambda q_, k_, b_: _mm_nt(q_, k_) * scale + b_, q, k, bias)
        p = _each(lambda s_: jnp.exp(s_ - jnp.max(s_, axis=-1, keepdims=True)), s)
        l = _each(lambda p_: jnp.sum(p_, axis=-1, keepdims=True), p)
        o = _each(lambda p_, v_, l_: _mm(p_, v_) / l_, p, v, l)
        for j in range(0, len(o), 2):
            a = a0 + j // NA_HEADS
            lanes = slice(((j % NA_HEADS) // 2) * PAIR_W, ((j % NA_HEADS) // 2 + 1) * PAIR_W)
            o_ref[a * GRID_W:(a + 1) * GRID_W, lanes] = jnp.where(m0, o[j], o[j + 1]).astype(o_ref.dtype)


def _na(na_qkv, bias, tables):
    t = na_qkv.shape[0]
    nb = t // MIX_BLOCK
    blk = (MIX_BLOCK, NA_W)
    cur = lambda col: pl.BlockSpec(blk, lambda g, f, l, p, n: (g, col))
    prv = lambda col: pl.BlockSpec(blk, lambda g, f, l, p, n: (p[g], col))
    nxt = lambda col: pl.BlockSpec(blk, lambda g, f, l, p, n: (n[g], col))
    grid_spec = pltpu.PrefetchScalarGridSpec(
        num_scalar_prefetch=4,
        grid=(nb,),
        in_specs=[cur(0), prv(1), cur(1), nxt(1), prv(2), cur(2), nxt(2),
                  pl.BlockSpec((NA_HEADS, NA_WIN_ROWS, GRID_W, NA_KEYS), lambda g, f, l, p, n: (0, 0, 0, 0))],
        out_specs=pl.BlockSpec(blk, lambda g, f, l, p, n: (g, 0)),
        scratch_shapes=[pltpu.VMEM((3 * MIX_BLOCK, NA_W), BF16), pltpu.VMEM((3 * MIX_BLOCK, NA_W), BF16)],
    )
    return pl.pallas_call(
        _na_kernel,
        grid_spec=grid_spec,
        out_shape=jax.ShapeDtypeStruct((t, NA_W), BF16),
        compiler_params=_params("arbitrary"),
        name="na",
    )(*tables, na_qkv, na_qkv, na_qkv, na_qkv, na_qkv, na_qkv, na_qkv, bias)


STK = 2 * CHUNK
GATE_ROWS = 4 * GATE_GROUP


class _Masks:
    def __init__(self, backward):
        r = lax.broadcasted_iota(jnp.int32, (CHUNK, STK), 0)
        c = lax.broadcasted_iota(jnp.int32, (CHUNK, STK), 1) % CHUNK
        self.incl = (c >= r) if backward else (c <= r)
        self.strict = (c > r) if backward else (c < r)
        self.blk16 = (r // 16) == (c // 16)
        self.eye = jnp.where(r == c, 1.0, 0.0).astype(F32)
        rr = lax.broadcasted_iota(jnp.int32, (STK, STK), 0) // CHUNK
        cc = lax.broadcasted_iota(jnp.int32, (STK, STK), 1) // CHUNK
        self.same = rr == cc
        self.last_row = 0 if backward else CHUNK - 1


def _cumsum_matrix(backward):
    r = lax.broadcasted_iota(jnp.int32, (CHUNK, CHUNK), 0)
    c = lax.broadcasted_iota(jnp.int32, (CHUNK, CHUNK), 1)
    keep = (c >= r) if backward else (c <= r)
    return jnp.where(keep, 1.0, 0.0).astype(BF16)


def _row_cumsum_matrix(backward):
    r = lax.broadcasted_iota(jnp.int32, (STK, STK), 0)
    c = lax.broadcasted_iota(jnp.int32, (STK, STK), 1)
    keep = jnp.logical_and(r // CHUNK == c // CHUNK, (r >= c) if backward else (r <= c))
    return jnp.where(keep, 1.0, 0.0).astype(BF16)


def _lane_half():
    return lax.broadcasted_iota(jnp.int32, (CHUNK, PAIR_W), 1) < HEAD_DIM


def _expand(tile, c0, c1, m0):
    return jnp.where(m0, tile[:, c0:c0 + 1], tile[:, c1:c1 + 1])


def _pair_row(tile, rolled, r0, r1, chunk):
    low = lax.broadcasted_iota(jnp.int32, (1, STK), 1) < CHUNK
    if chunk % 2 == 0:
        return jnp.where(low, tile[r0:r0 + 1, :], rolled[r1:r1 + 1, :])
    return jnp.where(low, rolled[r0:r0 + 1, :], tile[r1:r1 + 1, :])


def _stack(x, m0):
    return jnp.concatenate([jnp.where(m0, x, 0.0), jnp.where(m0, 0.0, x)], axis=0)


def _each(fn, *lists):
    return [fn(*args) for args in zip(*lists)]


def _no_tick():
    pass


def _unit_tri_solve(a_list, mk_list, rhs_list, m0, tick=_no_tick):
    eye = mk_list[0].eye
    pmm = lambda x, y: _mm(x, _stack(y, m0))
    d = _each(lambda a, mk: jnp.where(mk.blk16, a, 0.0), a_list, mk_list)
    rest = _each(lambda a, x: a - x, a_list, d)
    d2 = _each(lambda x: pmm(x, x), d)
    tick()
    d4 = _each(lambda x: pmm(x, x), d2)
    td = _each(lambda x, y: pmm(eye - x, eye + y), d, d2)
    tick()
    d8 = _each(lambda x: pmm(x, x), d4)
    td = _each(lambda x, y: pmm(x, eye + y), td, d4)
    tick()
    td = _each(lambda x, y: pmm(x, eye + y), td, d8)
    tick()
    n = _each(pmm, td, rest)
    tick()
    n2 = _each(lambda x: pmm(x, x), n)
    tick()
    t = _each(lambda x, y: pmm(eye - x, eye + y), n, n2)
    tick()
    t = _each(pmm, t, td)
    tick()
    return _each(_mm, t, rhs_list)


DIRS = (0, 1)
CHAINS = [(d, p) for d in DIRS for p in range(N_PAIRS)]


def _pair_lanes(p, group=0, width=GDN_W):
    return slice(group * width + p * PAIR_W, group * width + (p + 1) * PAIR_W)


def _chunk_rows(c):
    return slice(c * CHUNK, (c + 1) * CHUNK)


def _group_lanes(c):
    return slice((c // 2) * STK, (c // 2 + 1) * STK)


def _interleaved(local_fn, seq_gen):
    if seq_gen is None:
        local_fn(_no_tick)
        return
    local_fn(lambda: next(seq_gen, None))
    for _ in seq_gen:
        pass


def _part_chunks(n_chunks, part):
    m = PART_CHUNKS
    fwd = list(range(part * m, (part + 1) * m))
    bwd = list(range(n_chunks - 1 - part * m, n_chunks - 1 - (part + 1) * m, -1))
    return (fwd, bwd)


def _run_parts(n_chunks, local, seq):
    n_parts = n_chunks // PART_CHUNKS
    _interleaved(local(0), None)
    for k in range(1, n_parts):
        _interleaved(local(k), seq(k - 1))
    for _ in seq(n_parts - 1):
        pass


def _gdn_local(mks, m0, x_refs, g_refs, gt_refs, gp_ref, gpt_ref, cs, rcs, chunk_ids,
               u_ref, wq_ref, kd_ref, p_ref, egl_ref, tick=_no_tick):
    bias_row = gp_ref[0:1, :]
    a_row = -jnp.exp(gp_ref[1:2, :])
    bias_col = gpt_ref[0:GATE_ROWS, 0:1]
    a_col = -jnp.exp(gpt_ref[0:GATE_ROWS, 1:2])
    tiles, rows = {}, {}
    for d in DIRS:
        for j, c in enumerate(chunk_ids[d]):
            gates = g_refs[d][_chunk_rows(c), :]
            log_decay = a_row * _softplus(gates + bias_row)
            tiles[d, j] = (_split_mm_left(cs[d], log_decay, 3), _sigmoid(gates))
            if (d, c // 2) not in rows:
                gt = gt_refs[d][0:GATE_ROWS, _group_lanes(c)]
                gct = _split_mm(a_col * _softplus(gt + bias_col), rcs[d], 3)
                rows[d, c // 2] = (gct, pltpu.roll(gct, CHUNK, 1))
    tick()
    order = [(d, j, p) for j in range(len(chunk_ids[0])) for d in DIRS for p in range(N_PAIRS)]
    ch = [(d, chunk_ids[d][j], p) for d, j, p in order]
    mk = [mks[d] for d, j, p in order]
    col = [d * GDN_HEADS + 2 * p for d, j, p in order]
    gc = [tiles[d, j][0] for d, j, p in order]
    beta = [tiles[d, j][1] for d, j, p in order]
    gc_row = [_pair_row(*rows[d, c // 2], c0, c0 + 1, c) for (d, c, p), c0 in zip(ch, col)]
    q2 = [x_refs[d][_chunk_rows(c), _pair_lanes(p, 0)] for d, c, p in ch]
    k2 = [x_refs[d][_chunk_rows(c), _pair_lanes(p, 1)] for d, c, p in ch]
    v2 = [x_refs[d][_chunk_rows(c), _pair_lanes(p, 2)] for d, c, p in ch]

    gc2 = _each(lambda t, c0: _expand(t, c0, c0 + 1, m0), gc, col)
    b2 = _each(lambda t, c0: _expand(t, GATE_GROUP + c0, GATE_GROUP + c0 + 1, m0), beta, col)
    gl = _each(lambda g, m: g[m.last_row:m.last_row + 1, :], gc2, mk)
    eg2 = _each(jnp.exp, gc2)
    kb2 = _each(lambda k, b: k * b, k2, b2)
    vb2 = _each(lambda v, b: v * b, v2, b2)
    kbg2 = _each(lambda k, e: k * e, kb2, eg2)
    qg2 = _each(lambda q, e: q * e, q2, eg2)
    kdec2 = _each(lambda k, l, g: k * jnp.exp(l - g), k2, gl, gc2)
    decay = _each(lambda g, r, m: jnp.exp(jnp.where(m.incl, g - r, NEG)), gc2, gc_row, mk)
    aq = _each(lambda kb, q, k: _mm_nt(jnp.concatenate([kb, q], axis=0), _stack(k, m0)), kb2, q2, k2)
    tick()
    a = _each(lambda x, dc, m: jnp.where(m.strict, x[:CHUNK] * dc, 0.0), aq, decay, mk)
    pm = _each(lambda x, dc: x[CHUNK:] * dc, aq, decay)
    rhs = _each(lambda vb, kbg: jnp.concatenate([_stack(vb, m0), _stack(kbg, m0)], axis=1), vb2, kbg2)
    uw = _unit_tri_solve(a, mk, rhs, m0, tick)
    tick()
    for i, (d, c, p) in enumerate(ch):
        lanes = _pair_lanes(p)
        u_ref[d, c, :, lanes] = uw[i][:, :PAIR_W]
        wq_ref[d, c, 0:CHUNK, lanes] = _bf(uw[i][:, PAIR_W:])
        wq_ref[d, c, CHUNK:STK, lanes] = _bf(qg2[i])
        kd_ref[d, c, :, lanes] = _bf(kdec2[i])
        p_ref[d, c, p] = _bf(pm[i])
        egl_ref[d, c, :, lanes] = jnp.exp(gl[i])


def _gdn_seq(mks, m0, chunk_ids, o_refs, s_ref, u_ref, wq_ref, kd_ref, p_ref, egl_ref):
    for j in range(len(chunk_ids[0])):
        cidx = [chunk_ids[d][j] for d in DIRS]
        s = [s_ref[d, p] for d, p in CHAINS]
        wq = [_mm(wq_ref[d, cidx[d], :, _pair_lanes(p)], s_) for (d, p), s_ in zip(CHAINS, s)]
        yield
        vn = [u_ref[d, cidx[d], :, _pair_lanes(p)] - x[:CHUNK] for (d, p), x in zip(CHAINS, wq)]
        pv = [_mm(p_ref[d, cidx[d], p], _stack(v, m0)) for (d, p), v in zip(CHAINS, vn)]
        kv = [_mm_tn(kd_ref[d, cidx[d], :, _pair_lanes(p)], v) for (d, p), v in zip(CHAINS, vn)]
        yield
        for i, (d, p) in enumerate(CHAINS):
            o_refs[d][_chunk_rows(cidx[d]), _pair_lanes(p)] = (wq[i][CHUNK:] + pv[i]).astype(o_refs[d].dtype)
            egl = egl_ref[d, cidx[d], :, _pair_lanes(p)]
            s_ref[d, p] = egl * s[i] + jnp.where(mks[d].same, kv[i], 0.0)


def _gdn_kernel(first_ref, last_ref, xf_ref, xb_ref, gf_ref, gb_ref, gtf_ref, gtb_ref, gp_ref, gpt_ref,
                of_ref, ob_ref, s_ref, u_ref, wq_ref, kd_ref, p_ref, egl_ref):
    i = pl.program_id(0)
    nb = pl.num_programs(0)
    n_chunks = xf_ref.shape[0] // CHUNK

    @pl.when(first_ref[i] == 1)
    def _():
        s_ref[0] = jnp.zeros_like(s_ref[0])

    @pl.when(last_ref[nb - 1 - i] == 1)
    def _():
        s_ref[1] = jnp.zeros_like(s_ref[1])

    m0 = _lane_half()
    mks = [_Masks(backward=False), _Masks(backward=True)]
    cs = [_cumsum_matrix(backward=False), _cumsum_matrix(backward=True)]
    rcs = [_row_cumsum_matrix(backward=False), _row_cumsum_matrix(backward=True)]
    scratch = (u_ref, wq_ref, kd_ref, p_ref, egl_ref)

    def local(k):
        return lambda tick: _gdn_local(mks, m0, (xf_ref, xb_ref), (gf_ref, gb_ref), (gtf_ref, gtb_ref), gp_ref,
                                       gpt_ref, cs, rcs, _part_chunks(n_chunks, k), *scratch, tick=tick)

    def seq(k):
        return _gdn_seq(mks, m0, _part_chunks(n_chunks, k), (of_ref, ob_ref), s_ref, *scratch)

    _run_parts(n_chunks, local, seq)


def _rec_specs(nb, widths):
    fwd = lambda w: pl.BlockSpec((REC_BLOCK, w), lambda i, f, l: (i, 0))
    bwd = lambda w: pl.BlockSpec((REC_BLOCK, w), lambda i, f, l: (nb - 1 - i, 0))
    const = lambda r: pl.BlockSpec((r, GATE_W), lambda i, f, l: (0, 0))
    ins = [fwd(widths[0]), bwd(widths[0]), fwd(GATE_W), bwd(GATE_W),
           pl.BlockSpec((GATE_W, REC_BLOCK), lambda i, f, l: (0, i)),
           pl.BlockSpec((GATE_W, REC_BLOCK), lambda i, f, l: (0, nb - 1 - i)),
           const(8), const(GATE_W)]
    return ins, [fwd(widths[1]), bwd(widths[1])]


def _gdn(gqkv, gates, gates_t, gate_params, tables):
    t = gqkv.shape[0]
    nb = t // REC_BLOCK
    nc = REC_BLOCK // CHUNK
    first, last = tables[0], tables[1]
    in_specs, out_specs = _rec_specs(nb, (3 * GDN_W, GDN_W))
    grid_spec = pltpu.PrefetchScalarGridSpec(
        num_scalar_prefetch=2,
        grid=(nb,),
        in_specs=in_specs,
        out_specs=out_specs,
        scratch_shapes=[pltpu.VMEM((2, N_PAIRS, STK, STK), F32),
                        pltpu.VMEM((2, nc, CHUNK, GDN_W), F32),
                        pltpu.VMEM((2, nc, STK, GDN_W), BF16),
                        pltpu.VMEM((2, nc, CHUNK, GDN_W), BF16),
                        pltpu.VMEM((2, nc, N_PAIRS, CHUNK, STK), BF16),
                        pltpu.VMEM((2, nc, 1, GDN_W), F32)],
    )
    return pl.pallas_call(
        _gdn_kernel,
        grid_spec=grid_spec,
        out_shape=[jax.ShapeDtypeStruct((t, GDN_W), BF16)] * 2,
        compiler_params=_params("arbitrary"),
        name="gdn",
    )(first, last, gqkv, gqkv, gates, gates, gates_t, gates_t, gate_params[0], gate_params[1])


def _mlstm_local(mks, m0, bd2, x_refs, g_refs, gt_refs, gp_ref, gpt_ref, cs, rcs, chunk_ids,
                 upd_ref, intra_ref, mi_ref, fc_ref, fl_ref, ml_ref, tick=_no_tick):
    bias_row = gp_ref[0:1, :]
    bias_col = gpt_ref[0:GATE_ROWS, 0:1]
    tiles, rows = {}, {}
    for d in DIRS:
        for j, c in enumerate(chunk_ids[d]):
            li = g_refs[d][_chunk_rows(c), :] + bias_row
            tiles[d, j] = (_split_mm_left(cs[d], -_softplus(-li), 3), li)
            if (d, c // 2) not in rows:
                lit = gt_refs[d][0:GATE_ROWS, _group_lanes(c)] + bias_col
                fct = _split_mm(-_softplus(-lit), rcs[d], 3)
                rows[d, c // 2] = (fct, pltpu.roll(fct, CHUNK, 1), lit, pltpu.roll(lit, CHUNK, 1))
    tick()
    order = [(d, j, p) for j in range(len(chunk_ids[0])) for d in DIRS for p in range(N_PAIRS)]
    ch = [(d, chunk_ids[d][j], p) for d, j, p in order]
    mk = [mks[d] for d, j, p in order]
    col_i = [2 * GATE_GROUP + d * MLSTM_HEADS + 2 * p for d, j, p in order]
    col_f = [3 * GATE_GROUP + d * MLSTM_HEADS + 2 * p for d, j, p in order]
    fc = [tiles[d, j][0] for d, j, p in order]
    li = [tiles[d, j][1] for d, j, p in order]
    e_row = [_pair_row(rows[d, c // 2][0], rows[d, c // 2][1], cf, cf + 1, c)
             - _pair_row(rows[d, c // 2][2], rows[d, c // 2][3], ci, ci + 1, c)
             for (d, c, p), cf, ci in zip(ch, col_f, col_i)]
    q2 = [x_refs[d][_chunk_rows(c), _pair_lanes(p, 0, MLSTM_W)] for d, c, p in ch]
    k2 = [x_refs[d][_chunk_rows(c), _pair_lanes(p, 1, MLSTM_W)] * (HEAD_DIM ** -0.5) for d, c, p in ch]
    v2 = [x_refs[d][_chunk_rows(c), _pair_lanes(p, 2, MLSTM_W)] for d, c, p in ch]
    ones = jnp.ones((CHUNK, PAIR_W), F32)
    ones_st = _stack(ones, m0)

    fc2 = _each(lambda t, c0: _expand(t, c0, c0 + 1, m0), fc, col_f)
    li2 = _each(lambda t, c0: _expand(t, c0, c0 + 1, m0), li, col_i)
    fl = _each(lambda f, m: f[m.last_row:m.last_row + 1, :], fc2, mk)
    g_end = _each(lambda l, f, i_: l - f + i_, fl, fc2, li2)
    ml = _each(lambda g: jnp.max(g, axis=0, keepdims=True), g_end)
    kw2 = _each(lambda k, g, m: k * jnp.exp(g - m), k2, g_end, ml)
    upd = _each(lambda kw, v: jnp.where(bd2, _mm_tn(kw, jnp.concatenate([v, ones], axis=1)), 0.0), kw2, v2)
    tick()
    dmat = _each(lambda f, e, m: jnp.where(m.incl, f - e, NEG), fc2, e_row, mk)
    mi2 = _each(lambda x: jnp.where(m0, jnp.max(jnp.where(m0, x, NEG), axis=1, keepdims=True),
                                    jnp.max(jnp.where(m0, NEG, x), axis=1, keepdims=True)), dmat)
    tick()
    qk = _each(lambda q, k: _mm_nt(q, _stack(k, m0)), q2, k2)
    tick()
    s_loc = _each(lambda x, dm, mi: x * jnp.exp(dm - mi), qk, dmat, mi2)
    tick()
    intra = _each(lambda s, v: _mm(s, jnp.concatenate([_stack(v, m0), ones_st], axis=1)), s_loc, v2)
    tick()
    for i, (d, c, p) in enumerate(ch):
        lanes = _pair_lanes(p)
        upd_ref[d, c, p] = upd[i]
        intra_ref[d, c, p] = intra[i]
        mi_ref[d, c, :, lanes] = mi2[i]
        fc_ref[d, c, :, lanes] = fc2[i]
        fl_ref[d, c, :, lanes] = fl[i]
        ml_ref[d, c, :, lanes] = ml[i]
        if i % N_PAIRS == N_PAIRS - 1:
            tick()


def _mlstm_seq(chunk_ids, x_refs, o_refs, c_ref, m_ref, upd_ref, intra_ref, mi_ref, fc_ref, fl_ref, ml_ref):
    for j, d, p in [(j, d, p) for j in range(len(chunk_ids[0])) for d, p in CHAINS]:
        c = chunk_ids[d][j]
        lanes = _pair_lanes(p)
        m_st = m_ref[d, p]
        cn = c_ref[d, p]
        fl = fl_ref[d, c, :, lanes]
        ml = ml_ref[d, c, :, lanes]
        m_new = jnp.maximum(fl + m_st, ml)
        a_row = jnp.exp(fl + m_st - m_new)
        b_row = jnp.exp(ml - m_new)
        m_inter = fc_ref[d, c, :, lanes] + m_st
        mi2 = mi_ref[d, c, :, lanes]
        m_t = jnp.maximum(m_inter, mi2)
        w_intra = jnp.exp(mi2 - m_t)
        w_inter = jnp.exp(m_inter - m_t)
        inter = _mm(x_refs[d][_chunk_rows(c), _pair_lanes(p, 0, MLSTM_W)], cn)
        intra = intra_ref[d, c, p]
        num = w_intra * intra[:, :PAIR_W] + w_inter * inter[:, :PAIR_W]
        den = w_intra * intra[:, PAIR_W:] + w_inter * inter[:, PAIR_W:]
        out = num / jnp.maximum(jnp.abs(den), jnp.exp(-m_t))
        o_refs[d][_chunk_rows(c), lanes] = out.astype(o_refs[d].dtype)
        c_ref[d, p] = (jnp.concatenate([a_row, a_row], axis=1) * cn
                       + jnp.concatenate([b_row, b_row], axis=1) * upd_ref[d, c, p])
        m_ref[d, p] = m_new
        if p == N_PAIRS - 1:
            yield


def _mlstm_kernel(first_ref, last_ref, xf_ref, xb_ref, gf_ref, gb_ref, gtf_ref, gtb_ref, gp_ref, gpt_ref,
                  of_ref, ob_ref, c_ref, m_ref, upd_ref, intra_ref, mi_ref, fc_ref, fl_ref, ml_ref):
    i = pl.program_id(0)
    nb = pl.num_programs(0)
    n_chunks = xf_ref.shape[0] // CHUNK

    @pl.when(first_ref[i] == 1)
    def _():
        c_ref[0] = jnp.zeros_like(c_ref[0])
        m_ref[0] = jnp.full_like(m_ref[0], M_INIT)

    @pl.when(last_ref[nb - 1 - i] == 1)
    def _():
        c_ref[1] = jnp.zeros_like(c_ref[1])
        m_ref[1] = jnp.full_like(m_ref[1], M_INIT)

    m0 = _lane_half()
    r = lax.broadcasted_iota(jnp.int32, (STK, 2 * STK), 0) // HEAD_DIM
    c = (lax.broadcasted_iota(jnp.int32, (STK, 2 * STK), 1) % STK) // HEAD_DIM
    bd2 = r == c
    mks = [_Masks(backward=False), _Masks(backward=True)]
    cs = [_cumsum_matrix(backward=False), _cumsum_matrix(backward=True)]
    rcs = [_row_cumsum_matrix(backward=False), _row_cumsum_matrix(backward=True)]
    scratch = (upd_ref, intra_ref, mi_ref, fc_ref, fl_ref, ml_ref)

    def local(k):
        return lambda tick: _mlstm_local(mks, m0, bd2, (xf_ref, xb_ref), (gf_ref, gb_ref), (gtf_ref, gtb_ref),
                                         gp_ref, gpt_ref, cs, rcs, _part_chunks(n_chunks, k), *scratch, tick=tick)

    def seq(k):
        return _mlstm_seq(_part_chunks(n_chunks, k), (xf_ref, xb_ref), (of_ref, ob_ref), c_ref, m_ref, *scratch)

    _run_parts(n_chunks, local, seq)


def _mlstm(ml_qkv, gates, gates_t, gate_params, tables):
    t = ml_qkv.shape[0]
    nb = t // REC_BLOCK
    nc = REC_BLOCK // CHUNK
    first, last = tables[0], tables[1]
    in_specs, out_specs = _rec_specs(nb, (3 * MLSTM_W, MLSTM_W))
    grid_spec = pltpu.PrefetchScalarGridSpec(
        num_scalar_prefetch=2,
        grid=(nb,),
        in_specs=in_specs,
        out_specs=out_specs,
        scratch_shapes=[pltpu.VMEM((2, N_PAIRS, STK, 2 * STK), F32),
                        pltpu.VMEM((2, N_PAIRS, 1, PAIR_W), F32),
                        pltpu.VMEM((2, nc, N_PAIRS, STK, 2 * STK), F32),
                        pltpu.VMEM((2, nc, N_PAIRS, CHUNK, 2 * PAIR_W), F32),
                        pltpu.VMEM((2, nc, CHUNK, MLSTM_W), F32),
                        pltpu.VMEM((2, nc, CHUNK, MLSTM_W), F32),
                        pltpu.VMEM((2, nc, 1, MLSTM_W), F32),
                        pltpu.VMEM((2, nc, 1, MLSTM_W), F32)],
    )
    return pl.pallas_call(
        _mlstm_kernel,
        grid_spec=grid_spec,
        out_shape=[jax.ShapeDtypeStruct((t, MLSTM_W), BF16)] * 2,
        compiler_params=_params("arbitrary"),
        name="mlstm",
    )(first, last, ml_qkv, ml_qkv, gates, gates, gates_t, gates_t, gate_params[0], gate_params[1])


OUT_TM = 512


def _outproj_kernel(x_ref, na_ref, gf_ref, gb_ref, z_ref, hf_ref, hb_ref, o_ref, gn_ref, mn_ref,
                    w_ref, y_ref):
    def head_rms(v, gain):
        ms = _head_sumsq(v) * (1.0 / HEAD_DIM)
        return v * lax.rsqrt(ms + EPS) * gain

    go = head_rms(gf_ref[...].astype(F32) + gb_ref[...].astype(F32), gn_ref[...])
    z = z_ref[...].astype(F32)
    gdn_out = go * (z * _sigmoid(z))
    mh = head_rms(hf_ref[...].astype(F32) + hb_ref[...].astype(F32), mn_ref[...])
    ml_out = _sigmoid(o_ref[...].astype(F32)) * mh
    y = x_ref[...]
    y = y + jnp.dot(na_ref[...], w_ref[0:NA_W, :], preferred_element_type=F32)
    y = y + jnp.dot(_bf(gdn_out), w_ref[NA_W:NA_W + GDN_W, :], preferred_element_type=F32)
    y = y + jnp.dot(_bf(ml_out), w_ref[NA_W + GDN_W:, :], preferred_element_type=F32)
    y_ref[...] = y


def _outproj(x, na_out, gf, gb, z, hf, hb, o, gdn_norm, mlstm_norm, w_out):
    t = x.shape[0]
    tm = min(OUT_TM, t)
    tok = lambda w: pl.BlockSpec((tm, w), lambda i: (i, 0))
    const = lambda r, w: pl.BlockSpec((r, w), lambda i: (0, 0))
    gn = jnp.tile(gdn_norm.astype(F32), GDN_HEADS).reshape(1, GDN_W)
    mn = mlstm_norm.astype(F32).reshape(1, MLSTM_W)
    return pl.pallas_call(
        _outproj_kernel,
        grid=(t // tm,),
        in_specs=[tok(D_MODEL), tok(NA_W), tok(GDN_W), tok(GDN_W), tok(GDN_W), tok(MLSTM_W), tok(MLSTM_W),
                  tok(MLSTM_W), const(1, GDN_W), const(1, MLSTM_W), const(D_MODEL, D_MODEL)],
        out_specs=tok(D_MODEL),
        out_shape=jax.ShapeDtypeStruct((t, D_MODEL), F32),
        compiler_params=_params("parallel"),
        name="outproj",
    )(x, na_out, gf, gb, z, hf, hb, o, gn, mn, _bf(w_out))


def _gate_params(bias_rows, a_log):
    pad = GATE_W - 4 * GATE_GROUP
    row0 = jnp.concatenate([b.astype(F32).reshape(-1) for b in bias_rows] + [jnp.zeros((pad,), F32)])
    row1 = jnp.concatenate([a_log.astype(F32).reshape(-1), jnp.zeros((GATE_W - GATE_GROUP,), F32)])
    rows = jnp.concatenate([row0[None], row1[None], jnp.zeros((6, GATE_W), F32)], axis=0)
    cols = jnp.concatenate([rows.T, jnp.zeros((GATE_W, GATE_W - 8), F32)], axis=1)
    return rows, cols


def _trunk(xs, seq_lens, p):
    depth = p["ffn1_norm"].shape[0]
    tables = tuple(jnp.asarray(a) for a in _block_tables(seq_lens, MIX_BLOCK))
    rec_tables = tuple(jnp.asarray(a) for a in _block_tables(seq_lens, REC_BLOCK))
    zeros_g = jnp.zeros((2, GDN_HEADS), F32)
    out_sizes = [x.shape[0] for x in xs]
    x = None
    for i in range(depth):
        x = _ffn(xs if i == 0 else [x], p["ffn1_norm"][i], _bf(p["ffn1_w_in"][i]), _bf(p["ffn1_w_out"][i]),
                 p["final_norm"], False)
        na_qkv, gqkv, gdn_z, ml_qkv, ml_o, gates = _inproj(x, p["mix_norm"][i], _permute_w_in(p["w_in"][i]),
                                                           p["gdn_conv"][i], tables)
        na_out = _na(na_qkv, _na_bias_table(p["na_rpb"][i]), tables)
        gdn_gp = _gate_params([p["gdn_dt_bias"][i], zeros_g, zeros_g, zeros_g], p["gdn_a_log"][i])
        gates_t = gates.T
        gf, gb = _gdn(gqkv, gates, gates_t, gdn_gp, rec_tables)
        ml_gp = _gate_params([zeros_g, zeros_g, p["mlstm_i_bias"][i], p["mlstm_f_bias"][i]], zeros_g)
        hf, hb = _mlstm(ml_qkv, gates, gates_t, ml_gp, rec_tables)
        x = _outproj(x, na_out, gf, gb, gdn_z, hf, hb, ml_o, p["gdn_norm"][i], p["mlstm_norm"][i], p["w_out"][i])
        last = i == depth - 1
        x = _ffn([x], p["ffn2_norm"][i], _bf(p["ffn2_w_in"][i]), _bf(p["ffn2_w_out"][i]), p["final_norm"],
                 last, out_sizes if last else None)
    return x if len(out_sizes) > 1 else [x]


def kernel(x_prompt, x_sample, ffn1_norm, ffn1_w_in, ffn1_w_out, mix_norm, w_in, w_out, na_rpb, gdn_conv,
           gdn_a_log, gdn_dt_bias, gdn_norm, mlstm_i_bias, mlstm_f_bias, mlstm_norm, ffn2_norm, ffn2_w_in,
           ffn2_w_out, final_norm):
    p = dict(ffn1_norm=ffn1_norm, ffn1_w_in=ffn1_w_in, ffn1_w_out=ffn1_w_out, mix_norm=mix_norm, w_in=w_in,
             w_out=w_out, na_rpb=na_rpb, gdn_conv=gdn_conv, gdn_a_log=gdn_a_log, gdn_dt_bias=gdn_dt_bias,
             gdn_norm=gdn_norm, mlstm_i_bias=mlstm_i_bias, mlstm_f_bias=mlstm_f_bias, mlstm_norm=mlstm_norm,
             ffn2_norm=ffn2_norm, ffn2_w_in=ffn2_w_in, ffn2_w_out=ffn2_w_out, final_norm=final_norm)
    bp, lp, _ = x_prompt.shape
    bs, ls, _ = x_sample.shape
    y_prompt, y_sample = _trunk([x_prompt.reshape(bp * lp, D_MODEL), x_sample.reshape(bs * ls, D_MODEL)],
                                (lp,) * bp + (ls,) * bs, p)
    return (y_prompt.reshape(bp, lp, D_MODEL), y_sample.reshape(bs, ls, D_MODEL))
```

```python
import functools

import jax
import jax.numpy as jnp
import numpy as np
from jax import lax
from jax.experimental import pallas as pl
from jax.experimental.pallas import tpu as pltpu

D_MODEL = 1024
D_FF = 2816
HEAD_DIM = 64
NA_HEADS = 4
GDN_HEADS = 6
MLSTM_HEADS = 6
NA_W = NA_HEADS * HEAD_DIM
GDN_W = GDN_HEADS * HEAD_DIM
MLSTM_W = MLSTM_HEADS * HEAD_DIM
GRID_W = 64
NA_WIN_ROWS = 8
NA_WIN_COLS = 16
GDN_CONV = 5
CHUNK = 64
M_INIT = -1e30
EPS = 1e-6
NEG = -1e30

LANES = 128
PAIR_W = 2 * HEAD_DIM
N_PAIRS = GDN_HEADS // 2
GATE_W = LANES
GATE_GROUP = 2 * GDN_HEADS
MIX_BLOCK = 512
REC_BLOCK = 1024
GDN_PART_CHUNKS = 4
MLSTM_PART_CHUNKS = 8
VMEM_LIMIT = 56 * 1024 * 1024

F32 = jnp.float32
BF16 = jnp.bfloat16


def _bf(x):
    return x.astype(BF16)


def _mm(a, b):
    return jnp.dot(_bf(a), _bf(b), preferred_element_type=F32)


def _mm_nt(a, b):
    return lax.dot_general(_bf(a), _bf(b), (((1,), (1,)), ((), ())), preferred_element_type=F32)


def _mm_tn(a, b):
    return lax.dot_general(_bf(a), _bf(b), (((0,), (0,)), ((), ())), preferred_element_type=F32)


def _split_mm(a, b_exact, parts):
    acc = None
    rem = a
    for _ in range(parts):
        piece = _bf(rem)
        term = jnp.dot(piece, b_exact, preferred_element_type=F32)
        acc = term if acc is None else acc + term
        rem = rem - piece.astype(F32)
    return acc


def _split_mm_left(a_exact, b, parts):
    acc = None
    rem = b
    for _ in range(parts):
        piece = _bf(rem)
        term = jnp.dot(a_exact, piece, preferred_element_type=F32)
        acc = term if acc is None else acc + term
        rem = rem - piece.astype(F32)
    return acc


def _sigmoid(x):
    return 1.0 / (1.0 + jnp.exp(-x))


def _softplus(x):
    return jnp.maximum(x, 0.0) + jnp.log(1.0 + jnp.exp(-jnp.abs(x)))


def _rms(x, gain):
    return x * lax.rsqrt(jnp.mean(x * x, axis=-1, keepdims=True) + EPS) * gain


def _params(*sem):
    return pltpu.CompilerParams(dimension_semantics=sem, vmem_limit_bytes=VMEM_LIMIT)


FFN_TM = 1024
FFN_SUB = 512
FFN_TF = 256


def _ffn_kernel(*refs, final, sub, in_blocks, out_blocks):
    n_in, n_out = len(in_blocks), len(out_blocks)
    x_refs = refs[:n_in]
    g_ref, wi_ref, wo_ref, fn_ref = refs[n_in:n_in + 4]
    o_refs = refs[n_in + 4:n_in + 4 + n_out]
    act_ref = refs[-1]
    i = pl.program_id(0)
    in_starts = np.cumsum((0,) + tuple(in_blocks))
    out_starts = np.cumsum((0,) + tuple(out_blocks))
    for s in range(x_refs[0].shape[0] // sub):
        rows = slice(s * sub, (s + 1) * sub)
        x = x_refs[-1][rows, :]
        for k in range(n_in - 2, -1, -1):
            x = jnp.where(i < in_starts[k + 1], x_refs[k][rows, :], x)
        xn = _bf(_rms(x, g_ref[...]))
        for j in range(D_FF // FFN_TF):
            gate = jnp.dot(xn, wi_ref[:, j * FFN_TF:(j + 1) * FFN_TF], preferred_element_type=F32)
            up = jnp.dot(xn, wi_ref[:, D_FF + j * FFN_TF:D_FF + (j + 1) * FFN_TF], preferred_element_type=F32)
            act_ref[rows, j * FFN_TF:(j + 1) * FFN_TF] = _bf(gate * _sigmoid(gate) * up)
        y = x + 0.5 * jnp.dot(act_ref[rows, :], wo_ref[...], preferred_element_type=F32)
        if final:
            y = _rms(y, fn_ref[...])
        if n_out == 1:
            o_refs[0][rows, :] = y
        else:
            for k in range(n_out):
                @pl.when(jnp.logical_and(i >= out_starts[k], i < out_starts[k + 1]))
                def _(k=k, y=y):
                    o_refs[k][rows, :] = y


def _ffn(xs, gain, w_in, w_out, final_gain, final, out_sizes=None):
    sizes = [x.shape[0] for x in xs]
    t = sum(sizes)
    out_sizes = [t] if out_sizes is None else list(out_sizes)
    tm = min([FFN_TM] + sizes + out_sizes)
    sub = min(FFN_SUB, tm)
    in_blocks = [s // tm for s in sizes]
    out_blocks = [s // tm for s in out_sizes]

    def part_spec(blocks, k):
        start = int(np.sum(blocks[:k]))
        return pl.BlockSpec((tm, D_MODEL), lambda i: (jnp.clip(i - start, 0, blocks[k] - 1), 0))

    resident = lambda shape: pl.BlockSpec(shape, lambda i: (0, 0), pipeline_mode=pl.Buffered(1))
    outs = pl.pallas_call(
        functools.partial(_ffn_kernel, final=final, sub=sub, in_blocks=tuple(in_blocks),
                          out_blocks=tuple(out_blocks)),
        grid=(t // tm,),
        in_specs=[part_spec(in_blocks, k) for k in range(len(xs))] + [
            resident((1, D_MODEL)),
            resident((D_MODEL, 2 * D_FF)),
            resident((D_FF, D_MODEL)),
            resident((1, D_MODEL)),
        ],
        out_specs=[part_spec(out_blocks, k) for k in range(len(out_sizes))],
        out_shape=[jax.ShapeDtypeStruct((s, D_MODEL), F32) for s in out_sizes],
        scratch_shapes=[pltpu.VMEM((tm, D_FF), BF16)],
        compiler_params=_params("arbitrary"),
        name="ffn",
    )(*xs, gain.reshape(1, D_MODEL), w_in, w_out, final_gain.reshape(1, D_MODEL))
    return outs[0] if len(outs) == 1 else outs


PROJ_TM = 512
PROJ_WIDTHS = (3 * NA_W, 3 * GDN_W, GDN_W, 3 * MLSTM_W, MLSTM_W, GATE_W)
PROJ_DTYPES = (BF16, F32, BF16, F32, BF16, F32)
P_PERM = sum(PROJ_WIDTHS)


def _permute_w_in(w_in):
    o = np.cumsum([0, 3 * NA_W, 3 * GDN_W, GDN_W, 2 * GDN_HEADS, 2 * GDN_HEADS,
                   3 * MLSTM_W, MLSTM_W, 2 * MLSTM_HEADS, 2 * MLSTM_HEADS])
    n_gates = 4 * 2 * GDN_HEADS
    cols = [w_in[:, o[0]:o[3]], w_in[:, o[5]:o[7]], w_in[:, o[3]:o[5]], w_in[:, o[7]:o[9]],
            jnp.zeros((D_MODEL, GATE_W - n_gates), w_in.dtype)]
    return _bf(jnp.concatenate(cols, axis=1))


PROJ_DOT_GROUPS = ((1, 2), (0,), (3, 4), (5,))
PROJ_OFFSETS = tuple(int(v) for v in np.cumsum((0,) + PROJ_WIDTHS[:-1]))
PROJ_HALO = 16


def _head_sumsq(x):
    r = lax.broadcasted_iota(jnp.int32, (PAIR_W, PAIR_W), 0) // HEAD_DIM
    c = lax.broadcasted_iota(jnp.int32, (PAIR_W, PAIR_W), 1) // HEAD_DIM
    ones_bd = jnp.where(r == c, 1.0, 0.0).astype(BF16)
    sq = x * x
    parts = [_split_mm(sq[:, g:g + PAIR_W], ones_bd, 2) for g in range(0, x.shape[1], PAIR_W)]
    return jnp.concatenate(parts, axis=1)


def _gdn_preprocess(y, w_ref, tb):
    n = y.shape[0]
    acc = None
    for j in range(GDN_CONV):
        shift = (GDN_CONV // 2 - j) % n
        tap = y if shift == 0 else pltpu.roll(y, shift, 0)
        term = tap[PROJ_HALO:PROJ_HALO + tb] * w_ref[j:j + 1, :]
        acc = term if acc is None else acc + term
    act = acc * _sigmoid(acc)
    q = act[:, 0:GDN_W]
    k = act[:, GDN_W:2 * GDN_W]
    qn = q * lax.rsqrt(_head_sumsq(q) + EPS) * (HEAD_DIM ** -0.5)
    kn = k * lax.rsqrt(_head_sumsq(k) + EPS)
    return qn, kn, act[:, 2 * GDN_W:3 * GDN_W]


def _inproj_kernel(first_ref, last_ref, x_ref, xp_ref, xn_ref, g_ref, w_ref, cw_ref, *o_refs):
    i = pl.program_id(0)
    tb = x_ref.shape[0]
    prev = jnp.where(first_ref[i] == 1, 0.0, xp_ref[...])
    nxt = jnp.where(last_ref[i] == 1, 0.0, xn_ref[...])
    xe = jnp.concatenate([prev, x_ref[...], nxt], axis=0)
    xn = _bf(_rms(xe, g_ref[...]))
    centre = slice(PROJ_HALO, PROJ_HALO + tb)
    gdn_part = None
    for group in PROJ_DOT_GROUPS:
        off = PROJ_OFFSETS[group[0]]
        width = sum(PROJ_WIDTHS[k] for k in group)
        res = jnp.dot(xn, w_ref[:, off:off + width], preferred_element_type=F32)
        col = 0
        for k in group:
            part = res[:, col:col + PROJ_WIDTHS[k]]
            col += PROJ_WIDTHS[k]
            if k == 1:
                gdn_part = part
            else:
                o_refs[k][...] = part[centre].astype(o_refs[k].dtype)
    qn, kn, v = _gdn_preprocess(gdn_part, cw_ref, tb)
    o_refs[1][:, 0:GDN_W] = qn
    o_refs[1][:, GDN_W:2 * GDN_W] = kn
    o_refs[1][:, 2 * GDN_W:3 * GDN_W] = v


def _inproj(x, gain, w_perm, conv_w, tables):
    t = x.shape[0]
    tm = MIX_BLOCK
    per = tm // PROJ_HALO
    n_halo = t // PROJ_HALO
    first, last = tables[0], tables[1]
    const = lambda shape: pl.BlockSpec(shape, lambda i, f, l: (0, 0))
    grid_spec = pltpu.PrefetchScalarGridSpec(
        num_scalar_prefetch=2,
        grid=(t // tm,),
        in_specs=[
            pl.BlockSpec((tm, D_MODEL), lambda i, f, l: (i, 0)),
            pl.BlockSpec((PROJ_HALO, D_MODEL), lambda i, f, l: (jnp.maximum(i * per - 1, 0), 0)),
            pl.BlockSpec((PROJ_HALO, D_MODEL), lambda i, f, l: (jnp.minimum((i + 1) * per, n_halo - 1), 0)),
            const((1, D_MODEL)),
            const((D_MODEL, P_PERM)),
            const((8, 3 * GDN_W)),
        ],
        out_specs=[pl.BlockSpec((tm, w), lambda i, f, l: (i, 0)) for w in PROJ_WIDTHS],
    )
    w8 = jnp.concatenate([conv_w.astype(F32), jnp.zeros((8 - GDN_CONV, 3 * GDN_W), F32)], axis=0)
    return pl.pallas_call(
        _inproj_kernel,
        grid_spec=grid_spec,
        out_shape=[jax.ShapeDtypeStruct((t, w), d) for w, d in zip(PROJ_WIDTHS, PROJ_DTYPES)],
        compiler_params=_params("arbitrary"),
        name="inproj",
    )(first, last, x, x, x, gain.reshape(1, D_MODEL), w_perm, w8)


def _block_tables(seq_lens, block):
    first, last = [], []
    for n in seq_lens:
        nb = n // block
        first += [1] + [0] * (nb - 1)
        last += [0] * (nb - 1) + [1]
    first = np.asarray(first, np.int32)
    last = np.asarray(last, np.int32)
    idx = np.arange(len(first), dtype=np.int32)
    prev = np.where(first == 1, idx, idx - 1).astype(np.int32)
    nxt = np.where(last == 1, idx, idx + 1).astype(np.int32)
    return first, last, prev, nxt


NA_ROWS = MIX_BLOCK // GRID_W
NA_KEYS = NA_WIN_ROWS * GRID_W
NA_ROW_GROUP = 4


def _na_bias_table(rpb):
    c = np.arange(GRID_W)
    c_start = np.clip(c - NA_WIN_COLS // 2, 0, GRID_W - NA_WIN_COLS)
    in_win = (c[None, :] >= c_start[:, None]) & (c[None, :] < c_start[:, None] + NA_WIN_COLS)
    dc = np.clip(c[None, :] - c[:, None], 1 - NA_WIN_COLS, NA_WIN_COLS - 1) + NA_WIN_COLS - 1
    one_hot = jnp.asarray(dc[None] == np.arange(2 * NA_WIN_COLS - 1)[:, None, None], F32)
    by_col = jnp.einsum("hrc,cqk->hrqk", rpb.astype(F32), one_hot, precision=lax.Precision.HIGHEST)
    tab = jnp.stack([by_col[:, NA_WIN_ROWS - 1 - v:2 * NA_WIN_ROWS - 1 - v] for v in range(NA_WIN_ROWS)],
                    axis=1)
    tab = jnp.where(in_win[None, None, None], tab, NEG).transpose(0, 1, 3, 2, 4)
    return tab.reshape(NA_HEADS, NA_WIN_ROWS, GRID_W, NA_KEYS)


def _na_kernel(first_ref, last_ref, prev_ref, next_ref,
               q_ref, kp_ref, kc_ref, kn_ref, vp_ref, vc_ref, vn_ref, bias_ref, o_ref, kbuf, vbuf):
    del prev_ref, next_ref
    g = pl.program_id(0)
    kbuf[0:MIX_BLOCK] = kp_ref[...]
    kbuf[MIX_BLOCK:2 * MIX_BLOCK] = kc_ref[...]
    kbuf[2 * MIX_BLOCK:3 * MIX_BLOCK] = kn_ref[...]
    vbuf[0:MIX_BLOCK] = vp_ref[...]
    vbuf[MIX_BLOCK:2 * MIX_BLOCK] = vc_ref[...]
    vbuf[2 * MIX_BLOCK:3 * MIX_BLOCK] = vn_ref[...]
    scale = jnp.asarray(HEAD_DIM ** -0.5, BF16)
    half = NA_WIN_ROWS // 2
    m0 = lax.broadcasted_iota(jnp.int32, (GRID_W, PAIR_W), 1) < HEAD_DIM
    for a0 in range(0, NA_ROWS, NA_ROW_GROUP):
        q, k, v, bias = [], [], [], []
        for a in range(a0, a0 + NA_ROW_GROUP):
            edge = (first_ref[g] if a < half else last_ref[g]) == 1
            start = pl.multiple_of(jnp.where(edge, NA_ROWS, a + half) * GRID_W, GRID_W)
            variant = jnp.where(edge, a, half)
            for h in range(NA_HEADS):
                lanes = slice((h // 2) * PAIR_W, (h // 2 + 1) * PAIR_W)
                q2 = q_ref[a * GRID_W:(a + 1) * GRID_W, lanes] * scale
                q.append(jnp.where(m0 if h % 2 == 0 else jnp.logical_not(m0), q2, jnp.zeros_like(q2)))
                k.append(kbuf[pl.ds(start, NA_KEYS), lanes])
                v.append(vbuf[pl.ds(start, NA_KEYS), lanes])
                bias.append(bias_ref[h, variant])
        s = _each(lambda q_, k_, b_: _mm_nt(q_, k_) + b_, q, k, bias)
        p = _each(lambda s_: jnp.exp(s_ - jnp.max(s_, axis=-1, keepdims=True)), s)
        l = _each(lambda p_: jnp.sum(p_, axis=-1, keepdims=True), p)
        o = _each(lambda p_, v_, l_: _mm(p_, v_) / l_, p, v, l)
        for j in range(0, len(o), 2):
            a = a0 + j // NA_HEADS
            lanes = slice(((j % NA_HEADS) // 2) * PAIR_W, ((j % NA_HEADS) // 2 + 1) * PAIR_W)
            o_ref[a * GRID_W:(a + 1) * GRID_W, lanes] = jnp.where(m0, o[j], o[j + 1]).astype(o_ref.dtype)


def _na(na_qkv, bias, tables):
    t = na_qkv.shape[0]
    nb = t // MIX_BLOCK
    blk = (MIX_BLOCK, NA_W)
    cur = lambda col: pl.BlockSpec(blk, lambda g, f, l, p, n: (g, col))
    prv = lambda col: pl.BlockSpec(blk, lambda g, f, l, p, n: (p[g], col))
    nxt = lambda col: pl.BlockSpec(blk, lambda g, f, l, p, n: (n[g], col))
    grid_spec = pltpu.PrefetchScalarGridSpec(
        num_scalar_prefetch=4,
        grid=(nb,),
        in_specs=[cur(0), prv(1), cur(1), nxt(1), prv(2), cur(2), nxt(2),
                  pl.BlockSpec((NA_HEADS, NA_WIN_ROWS, GRID_W, NA_KEYS), lambda g, f, l, p, n: (0, 0, 0, 0))],
        out_specs=pl.BlockSpec(blk, lambda g, f, l, p, n: (g, 0)),
        scratch_shapes=[pltpu.VMEM((3 * MIX_BLOCK, NA_W), BF16), pltpu.VMEM((3 * MIX_BLOCK, NA_W), BF16)],
    )
    return pl.pallas_call(
        _na_kernel,
        grid_spec=grid_spec,
        out_shape=jax.ShapeDtypeStruct((t, NA_W), BF16),
        compiler_params=_params("arbitrary"),
        name="na",
    )(*tables, na_qkv, na_qkv, na_qkv, na_qkv, na_qkv, na_qkv, na_qkv, bias)


STK = 2 * CHUNK
GATE_ROWS = 4 * GATE_GROUP


class _Masks:
    def __init__(self, backward):
        r = lax.broadcasted_iota(jnp.int32, (CHUNK, STK), 0)
        c = lax.broadcasted_iota(jnp.int32, (CHUNK, STK), 1) % CHUNK
        self.incl = (c >= r) if backward else (c <= r)
        self.strict = (c > r) if backward else (c < r)
        self.blk16 = (r // 16) == (c // 16)
        self.eye = jnp.where(r == c, 1.0, 0.0).astype(F32)
        rr = lax.broadcasted_iota(jnp.int32, (STK, STK), 0) // CHUNK
        cc = lax.broadcasted_iota(jnp.int32, (STK, STK), 1) // CHUNK
        self.same = rr == cc
        self.last_row = 0 if backward else CHUNK - 1


def _cumsum_matrix(backward):
    r = lax.broadcasted_iota(jnp.int32, (CHUNK, CHUNK), 0)
    c = lax.broadcasted_iota(jnp.int32, (CHUNK, CHUNK), 1)
    keep = (c >= r) if backward else (c <= r)
    return jnp.where(keep, 1.0, 0.0).astype(BF16)


def _row_cumsum_matrix(backward):
    r = lax.broadcasted_iota(jnp.int32, (STK, STK), 0)
    c = lax.broadcasted_iota(jnp.int32, (STK, STK), 1)
    keep = jnp.logical_and(r // CHUNK == c // CHUNK, (r >= c) if backward else (r <= c))
    return jnp.where(keep, 1.0, 0.0).astype(BF16)


def _lane_half():
    return lax.broadcasted_iota(jnp.int32, (CHUNK, PAIR_W), 1) < HEAD_DIM


def _expand(tile, c0, c1, m0):
    return jnp.where(m0, tile[:, c0:c0 + 1], tile[:, c1:c1 + 1])


def _pair_row(tile, rolled, r0, r1, chunk):
    low = lax.broadcasted_iota(jnp.int32, (1, STK), 1) < CHUNK
    if chunk % 2 == 0:
        return jnp.where(low, tile[r0:r0 + 1, :], rolled[r1:r1 + 1, :])
    return jnp.where(low, rolled[r0:r0 + 1, :], tile[r1:r1 + 1, :])


def _stack(x, m0):
    return jnp.concatenate([jnp.where(m0, x, 0.0), jnp.where(m0, 0.0, x)], axis=0)


def _each(fn, *lists):
    return [fn(*args) for args in zip(*lists)]


def _no_tick():
    pass


def _unit_tri_solve(a_list, mk_list, rhs_list, m0, tick=_no_tick):
    eye = mk_list[0].eye
    pmm = lambda x, y: _mm(x, _stack(y, m0))
    d = _each(lambda a, mk: jnp.where(mk.blk16, a, 0.0), a_list, mk_list)
    rest = _each(lambda a, x: a - x, a_list, d)
    d2 = _each(lambda x: pmm(x, x), d)
    tick()
    d4 = _each(lambda x: pmm(x, x), d2)
    td = _each(lambda x, y: pmm(eye - x, eye + y), d, d2)
    tick()
    d8 = _each(lambda x: pmm(x, x), d4)
    td = _each(lambda x, y: pmm(x, eye + y), td, d4)
    tick()
    td = _each(lambda x, y: pmm(x, eye + y), td, d8)
    tick()
    n = _each(pmm, td, rest)
    tick()
    n2 = _each(lambda x: pmm(x, x), n)
    tick()
    t = _each(lambda x, y: pmm(eye - x, eye + y), n, n2)
    tick()
    t = _each(pmm, t, td)
    tick()
    return _each(_mm, t, rhs_list)


DIRS = (0, 1)
CHAINS = [(d, p) for d in DIRS for p in range(N_PAIRS)]


def _pair_lanes(p, group=0, width=GDN_W):
    return slice(group * width + p * PAIR_W, group * width + (p + 1) * PAIR_W)


def _chunk_rows(c):
    return slice(c * CHUNK, (c + 1) * CHUNK)


def _group_lanes(c):
    return slice((c // 2) * STK, (c // 2 + 1) * STK)


def _interleaved(local_fn, seq_gen):
    if seq_gen is None:
        local_fn(_no_tick)
        return
    local_fn(lambda: next(seq_gen, None))
    for _ in seq_gen:
        pass


def _part_chunks(n_chunks, part, m):
    fwd = list(range(part * m, (part + 1) * m))
    bwd = list(range(n_chunks - 1 - part * m, n_chunks - 1 - (part + 1) * m, -1))
    return (fwd, bwd)


def _run_parts(n_parts, local, seq):
    _interleaved(local(0), None)
    for k in range(1, n_parts):
        _interleaved(local(k), seq(k - 1))
    for _ in seq(n_parts - 1):
        pass


def _gdn_local(mks, m0, x_refs, g_refs, gt_refs, gp_ref, gpt_ref, cs, rcs, chunk_ids,
               u_ref, wq_ref, kd_ref, p_ref, egl_ref, tick=_no_tick):
    bias_row = gp_ref[0:1, :]
    a_row = -jnp.exp(gp_ref[1:2, :])
    bias_col = gpt_ref[0:GATE_ROWS, 0:1]
    a_col = -jnp.exp(gpt_ref[0:GATE_ROWS, 1:2])
    tiles, rows = {}, {}
    for d in DIRS:
        for j, c in enumerate(chunk_ids[d]):
            gates = g_refs[d][_chunk_rows(c), :]
            log_decay = a_row * _softplus(gates + bias_row)
            tiles[d, j] = (_split_mm_left(cs[d], log_decay, 3), _sigmoid(gates))
            if (d, c // 2) not in rows:
                gt = gt_refs[d][0:GATE_ROWS, _group_lanes(c)]
                gct = _split_mm(a_col * _softplus(gt + bias_col), rcs[d], 3)
                rows[d, c // 2] = (gct, pltpu.roll(gct, CHUNK, 1))
    tick()
    order = [(d, j, p) for j in range(len(chunk_ids[0])) for d in DIRS for p in range(N_PAIRS)]
    ch = [(d, chunk_ids[d][j], p) for d, j, p in order]
    mk = [mks[d] for d, j, p in order]
    col = [d * GDN_HEADS + 2 * p for d, j, p in order]
    gc = [tiles[d, j][0] for d, j, p in order]
    beta = [tiles[d, j][1] for d, j, p in order]
    gc_row = [_pair_row(*rows[d, c // 2], c0, c0 + 1, c) for (d, c, p), c0 in zip(ch, col)]
    q2 = [x_refs[d][_chunk_rows(c), _pair_lanes(p, 0)] for d, c, p in ch]
    k2 = [x_refs[d][_chunk_rows(c), _pair_lanes(p, 1)] for d, c, p in ch]
    v2 = [x_refs[d][_chunk_rows(c), _pair_lanes(p, 2)] for d, c, p in ch]

    gc2 = _each(lambda t, c0: _expand(t, c0, c0 + 1, m0), gc, col)
    b2 = _each(lambda t, c0: _expand(t, GATE_GROUP + c0, GATE_GROUP + c0 + 1, m0), beta, col)
    gl = _each(lambda g, m: g[m.last_row:m.last_row + 1, :], gc2, mk)
    eg2 = _each(jnp.exp, gc2)
    kb2 = _each(lambda k, b: k * b, k2, b2)
    vb2 = _each(lambda v, b: v * b, v2, b2)
    kbg2 = _each(lambda k, e: k * e, kb2, eg2)
    qg2 = _each(lambda q, e: q * e, q2, eg2)
    kdec2 = _each(lambda k, l, g: k * jnp.exp(l - g), k2, gl, gc2)
    decay = _each(lambda g, r, m: jnp.exp(jnp.where(m.incl, g - r, NEG)), gc2, gc_row, mk)
    aq = _each(lambda kb, q, k: _mm_nt(jnp.concatenate([kb, q], axis=0), _stack(k, m0)), kb2, q2, k2)
    tick()
    a = _each(lambda x, dc, m: jnp.where(m.strict, x[:CHUNK] * dc, 0.0), aq, decay, mk)
    pm = _each(lambda x, dc: x[CHUNK:] * dc, aq, decay)
    rhs = _each(lambda vb, kbg: jnp.concatenate([_stack(vb, m0), _stack(kbg, m0)], axis=1), vb2, kbg2)
    uw = _unit_tri_solve(a, mk, rhs, m0, tick)
    tick()
    for i, (d, c, p) in enumerate(ch):
        lanes = _pair_lanes(p)
        u_ref[d, c, :, lanes] = uw[i][:, :PAIR_W]
        wq_ref[d, c, 0:CHUNK, lanes] = _bf(uw[i][:, PAIR_W:])
        wq_ref[d, c, CHUNK:STK, lanes] = _bf(qg2[i])
        kd_ref[d, c, :, lanes] = _bf(kdec2[i])
        p_ref[d, c, p] = _bf(pm[i])
        egl_ref[d, c, :, lanes] = jnp.exp(gl[i])


def _gdn_seq(mks, m0, chunk_ids, o_refs, s_ref, u_ref, wq_ref, kd_ref, p_ref, egl_ref):
    for j in range(len(chunk_ids[0])):
        cidx = [chunk_ids[d][j] for d in DIRS]
        s = [s_ref[d, p] for d, p in CHAINS]
        wq = [_mm(wq_ref[d, cidx[d], :, _pair_lanes(p)], s_) for (d, p), s_ in zip(CHAINS, s)]
        yield
        vn = [u_ref[d, cidx[d], :, _pair_lanes(p)] - x[:CHUNK] for (d, p), x in zip(CHAINS, wq)]
        pv = [_mm(p_ref[d, cidx[d], p], _stack(v, m0)) for (d, p), v in zip(CHAINS, vn)]
        kv = [_mm_tn(kd_ref[d, cidx[d], :, _pair_lanes(p)], v) for (d, p), v in zip(CHAINS, vn)]
        yield
        for i, (d, p) in enumerate(CHAINS):
            o_refs[d][_chunk_rows(cidx[d]), _pair_lanes(p)] = (wq[i][CHUNK:] + pv[i]).astype(o_refs[d].dtype)
            egl = egl_ref[d, cidx[d], :, _pair_lanes(p)]
            s_ref[d, p] = egl * s[i] + jnp.where(mks[d].same, kv[i], 0.0)


def _gdn_kernel(first_ref, last_ref, xf_ref, xb_ref, gf_ref, gb_ref, gtf_ref, gtb_ref, gp_ref, gpt_ref,
                of_ref, ob_ref, s_ref, u_ref, wq_ref, kd_ref, p_ref, egl_ref):
    i = pl.program_id(0)
    nb = pl.num_programs(0)
    n_chunks = xf_ref.shape[0] // CHUNK

    @pl.when(first_ref[i] == 1)
    def _():
        s_ref[0] = jnp.zeros_like(s_ref[0])

    @pl.when(last_ref[nb - 1 - i] == 1)
    def _():
        s_ref[1] = jnp.zeros_like(s_ref[1])

    m0 = _lane_half()
    mks = [_Masks(backward=False), _Masks(backward=True)]
    cs = [_cumsum_matrix(backward=False), _cumsum_matrix(backward=True)]
    rcs = [_row_cumsum_matrix(backward=False), _row_cumsum_matrix(backward=True)]
    scratch = (u_ref, wq_ref, kd_ref, p_ref, egl_ref)

    def local(k):
        return lambda tick: _gdn_local(mks, m0, (xf_ref, xb_ref), (gf_ref, gb_ref), (gtf_ref, gtb_ref), gp_ref,
                                       gpt_ref, cs, rcs, _part_chunks(n_chunks, k, GDN_PART_CHUNKS), *scratch,
                                       tick=tick)

    def seq(k):
        return _gdn_seq(mks, m0, _part_chunks(n_chunks, k, GDN_PART_CHUNKS), (of_ref, ob_ref), s_ref, *scratch)

    _run_parts(n_chunks // GDN_PART_CHUNKS, local, seq)


def _rec_specs(nb, widths):
    fwd = lambda w: pl.BlockSpec((REC_BLOCK, w), lambda i, f, l: (i, 0))
    bwd = lambda w: pl.BlockSpec((REC_BLOCK, w), lambda i, f, l: (nb - 1 - i, 0))
    const = lambda r: pl.BlockSpec((r, GATE_W), lambda i, f, l: (0, 0))
    ins = [fwd(widths[0]), bwd(widths[0]), fwd(GATE_W), bwd(GATE_W),
           pl.BlockSpec((GATE_W, REC_BLOCK), lambda i, f, l: (0, i)),
           pl.BlockSpec((GATE_W, REC_BLOCK), lambda i, f, l: (0, nb - 1 - i)),
           const(8), const(GATE_W)]
    return ins, [fwd(widths[1]), bwd(widths[1])]


def _gdn(gqkv, gates, gates_t, gate_params, tables):
    t = gqkv.shape[0]
    nb = t // REC_BLOCK
    nc = REC_BLOCK // CHUNK
    first, last = tables[0], tables[1]
    in_specs, out_specs = _rec_specs(nb, (3 * GDN_W, GDN_W))
    grid_spec = pltpu.PrefetchScalarGridSpec(
        num_scalar_prefetch=2,
        grid=(nb,),
        in_specs=in_specs,
        out_specs=out_specs,
        scratch_shapes=[pltpu.VMEM((2, N_PAIRS, STK, STK), F32),
                        pltpu.VMEM((2, nc, CHUNK, GDN_W), F32),
                        pltpu.VMEM((2, nc, STK, GDN_W), BF16),
                        pltpu.VMEM((2, nc, CHUNK, GDN_W), BF16),
                        pltpu.VMEM((2, nc, N_PAIRS, CHUNK, STK), BF16),
                        pltpu.VMEM((2, nc, 1, GDN_W), F32)],
    )
    return pl.pallas_call(
        _gdn_kernel,
        grid_spec=grid_spec,
        out_shape=[jax.ShapeDtypeStruct((t, GDN_W), BF16)] * 2,
        compiler_params=_params("arbitrary"),
        name="gdn",
    )(first, last, gqkv, gqkv, gates, gates, gates_t, gates_t, gate_params[0], gate_params[1])


def _mlstm_local(mks, m0, bd2, x_refs, g_refs, gt_refs, gp_ref, gpt_ref, cs, rcs, chunk_ids,
                 upd_ref, intra_ref, mi_ref, fc_ref, fl_ref, ml_ref, tick=_no_tick):
    bias_row = gp_ref[0:1, :]
    bias_col = gpt_ref[0:GATE_ROWS, 0:1]
    tiles, rows = {}, {}
    for d in DIRS:
        for j, c in enumerate(chunk_ids[d]):
            li = g_refs[d][_chunk_rows(c), :] + bias_row
            tiles[d, j] = (_split_mm_left(cs[d], -_softplus(-li), 3), li)
            if (d, c // 2) not in rows:
                lit = gt_refs[d][0:GATE_ROWS, _group_lanes(c)] + bias_col
                fct = _split_mm(-_softplus(-lit), rcs[d], 3)
                rows[d, c // 2] = (fct, pltpu.roll(fct, CHUNK, 1), lit, pltpu.roll(lit, CHUNK, 1))
    tick()
    order = [(d, j, p) for j in range(len(chunk_ids[0])) for d in DIRS for p in range(N_PAIRS)]
    ch = [(d, chunk_ids[d][j], p) for d, j, p in order]
    mk = [mks[d] for d, j, p in order]
    col_i = [2 * GATE_GROUP + d * MLSTM_HEADS + 2 * p for d, j, p in order]
    col_f = [3 * GATE_GROUP + d * MLSTM_HEADS + 2 * p for d, j, p in order]
    fc = [tiles[d, j][0] for d, j, p in order]
    li = [tiles[d, j][1] for d, j, p in order]
    e_row = [_pair_row(rows[d, c // 2][0], rows[d, c // 2][1], cf, cf + 1, c)
             - _pair_row(rows[d, c // 2][2], rows[d, c // 2][3], ci, ci + 1, c)
             for (d, c, p), cf, ci in zip(ch, col_f, col_i)]
    q2 = [x_refs[d][_chunk_rows(c), _pair_lanes(p, 0, MLSTM_W)] for d, c, p in ch]
    k2 = [x_refs[d][_chunk_rows(c), _pair_lanes(p, 1, MLSTM_W)] * (HEAD_DIM ** -0.5) for d, c, p in ch]
    v2 = [x_refs[d][_chunk_rows(c), _pair_lanes(p, 2, MLSTM_W)] for d, c, p in ch]
    ones = jnp.ones((CHUNK, PAIR_W), F32)
    ones_st = _stack(ones, m0)

    fc2 = _each(lambda t, c0: _expand(t, c0, c0 + 1, m0), fc, col_f)
    li2 = _each(lambda t, c0: _expand(t, c0, c0 + 1, m0), li, col_i)
    fl = _each(lambda f, m: f[m.last_row:m.last_row + 1, :], fc2, mk)
    g_end = _each(lambda l, f, i_: l - f + i_, fl, fc2, li2)
    ml = _each(lambda g: jnp.max(g, axis=0, keepdims=True), g_end)
    kw2 = _each(lambda k, g, m: k * jnp.exp(g - m), k2, g_end, ml)
    upd = _each(lambda kw, v: jnp.where(bd2, _mm_tn(kw, jnp.concatenate([v, ones], axis=1)), 0.0), kw2, v2)
    tick()
    dmat = _each(lambda f, e, m: jnp.where(m.incl, f - e, NEG), fc2, e_row, mk)
    mi2 = _each(lambda x: jnp.where(m0, jnp.max(jnp.where(m0, x, NEG), axis=1, keepdims=True),
                                    jnp.max(jnp.where(m0, NEG, x), axis=1, keepdims=True)), dmat)
    tick()
    qk = _each(lambda q, k: _mm_nt(q, _stack(k, m0)), q2, k2)
    tick()
    s_loc = _each(lambda x, dm, mi: x * jnp.exp(dm - mi), qk, dmat, mi2)
    tick()
    intra = _each(lambda s, v: _mm(s, jnp.concatenate([_stack(v, m0), ones_st], axis=1)), s_loc, v2)
    tick()
    for i, (d, c, p) in enumerate(ch):
        lanes = _pair_lanes(p)
        upd_ref[d, c, p] = upd[i]
        intra_ref[d, c, p] = intra[i]
        mi_ref[d, c, :, lanes] = mi2[i]
        fc_ref[d, c, :, lanes] = fc2[i]
        fl_ref[d, c, :, lanes] = fl[i]
        ml_ref[d, c, :, lanes] = ml[i]
        if i % N_PAIRS == N_PAIRS - 1:
            tick()


def _mlstm_seq(chunk_ids, x_refs, o_refs, c_ref, m_ref, upd_ref, intra_ref, mi_ref, fc_ref, fl_ref, ml_ref):
    for j, d, p in [(j, d, p) for j in range(len(chunk_ids[0])) for d, p in CHAINS]:
        c = chunk_ids[d][j]
        lanes = _pair_lanes(p)
        m_st = m_ref[d, p]
        cn = c_ref[d, p]
        fl = fl_ref[d, c, :, lanes]
        ml = ml_ref[d, c, :, lanes]
        m_new = jnp.maximum(fl + m_st, ml)
        a_row = jnp.exp(fl + m_st - m_new)
        b_row = jnp.exp(ml - m_new)
        m_inter = fc_ref[d, c, :, lanes] + m_st
        mi2 = mi_ref[d, c, :, lanes]
        m_t = jnp.maximum(m_inter, mi2)
        w_intra = jnp.exp(mi2 - m_t)
        w_inter = jnp.exp(m_inter - m_t)
        inter = _mm(x_refs[d][_chunk_rows(c), _pair_lanes(p, 0, MLSTM_W)], cn)
        intra = intra_ref[d, c, p]
        num = w_intra * intra[:, :PAIR_W] + w_inter * inter[:, :PAIR_W]
        den = w_intra * intra[:, PAIR_W:] + w_inter * inter[:, PAIR_W:]
        out = num / jnp.maximum(jnp.abs(den), jnp.exp(-m_t))
        o_refs[d][_chunk_rows(c), lanes] = out.astype(o_refs[d].dtype)
        c_ref[d, p] = (jnp.concatenate([a_row, a_row], axis=1) * cn
                       + jnp.concatenate([b_row, b_row], axis=1) * upd_ref[d, c, p])
        m_ref[d, p] = m_new
        if p == N_PAIRS - 1:
            yield


def _mlstm_kernel(first_ref, last_ref, xf_ref, xb_ref, gf_ref, gb_ref, gtf_ref, gtb_ref, gp_ref, gpt_ref,
                  of_ref, ob_ref, c_ref, m_ref, upd_ref, intra_ref, mi_ref, fc_ref, fl_ref, ml_ref):
    i = pl.program_id(0)
    nb = pl.num_programs(0)
    n_chunks = xf_ref.shape[0] // CHUNK

    @pl.when(first_ref[i] == 1)
    def _():
        c_ref[0] = jnp.zeros_like(c_ref[0])
        m_ref[0] = jnp.full_like(m_ref[0], M_INIT)

    @pl.when(last_ref[nb - 1 - i] == 1)
    def _():
        c_ref[1] = jnp.zeros_like(c_ref[1])
        m_ref[1] = jnp.full_like(m_ref[1], M_INIT)

    m0 = _lane_half()
    r = lax.broadcasted_iota(jnp.int32, (STK, 2 * STK), 0) // HEAD_DIM
    c = (lax.broadcasted_iota(jnp.int32, (STK, 2 * STK), 1) % STK) // HEAD_DIM
    bd2 = r == c
    mks = [_Masks(backward=False), _Masks(backward=True)]
    cs = [_cumsum_matrix(backward=False), _cumsum_matrix(backward=True)]
    rcs = [_row_cumsum_matrix(backward=False), _row_cumsum_matrix(backward=True)]
    scratch = (upd_ref, intra_ref, mi_ref, fc_ref, fl_ref, ml_ref)

    def local(k):
        return lambda tick: _mlstm_local(mks, m0, bd2, (xf_ref, xb_ref), (gf_ref, gb_ref), (gtf_ref, gtb_ref),
                                         gp_ref, gpt_ref, cs, rcs, _part_chunks(n_chunks, k, MLSTM_PART_CHUNKS),
                                         *scratch, tick=tick)

    def seq(k):
        return _mlstm_seq(_part_chunks(n_chunks, k, MLSTM_PART_CHUNKS), (xf_ref, xb_ref), (of_ref, ob_ref), c_ref,
                          m_ref, *scratch)

    _run_parts(n_chunks // MLSTM_PART_CHUNKS, local, seq)


def _mlstm(ml_qkv, gates, gates_t, gate_params, tables):
    t = ml_qkv.shape[0]
    nb = t // REC_BLOCK
    nc = REC_BLOCK // CHUNK
    first, last = tables[0], tables[1]
    in_specs, out_specs = _rec_specs(nb, (3 * MLSTM_W, MLSTM_W))
    grid_spec = pltpu.PrefetchScalarGridSpec(
        num_scalar_prefetch=2,
        grid=(nb,),
        in_specs=in_specs,
        out_specs=out_specs,
        scratch_shapes=[pltpu.VMEM((2, N_PAIRS, STK, 2 * STK), F32),
                        pltpu.VMEM((2, N_PAIRS, 1, PAIR_W), F32),
                        pltpu.VMEM((2, nc, N_PAIRS, STK, 2 * STK), F32),
                        pltpu.VMEM((2, nc, N_PAIRS, CHUNK, 2 * PAIR_W), F32),
                        pltpu.VMEM((2, nc, CHUNK, MLSTM_W), F32),
                        pltpu.VMEM((2, nc, CHUNK, MLSTM_W), F32),
                        pltpu.VMEM((2, nc, 1, MLSTM_W), F32),
                        pltpu.VMEM((2, nc, 1, MLSTM_W), F32)],
    )
    return pl.pallas_call(
        _mlstm_kernel,
        grid_spec=grid_spec,
        out_shape=[jax.ShapeDtypeStruct((t, MLSTM_W), BF16)] * 2,
        compiler_params=_params("arbitrary"),
        name="mlstm",
    )(first, last, ml_qkv, ml_qkv, gates, gates, gates_t, gates_t, gate_params[0], gate_params[1])


OUT_TM = 512


def _outproj_kernel(x_ref, na_ref, gf_ref, gb_ref, z_ref, hf_ref, hb_ref, o_ref, gn_ref, mn_ref,
                    w_ref, y_ref):
    def head_rms(v, gain):
        ms = _head_sumsq(v) * (1.0 / HEAD_DIM)
        return v * lax.rsqrt(ms + EPS) * gain

    go = head_rms(gf_ref[...].astype(F32) + gb_ref[...].astype(F32), gn_ref[...])
    z = z_ref[...].astype(F32)
    gdn_out = go * (z * _sigmoid(z))
    mh = head_rms(hf_ref[...].astype(F32) + hb_ref[...].astype(F32), mn_ref[...])
    ml_out = _sigmoid(o_ref[...].astype(F32)) * mh
    y = x_ref[...]
    y = y + jnp.dot(na_ref[...], w_ref[0:NA_W, :], preferred_element_type=F32)
    y = y + jnp.dot(_bf(gdn_out), w_ref[NA_W:NA_W + GDN_W, :], preferred_element_type=F32)
    y = y + jnp.dot(_bf(ml_out), w_ref[NA_W + GDN_W:, :], preferred_element_type=F32)
    y_ref[...] = y


def _outproj(x, na_out, gf, gb, z, hf, hb, o, gdn_norm, mlstm_norm, w_out):
    t = x.shape[0]
    tm = min(OUT_TM, t)
    tok = lambda w: pl.BlockSpec((tm, w), lambda i: (i, 0))
    const = lambda r, w: pl.BlockSpec((r, w), lambda i: (0, 0))
    gn = jnp.tile(gdn_norm.astype(F32), GDN_HEADS).reshape(1, GDN_W)
    mn = mlstm_norm.astype(F32).reshape(1, MLSTM_W)
    return pl.pallas_call(
        _outproj_kernel,
        grid=(t // tm,),
        in_specs=[tok(D_MODEL), tok(NA_W), tok(GDN_W), tok(GDN_W), tok(GDN_W), tok(MLSTM_W), tok(MLSTM_W),
                  tok(MLSTM_W), const(1, GDN_W), const(1, MLSTM_W), const(D_MODEL, D_MODEL)],
        out_specs=tok(D_MODEL),
        out_shape=jax.ShapeDtypeStruct((t, D_MODEL), F32),
        compiler_params=_params("parallel"),
        name="outproj",
    )(x, na_out, gf, gb, z, hf, hb, o, gn, mn, _bf(w_out))


def _gate_params(bias_rows, a_log):
    pad = GATE_W - 4 * GATE_GROUP
    row0 = jnp.concatenate([b.astype(F32).reshape(-1) for b in bias_rows] + [jnp.zeros((pad,), F32)])
    row1 = jnp.concatenate([a_log.astype(F32).reshape(-1), jnp.zeros((GATE_W - GATE_GROUP,), F32)])
    rows = jnp.concatenate([row0[None], row1[None], jnp.zeros((6, GATE_W), F32)], axis=0)
    cols = jnp.concatenate([rows.T, jnp.zeros((GATE_W, GATE_W - 8), F32)], axis=1)
    return rows, cols


def _trunk(xs, seq_lens, p):
    depth = p["ffn1_norm"].shape[0]
    tables = tuple(jnp.asarray(a) for a in _block_tables(seq_lens, MIX_BLOCK))
    rec_tables = tuple(jnp.asarray(a) for a in _block_tables(seq_lens, REC_BLOCK))
    zeros_g = jnp.zeros((2, GDN_HEADS), F32)
    out_sizes = [x.shape[0] for x in xs]
    x = None
    for i in range(depth):
        x = _ffn(xs if i == 0 else [x], p["ffn1_norm"][i], _bf(p["ffn1_w_in"][i]), _bf(p["ffn1_w_out"][i]),
                 p["final_norm"], False)
        na_qkv, gqkv, gdn_z, ml_qkv, ml_o, gates = _inproj(x, p["mix_norm"][i], _permute_w_in(p["w_in"][i]),
                                                           p["gdn_conv"][i], tables)
        na_out = _na(na_qkv, _na_bias_table(p["na_rpb"][i]), tables)
        gdn_gp = _gate_params([p["gdn_dt_bias"][i], zeros_g, zeros_g, zeros_g], p["gdn_a_log"][i])
        gates_t = gates.T
        gf, gb = _gdn(gqkv, gates, gates_t, gdn_gp, rec_tables)
        ml_gp = _gate_params([zeros_g, zeros_g, p["mlstm_i_bias"][i], p["mlstm_f_bias"][i]], zeros_g)
        hf, hb = _mlstm(ml_qkv, gates, gates_t, ml_gp, rec_tables)
        x = _outproj(x, na_out, gf, gb, gdn_z, hf, hb, ml_o, p["gdn_norm"][i], p["mlstm_norm"][i], p["w_out"][i])
        last = i == depth - 1
        x = _ffn([x], p["ffn2_norm"][i], _bf(p["ffn2_w_in"][i]), _bf(p["ffn2_w_out"][i]), p["final_norm"],
                 last, out_sizes if last else None)
    return x if len(out_sizes) > 1 else [x]


def kernel(x_prompt, x_sample, ffn1_norm, ffn1_w_in, ffn1_w_out, mix_norm, w_in, w_out, na_rpb, gdn_conv,
           gdn_a_log, gdn_dt_bias, gdn_norm, mlstm_i_bias, mlstm_f_bias, mlstm_norm, ffn2_norm, ffn2_w_in,
           ffn2_w_out, final_norm):
    p = dict(ffn1_norm=ffn1_norm, ffn1_w_in=ffn1_w_in, ffn1_w_out=ffn1_w_out, mix_norm=mix_norm, w_in=w_in,
             w_out=w_out, na_rpb=na_rpb, gdn_conv=gdn_conv, gdn_a_log=gdn_a_log, gdn_dt_bias=gdn_dt_bias,
             gdn_norm=gdn_norm, mlstm_i_bias=mlstm_i_bias, mlstm_f_bias=mlstm_f_bias, mlstm_norm=mlstm_norm,
             ffn2_norm=ffn2_norm, ffn2_w_in=ffn2_w_in, ffn2_w_out=ffn2_w_out, final_norm=final_norm)
    bp, lp, _ = x_prompt.shape
    bs, ls, _ = x_sample.shape
    y_prompt, y_sample = _trunk([x_prompt.reshape(bp * lp, D_MODEL), x_sample.reshape(bs * ls, D_MODEL)],
                                (lp,) * bp + (ls,) * bs, p)
    return (y_prompt.reshape(bp, lp, D_MODEL), y_sample.reshape(bs, ls, D_MODEL))
```

```python
import functools

import jax
import jax.numpy as jnp
import numpy as np
from jax import lax
from jax.experimental import pallas as pl
from jax.experimental.pallas import tpu as pltpu

D_MODEL = 1024
D_FF = 2816
HEAD_DIM = 64
NA_HEADS = 4
GDN_HEADS = 6
MLSTM_HEADS = 6
NA_W = NA_HEADS * HEAD_DIM
GDN_W = GDN_HEADS * HEAD_DIM
MLSTM_W = MLSTM_HEADS * HEAD_DIM
GRID_W = 64
NA_WIN_ROWS = 8
NA_WIN_COLS = 16
GDN_CONV = 5
CHUNK = 64
M_INIT = -1e30
EPS = 1e-6
NEG = -1e30

LANES = 128
PAIR_W = 2 * HEAD_DIM
N_PAIRS = GDN_HEADS // 2
GATE_W = LANES
GATE_GROUP = 2 * GDN_HEADS
MIX_BLOCK = 512
REC_BLOCK = 1024
GDN_PART_CHUNKS = 4
MLSTM_PART_CHUNKS = 16
VMEM_LIMIT = 56 * 1024 * 1024

F32 = jnp.float32
BF16 = jnp.bfloat16


def _bf(x):
    return x.astype(BF16)


def _mm(a, b):
    return jnp.dot(_bf(a), _bf(b), preferred_element_type=F32)


def _mm_nt(a, b):
    return lax.dot_general(_bf(a), _bf(b), (((1,), (1,)), ((), ())), preferred_element_type=F32)


def _mm_tn(a, b):
    return lax.dot_general(_bf(a), _bf(b), (((0,), (0,)), ((), ())), preferred_element_type=F32)


def _split_mm(a, b_exact, parts):
    acc = None
    rem = a
    for _ in range(parts):
        piece = _bf(rem)
        term = jnp.dot(piece, b_exact, preferred_element_type=F32)
        acc = term if acc is None else acc + term
        rem = rem - piece.astype(F32)
    return acc


def _split_mm_left(a_exact, b, parts):
    acc = None
    rem = b
    for _ in range(parts):
        piece = _bf(rem)
        term = jnp.dot(a_exact, piece, preferred_element_type=F32)
        acc = term if acc is None else acc + term
        rem = rem - piece.astype(F32)
    return acc


def _sigmoid(x):
    return 1.0 / (1.0 + jnp.exp(-x))


def _softplus(x):
    return jnp.maximum(x, 0.0) + jnp.log(1.0 + jnp.exp(-jnp.abs(x)))


def _rms(x, gain):
    return x * lax.rsqrt(jnp.mean(x * x, axis=-1, keepdims=True) + EPS) * gain


def _params(*sem):
    return pltpu.CompilerParams(dimension_semantics=sem, vmem_limit_bytes=VMEM_LIMIT)


FFN_TM = 1024
FFN_SUB = 512
FFN_TF = 256


def _ffn_kernel(*refs, final, sub, in_blocks, out_blocks):
    n_in, n_out = len(in_blocks), len(out_blocks)
    x_refs = refs[:n_in]
    g_ref, wi_ref, wo_ref, fn_ref = refs[n_in:n_in + 4]
    o_refs = refs[n_in + 4:n_in + 4 + n_out]
    act_ref = refs[-1]
    i = pl.program_id(0)
    in_starts = np.cumsum((0,) + tuple(in_blocks))
    out_starts = np.cumsum((0,) + tuple(out_blocks))
    for s in range(x_refs[0].shape[0] // sub):
        rows = slice(s * sub, (s + 1) * sub)
        x = x_refs[-1][rows, :]
        for k in range(n_in - 2, -1, -1):
            x = jnp.where(i < in_starts[k + 1], x_refs[k][rows, :], x)
        xn = _bf(_rms(x, g_ref[...]))
        for j in range(D_FF // FFN_TF):
            gate = jnp.dot(xn, wi_ref[:, j * FFN_TF:(j + 1) * FFN_TF], preferred_element_type=F32)
            up = jnp.dot(xn, wi_ref[:, D_FF + j * FFN_TF:D_FF + (j + 1) * FFN_TF], preferred_element_type=F32)
            act_ref[rows, j * FFN_TF:(j + 1) * FFN_TF] = _bf(gate * _sigmoid(gate) * up)
        y = x + 0.5 * jnp.dot(act_ref[rows, :], wo_ref[...], preferred_element_type=F32)
        if final:
            y = _rms(y, fn_ref[...])
        if n_out == 1:
            o_refs[0][rows, :] = y
        else:
            for k in range(n_out):
                @pl.when(jnp.logical_and(i >= out_starts[k], i < out_starts[k + 1]))
                def _(k=k, y=y):
                    o_refs[k][rows, :] = y


def _ffn(xs, gain, w_in, w_out, final_gain, final, out_sizes=None):
    sizes = [x.shape[0] for x in xs]
    t = sum(sizes)
    out_sizes = [t] if out_sizes is None else list(out_sizes)
    tm = min([FFN_TM] + sizes + out_sizes)
    sub = min(FFN_SUB, tm)
    in_blocks = [s // tm for s in sizes]
    out_blocks = [s // tm for s in out_sizes]

    def part_spec(blocks, k):
        start = int(np.sum(blocks[:k]))
        return pl.BlockSpec((tm, D_MODEL), lambda i: (jnp.clip(i - start, 0, blocks[k] - 1), 0))

    resident = lambda shape: pl.BlockSpec(shape, lambda i: (0, 0), pipeline_mode=pl.Buffered(1))
    outs = pl.pallas_call(
        functools.partial(_ffn_kernel, final=final, sub=sub, in_blocks=tuple(in_blocks),
                          out_blocks=tuple(out_blocks)),
        grid=(t // tm,),
        in_specs=[part_spec(in_blocks, k) for k in range(len(xs))] + [
            resident((1, D_MODEL)),
            resident((D_MODEL, 2 * D_FF)),
            resident((D_FF, D_MODEL)),
            resident((1, D_MODEL)),
        ],
        out_specs=[part_spec(out_blocks, k) for k in range(len(out_sizes))],
        out_shape=[jax.ShapeDtypeStruct((s, D_MODEL), F32) for s in out_sizes],
        scratch_shapes=[pltpu.VMEM((tm, D_FF), BF16)],
        compiler_params=_params("arbitrary"),
        name="ffn",
    )(*xs, gain.reshape(1, D_MODEL), w_in, w_out, final_gain.reshape(1, D_MODEL))
    return outs[0] if len(outs) == 1 else outs


PROJ_TM = 512
PROJ_WIDTHS = (3 * NA_W, 3 * GDN_W, GDN_W, 3 * MLSTM_W, MLSTM_W, GATE_W)
PROJ_DTYPES = (BF16, F32, BF16, F32, BF16, F32)
P_PERM = sum(PROJ_WIDTHS)


def _permute_w_in(w_in):
    o = np.cumsum([0, 3 * NA_W, 3 * GDN_W, GDN_W, 2 * GDN_HEADS, 2 * GDN_HEADS,
                   3 * MLSTM_W, MLSTM_W, 2 * MLSTM_HEADS, 2 * MLSTM_HEADS])
    n_gates = 4 * 2 * GDN_HEADS
    cols = [w_in[:, o[0]:o[3]], w_in[:, o[5]:o[7]], w_in[:, o[3]:o[5]], w_in[:, o[7]:o[9]],
            jnp.zeros((D_MODEL, GATE_W - n_gates), w_in.dtype)]
    return _bf(jnp.concatenate(cols, axis=1))


PROJ_DOT_GROUPS = ((1, 2), (0,), (3, 4), (5,))
PROJ_OFFSETS = tuple(int(v) for v in np.cumsum((0,) + PROJ_WIDTHS[:-1]))
PROJ_HALO = 16


def _head_sumsq(x):
    r = lax.broadcasted_iota(jnp.int32, (PAIR_W, PAIR_W), 0) // HEAD_DIM
    c = lax.broadcasted_iota(jnp.int32, (PAIR_W, PAIR_W), 1) // HEAD_DIM
    ones_bd = jnp.where(r == c, 1.0, 0.0).astype(BF16)
    sq = x * x
    parts = [_split_mm(sq[:, g:g + PAIR_W], ones_bd, 2) for g in range(0, x.shape[1], PAIR_W)]
    return jnp.concatenate(parts, axis=1)


def _gdn_preprocess(y, w_ref, tb):
    n = y.shape[0]
    acc = None
    for j in range(GDN_CONV):
        shift = (GDN_CONV // 2 - j) % n
        tap = y if shift == 0 else pltpu.roll(y, shift, 0)
        term = tap[PROJ_HALO:PROJ_HALO + tb] * w_ref[j:j + 1, :]
        acc = term if acc is None else acc + term
    act = acc * _sigmoid(acc)
    q = act[:, 0:GDN_W]
    k = act[:, GDN_W:2 * GDN_W]
    qn = q * lax.rsqrt(_head_sumsq(q) + EPS) * (HEAD_DIM ** -0.5)
    kn = k * lax.rsqrt(_head_sumsq(k) + EPS)
    return qn, kn, act[:, 2 * GDN_W:3 * GDN_W]


def _inproj_kernel(first_ref, last_ref, x_ref, xp_ref, xn_ref, g_ref, w_ref, cw_ref, *o_refs):
    i = pl.program_id(0)
    tb = x_ref.shape[0]
    prev = jnp.where(first_ref[i] == 1, 0.0, xp_ref[...])
    nxt = jnp.where(last_ref[i] == 1, 0.0, xn_ref[...])
    xe = jnp.concatenate([prev, x_ref[...], nxt], axis=0)
    xn = _bf(_rms(xe, g_ref[...]))
    centre = slice(PROJ_HALO, PROJ_HALO + tb)
    xn_centre = xn[centre]
    gdn_part = None
    for group in PROJ_DOT_GROUPS:
        off = PROJ_OFFSETS[group[0]]
        width = sum(PROJ_WIDTHS[k] for k in group)
        with_halo = 1 in group
        res = jnp.dot(xn if with_halo else xn_centre, w_ref[:, off:off + width], preferred_element_type=F32)
        col = 0
        for k in group:
            part = res[:, col:col + PROJ_WIDTHS[k]]
            col += PROJ_WIDTHS[k]
            if k == 1:
                gdn_part = part
            else:
                o_refs[k][...] = (part[centre] if with_halo else part).astype(o_refs[k].dtype)
    qn, kn, v = _gdn_preprocess(gdn_part, cw_ref, tb)
    o_refs[1][:, 0:GDN_W] = qn
    o_refs[1][:, GDN_W:2 * GDN_W] = kn
    o_refs[1][:, 2 * GDN_W:3 * GDN_W] = v


def _inproj(x, gain, w_perm, conv_w, tables):
    t = x.shape[0]
    tm = MIX_BLOCK
    per = tm // PROJ_HALO
    n_halo = t // PROJ_HALO
    first, last = tables[0], tables[1]
    const = lambda shape: pl.BlockSpec(shape, lambda i, f, l: (0, 0))
    grid_spec = pltpu.PrefetchScalarGridSpec(
        num_scalar_prefetch=2,
        grid=(t // tm,),
        in_specs=[
            pl.BlockSpec((tm, D_MODEL), lambda i, f, l: (i, 0)),
            pl.BlockSpec((PROJ_HALO, D_MODEL), lambda i, f, l: (jnp.maximum(i * per - 1, 0), 0)),
            pl.BlockSpec((PROJ_HALO, D_MODEL), lambda i, f, l: (jnp.minimum((i + 1) * per, n_halo - 1), 0)),
            const((1, D_MODEL)),
            const((D_MODEL, P_PERM)),
            const((8, 3 * GDN_W)),
        ],
        out_specs=[pl.BlockSpec((tm, w), lambda i, f, l: (i, 0)) for w in PROJ_WIDTHS],
    )
    w8 = jnp.concatenate([conv_w.astype(F32), jnp.zeros((8 - GDN_CONV, 3 * GDN_W), F32)], axis=0)
    return pl.pallas_call(
        _inproj_kernel,
        grid_spec=grid_spec,
        out_shape=[jax.ShapeDtypeStruct((t, w), d) for w, d in zip(PROJ_WIDTHS, PROJ_DTYPES)],
        compiler_params=_params("arbitrary"),
        name="inproj",
    )(first, last, x, x, x, gain.reshape(1, D_MODEL), w_perm, w8)


def _block_tables(seq_lens, block):
    first, last = [], []
    for n in seq_lens:
        nb = n // block
        first += [1] + [0] * (nb - 1)
        last += [0] * (nb - 1) + [1]
    first = np.asarray(first, np.int32)
    last = np.asarray(last, np.int32)
    idx = np.arange(len(first), dtype=np.int32)
    prev = np.where(first == 1, idx, idx - 1).astype(np.int32)
    nxt = np.where(last == 1, idx, idx + 1).astype(np.int32)
    return first, last, prev, nxt


NA_ROWS = MIX_BLOCK // GRID_W
NA_KEYS = NA_WIN_ROWS * GRID_W
NA_ROW_GROUP = 4


def _na_bias_table(rpb):
    c = np.arange(GRID_W)
    c_start = np.clip(c - NA_WIN_COLS // 2, 0, GRID_W - NA_WIN_COLS)
    in_win = (c[None, :] >= c_start[:, None]) & (c[None, :] < c_start[:, None] + NA_WIN_COLS)
    dc = np.clip(c[None, :] - c[:, None], 1 - NA_WIN_COLS, NA_WIN_COLS - 1) + NA_WIN_COLS - 1
    one_hot = jnp.asarray(dc[None] == np.arange(2 * NA_WIN_COLS - 1)[:, None, None], F32)
    by_col = jnp.einsum("hrc,cqk->hrqk", rpb.astype(F32), one_hot, precision=lax.Precision.HIGHEST)
    tab = jnp.stack([by_col[:, NA_WIN_ROWS - 1 - v:2 * NA_WIN_ROWS - 1 - v] for v in range(NA_WIN_ROWS)],
                    axis=1)
    tab = jnp.where(in_win[None, None, None], tab, NEG).transpose(0, 1, 3, 2, 4)
    return tab.reshape(NA_HEADS, NA_WIN_ROWS, GRID_W, NA_KEYS)


def _na_kernel(first_ref, last_ref, prev_ref, next_ref,
               q_ref, kp_ref, kc_ref, kn_ref, vp_ref, vc_ref, vn_ref, bias_ref, o_ref, kbuf, vbuf):
    del prev_ref, next_ref
    g = pl.program_id(0)
    kbuf[0:MIX_BLOCK] = kp_ref[...]
    kbuf[MIX_BLOCK:2 * MIX_BLOCK] = kc_ref[...]
    kbuf[2 * MIX_BLOCK:3 * MIX_BLOCK] = kn_ref[...]
    vbuf[0:MIX_BLOCK] = vp_ref[...]
    vbuf[MIX_BLOCK:2 * MIX_BLOCK] = vc_ref[...]
    vbuf[2 * MIX_BLOCK:3 * MIX_BLOCK] = vn_ref[...]
    scale = jnp.asarray(HEAD_DIM ** -0.5, BF16)
    half = NA_WIN_ROWS // 2
    m0 = lax.broadcasted_iota(jnp.int32, (GRID_W, PAIR_W), 1) < HEAD_DIM
    for a0 in range(0, NA_ROWS, NA_ROW_GROUP):
        q, k, v, bias = [], [], [], []
        for a in range(a0, a0 + NA_ROW_GROUP):
            edge = (first_ref[g] if a < half else last_ref[g]) == 1
            start = pl.multiple_of(jnp.where(edge, NA_ROWS, a + half) * GRID_W, GRID_W)
            variant = jnp.where(edge, a, half)
            for h in range(NA_HEADS):
                lanes = slice((h // 2) * PAIR_W, (h // 2 + 1) * PAIR_W)
                q2 = q_ref[a * GRID_W:(a + 1) * GRID_W, lanes] * scale
                q.append(jnp.where(m0 if h % 2 == 0 else jnp.logical_not(m0), q2, jnp.zeros_like(q2)))
                k.append(kbuf[pl.ds(start, NA_KEYS), lanes])
                v.append(vbuf[pl.ds(start, NA_KEYS), lanes])
                bias.append(bias_ref[h, variant])
        s = _each(lambda q_, k_, b_: _mm_nt(q_, k_) + b_, q, k, bias)
        p = _each(lambda s_: jnp.exp(s_ - jnp.max(s_, axis=-1, keepdims=True)), s)
        l = _each(lambda p_: jnp.sum(p_, axis=-1, keepdims=True), p)
        o = _each(lambda p_, v_, l_: _mm(p_, v_) / l_, p, v, l)
        for j in range(0, len(o), 2):
            a = a0 + j // NA_HEADS
            lanes = slice(((j % NA_HEADS) // 2) * PAIR_W, ((j % NA_HEADS) // 2 + 1) * PAIR_W)
            o_ref[a * GRID_W:(a + 1) * GRID_W, lanes] = jnp.where(m0, o[j], o[j + 1]).astype(o_ref.dtype)


def _na(na_qkv, bias, tables):
    t = na_qkv.shape[0]
    nb = t // MIX_BLOCK
    blk = (MIX_BLOCK, NA_W)
    cur = lambda col: pl.BlockSpec(blk, lambda g, f, l, p, n: (g, col))
    prv = lambda col: pl.BlockSpec(blk, lambda g, f, l, p, n: (p[g], col))
    nxt = lambda col: pl.BlockSpec(blk, lambda g, f, l, p, n: (n[g], col))
    grid_spec = pltpu.PrefetchScalarGridSpec(
        num_scalar_prefetch=4,
        grid=(nb,),
        in_specs=[cur(0), prv(1), cur(1), nxt(1), prv(2), cur(2), nxt(2),
                  pl.BlockSpec((NA_HEADS, NA_WIN_ROWS, GRID_W, NA_KEYS), lambda g, f, l, p, n: (0, 0, 0, 0))],
        out_specs=pl.BlockSpec(blk, lambda g, f, l, p, n: (g, 0)),
        scratch_shapes=[pltpu.VMEM((3 * MIX_BLOCK, NA_W), BF16), pltpu.VMEM((3 * MIX_BLOCK, NA_W), BF16)],
    )
    return pl.pallas_call(
        _na_kernel,
        grid_spec=grid_spec,
        out_shape=jax.ShapeDtypeStruct((t, NA_W), BF16),
        compiler_params=_params("arbitrary"),
        name="na",
    )(*tables, na_qkv, na_qkv, na_qkv, na_qkv, na_qkv, na_qkv, na_qkv, bias)


STK = 2 * CHUNK
GATE_ROWS = 4 * GATE_GROUP


class _Masks:
    def __init__(self, backward):
        r = lax.broadcasted_iota(jnp.int32, (CHUNK, STK), 0)
        c = lax.broadcasted_iota(jnp.int32, (CHUNK, STK), 1) % CHUNK
        self.incl = (c >= r) if backward else (c <= r)
        self.strict = (c > r) if backward else (c < r)
        self.blk16 = (r // 16) == (c // 16)
        self.eye = jnp.where(r == c, 1.0, 0.0).astype(F32)
        rr = lax.broadcasted_iota(jnp.int32, (STK, STK), 0) // CHUNK
        cc = lax.broadcasted_iota(jnp.int32, (STK, STK), 1) // CHUNK
        self.same = rr == cc
        self.last_row = 0 if backward else CHUNK - 1


def _cumsum_matrix(backward):
    r = lax.broadcasted_iota(jnp.int32, (CHUNK, CHUNK), 0)
    c = lax.broadcasted_iota(jnp.int32, (CHUNK, CHUNK), 1)
    keep = (c >= r) if backward else (c <= r)
    return jnp.where(keep, 1.0, 0.0).astype(BF16)


def _row_cumsum_matrix(backward):
    r = lax.broadcasted_iota(jnp.int32, (STK, STK), 0)
    c = lax.broadcasted_iota(jnp.int32, (STK, STK), 1)
    keep = jnp.logical_and(r // CHUNK == c // CHUNK, (r >= c) if backward else (r <= c))
    return jnp.where(keep, 1.0, 0.0).astype(BF16)


def _lane_half():
    return lax.broadcasted_iota(jnp.int32, (CHUNK, PAIR_W), 1) < HEAD_DIM


def _expand(tile, c0, c1, m0):
    return jnp.where(m0, tile[:, c0:c0 + 1], tile[:, c1:c1 + 1])


def _pair_row(tile, rolled, r0, r1, chunk):
    low = lax.broadcasted_iota(jnp.int32, (1, STK), 1) < CHUNK
    if chunk % 2 == 0:
        return jnp.where(low, tile[r0:r0 + 1, :], rolled[r1:r1 + 1, :])
    return jnp.where(low, rolled[r0:r0 + 1, :], tile[r1:r1 + 1, :])


def _stack(x, m0):
    return jnp.concatenate([jnp.where(m0, x, 0.0), jnp.where(m0, 0.0, x)], axis=0)


def _each(fn, *lists):
    return [fn(*args) for args in zip(*lists)]


def _no_tick():
    pass


def _unit_tri_solve(a_list, mk_list, rhs_list, m0, tick=_no_tick):
    eye = mk_list[0].eye
    pmm = lambda x, y: _mm(x, _stack(y, m0))
    d = _each(lambda a, mk: jnp.where(mk.blk16, a, 0.0), a_list, mk_list)
    rest = _each(lambda a, x: a - x, a_list, d)
    d2 = _each(lambda x: pmm(x, x), d)
    tick()
    d4 = _each(lambda x: pmm(x, x), d2)
    td = _each(lambda x, y: pmm(eye - x, eye + y), d, d2)
    tick()
    d8 = _each(lambda x: pmm(x, x), d4)
    td = _each(lambda x, y: pmm(x, eye + y), td, d4)
    tick()
    td = _each(lambda x, y: pmm(x, eye + y), td, d8)
    tick()
    n = _each(pmm, td, rest)
    tick()
    n2 = _each(lambda x: pmm(x, x), n)
    tick()
    t = _each(lambda x, y: pmm(eye - x, eye + y), n, n2)
    tick()
    t = _each(pmm, t, td)
    tick()
    return _each(_mm, t, rhs_list)


DIRS = (0, 1)
CHAINS = [(d, p) for d in DIRS for p in range(N_PAIRS)]


def _pair_lanes(p, group=0, width=GDN_W):
    return slice(group * width + p * PAIR_W, group * width + (p + 1) * PAIR_W)


def _chunk_rows(c):
    return slice(c * CHUNK, (c + 1) * CHUNK)


def _group_lanes(c):
    return slice((c // 2) * STK, (c // 2 + 1) * STK)


def _interleaved(local_fn, seq_gen):
    if seq_gen is None:
        local_fn(_no_tick)
        return
    local_fn(lambda: next(seq_gen, None))
    for _ in seq_gen:
        pass


def _part_chunks(n_chunks, part, m):
    fwd = list(range(part * m, (part + 1) * m))
    bwd = list(range(n_chunks - 1 - part * m, n_chunks - 1 - (part + 1) * m, -1))
    return (fwd, bwd)


def _run_parts(n_parts, local, seq):
    _interleaved(local(0), None)
    for k in range(1, n_parts):
        _interleaved(local(k), seq(k - 1))
    for _ in seq(n_parts - 1):
        pass


def _gdn_local(mks, m0, x_refs, g_refs, gt_refs, gp_ref, gpt_ref, cs, rcs, chunk_ids,
               u_ref, wq_ref, kd_ref, p_ref, egl_ref, tick=_no_tick):
    bias_row = gp_ref[0:1, :]
    a_row = -jnp.exp(gp_ref[1:2, :])
    bias_col = gpt_ref[0:GATE_ROWS, 0:1]
    a_col = -jnp.exp(gpt_ref[0:GATE_ROWS, 1:2])
    tiles, rows = {}, {}
    for d in DIRS:
        for j, c in enumerate(chunk_ids[d]):
            gates = g_refs[d][_chunk_rows(c), :]
            log_decay = a_row * _softplus(gates + bias_row)
            tiles[d, j] = (_split_mm_left(cs[d], log_decay, 3), _sigmoid(gates))
            if (d, c // 2) not in rows:
                gt = gt_refs[d][0:GATE_ROWS, _group_lanes(c)]
                gct = _split_mm(a_col * _softplus(gt + bias_col), rcs[d], 3)
                rows[d, c // 2] = (gct, pltpu.roll(gct, CHUNK, 1))
    tick()
    order = [(d, j, p) for j in range(len(chunk_ids[0])) for d in DIRS for p in range(N_PAIRS)]
    ch = [(d, chunk_ids[d][j], p) for d, j, p in order]
    mk = [mks[d] for d, j, p in order]
    col = [d * GDN_HEADS + 2 * p for d, j, p in order]
    gc = [tiles[d, j][0] for d, j, p in order]
    beta = [tiles[d, j][1] for d, j, p in order]
    gc_row = [_pair_row(*rows[d, c // 2], c0, c0 + 1, c) for (d, c, p), c0 in zip(ch, col)]
    q2 = [x_refs[d][_chunk_rows(c), _pair_lanes(p, 0)] for d, c, p in ch]
    k2 = [x_refs[d][_chunk_rows(c), _pair_lanes(p, 1)] for d, c, p in ch]
    v2 = [x_refs[d][_chunk_rows(c), _pair_lanes(p, 2)] for d, c, p in ch]

    gc2 = _each(lambda t, c0: _expand(t, c0, c0 + 1, m0), gc, col)
    b2 = _each(lambda t, c0: _expand(t, GATE_GROUP + c0, GATE_GROUP + c0 + 1, m0), beta, col)
    gl = _each(lambda g, m: g[m.last_row:m.last_row + 1, :], gc2, mk)
    eg2 = _each(jnp.exp, gc2)
    kb2 = _each(lambda k, b: k * b, k2, b2)
    vb2 = _each(lambda v, b: v * b, v2, b2)
    kbg2 = _each(lambda k, e: k * e, kb2, eg2)
    qg2 = _each(lambda q, e: q * e, q2, eg2)
    kdec2 = _each(lambda k, l, g: k * jnp.exp(l - g), k2, gl, gc2)
    decay = _each(lambda g, r, m: jnp.exp(jnp.where(m.incl, g - r, NEG)), gc2, gc_row, mk)
    aq = _each(lambda kb, q, k: _mm_nt(jnp.concatenate([kb, q], axis=0), _stack(k, m0)), kb2, q2, k2)
    tick()
    a = _each(lambda x, dc, m: jnp.where(m.strict, x[:CHUNK] * dc, 0.0), aq, decay, mk)
    pm = _each(lambda x, dc: x[CHUNK:] * dc, aq, decay)
    rhs = _each(lambda vb, kbg: jnp.concatenate([_stack(vb, m0), _stack(kbg, m0)], axis=1), vb2, kbg2)
    uw = _unit_tri_solve(a, mk, rhs, m0, tick)
    tick()
    for i, (d, c, p) in enumerate(ch):
        lanes = _pair_lanes(p)
        u_ref[d, c, :, lanes] = uw[i][:, :PAIR_W]
        wq_ref[d, c, 0:CHUNK, lanes] = _bf(uw[i][:, PAIR_W:])
        wq_ref[d, c, CHUNK:STK, lanes] = _bf(qg2[i])
        kd_ref[d, c, :, lanes] = _bf(kdec2[i])
        p_ref[d, c, p] = _bf(pm[i])
        egl_ref[d, c, :, lanes] = jnp.exp(gl[i])


def _gdn_seq(mks, m0, chunk_ids, o_refs, s_ref, u_ref, wq_ref, kd_ref, p_ref, egl_ref):
    for j in range(len(chunk_ids[0])):
        cidx = [chunk_ids[d][j] for d in DIRS]
        s = [s_ref[d, p] for d, p in CHAINS]
        wq = [_mm(wq_ref[d, cidx[d], :, _pair_lanes(p)], s_) for (d, p), s_ in zip(CHAINS, s)]
        yield
        vn = [u_ref[d, cidx[d], :, _pair_lanes(p)] - x[:CHUNK] for (d, p), x in zip(CHAINS, wq)]
        pv = [_mm(p_ref[d, cidx[d], p], _stack(v, m0)) for (d, p), v in zip(CHAINS, vn)]
        kv = [_mm_tn(kd_ref[d, cidx[d], :, _pair_lanes(p)], v) for (d, p), v in zip(CHAINS, vn)]
        yield
        for i, (d, p) in enumerate(CHAINS):
            o_refs[d][_chunk_rows(cidx[d]), _pair_lanes(p)] = (wq[i][CHUNK:] + pv[i]).astype(o_refs[d].dtype)
            egl = egl_ref[d, cidx[d], :, _pair_lanes(p)]
            s_ref[d, p] = egl * s[i] + jnp.where(mks[d].same, kv[i], 0.0)


def _gdn_kernel(first_ref, last_ref, xf_ref, xb_ref, gf_ref, gb_ref, gtf_ref, gtb_ref, gp_ref, gpt_ref,
                of_ref, ob_ref, s_ref, u_ref, wq_ref, kd_ref, p_ref, egl_ref):
    i = pl.program_id(0)
    nb = pl.num_programs(0)
    n_chunks = xf_ref.shape[0] // CHUNK

    @pl.when(first_ref[i] == 1)
    def _():
        s_ref[0] = jnp.zeros_like(s_ref[0])

    @pl.when(last_ref[nb - 1 - i] == 1)
    def _():
        s_ref[1] = jnp.zeros_like(s_ref[1])

    m0 = _lane_half()
    mks = [_Masks(backward=False), _Masks(backward=True)]
    cs = [_cumsum_matrix(backward=False), _cumsum_matrix(backward=True)]
    rcs = [_row_cumsum_matrix(backward=False), _row_cumsum_matrix(backward=True)]
    scratch = (u_ref, wq_ref, kd_ref, p_ref, egl_ref)

    def local(k):
        return lambda tick: _gdn_local(mks, m0, (xf_ref, xb_ref), (gf_ref, gb_ref), (gtf_ref, gtb_ref), gp_ref,
                                       gpt_ref, cs, rcs, _part_chunks(n_chunks, k, GDN_PART_CHUNKS), *scratch,
                                       tick=tick)

    def seq(k):
        return _gdn_seq(mks, m0, _part_chunks(n_chunks, k, GDN_PART_CHUNKS), (of_ref, ob_ref), s_ref, *scratch)

    _run_parts(n_chunks // GDN_PART_CHUNKS, local, seq)


def _rec_specs(nb, widths):
    fwd = lambda w: pl.BlockSpec((REC_BLOCK, w), lambda i, f, l: (i, 0))
    bwd = lambda w: pl.BlockSpec((REC_BLOCK, w), lambda i, f, l: (nb - 1 - i, 0))
    const = lambda r: pl.BlockSpec((r, GATE_W), lambda i, f, l: (0, 0))
    ins = [fwd(widths[0]), bwd(widths[0]), fwd(GATE_W), bwd(GATE_W),
           pl.BlockSpec((GATE_W, REC_BLOCK), lambda i, f, l: (0, i)),
           pl.BlockSpec((GATE_W, REC_BLOCK), lambda i, f, l: (0, nb - 1 - i)),
           const(8), const(GATE_W)]
    return ins, [fwd(widths[1]), bwd(widths[1])]


def _gdn(gqkv, gates, gates_t, gate_params, tables):
    t = gqkv.shape[0]
    nb = t // REC_BLOCK
    nc = REC_BLOCK // CHUNK
    first, last = tables[0], tables[1]
    in_specs, out_specs = _rec_specs(nb, (3 * GDN_W, GDN_W))
    grid_spec = pltpu.PrefetchScalarGridSpec(
        num_scalar_prefetch=2,
        grid=(nb,),
        in_specs=in_specs,
        out_specs=out_specs,
        scratch_shapes=[pltpu.VMEM((2, N_PAIRS, STK, STK), F32),
                        pltpu.VMEM((2, nc, CHUNK, GDN_W), F32),
                        pltpu.VMEM((2, nc, STK, GDN_W), BF16),
                        pltpu.VMEM((2, nc, CHUNK, GDN_W), BF16),
                        pltpu.VMEM((2, nc, N_PAIRS, CHUNK, STK), BF16),
                        pltpu.VMEM((2, nc, 1, GDN_W), F32)],
    )
    return pl.pallas_call(
        _gdn_kernel,
        grid_spec=grid_spec,
        out_shape=[jax.ShapeDtypeStruct((t, GDN_W), BF16)] * 2,
        compiler_params=_params("arbitrary"),
        name="gdn",
    )(first, last, gqkv, gqkv, gates, gates, gates_t, gates_t, gate_params[0], gate_params[1])


def _mlstm_local(mks, m0, bd2, x_refs, g_refs, gt_refs, gp_ref, gpt_ref, cs, rcs, chunk_ids,
                 upd_ref, intra_ref, mi_ref, fc_ref, fl_ref, ml_ref, tick=_no_tick):
    bias_row = gp_ref[0:1, :]
    bias_col = gpt_ref[0:GATE_ROWS, 0:1]
    tiles, rows = {}, {}
    for d in DIRS:
        for j, c in enumerate(chunk_ids[d]):
            li = g_refs[d][_chunk_rows(c), :] + bias_row
            tiles[d, j] = (_split_mm_left(cs[d], -_softplus(-li), 3), li)
            if (d, c // 2) not in rows:
                lit = gt_refs[d][0:GATE_ROWS, _group_lanes(c)] + bias_col
                fct = _split_mm(-_softplus(-lit), rcs[d], 3)
                rows[d, c // 2] = (fct, pltpu.roll(fct, CHUNK, 1), lit, pltpu.roll(lit, CHUNK, 1))
    tick()
    order = [(d, j, p) for j in range(len(chunk_ids[0])) for d in DIRS for p in range(N_PAIRS)]
    ch = [(d, chunk_ids[d][j], p) for d, j, p in order]
    mk = [mks[d] for d, j, p in order]
    col_i = [2 * GATE_GROUP + d * MLSTM_HEADS + 2 * p for d, j, p in order]
    col_f = [3 * GATE_GROUP + d * MLSTM_HEADS + 2 * p for d, j, p in order]
    fc = [tiles[d, j][0] for d, j, p in order]
    li = [tiles[d, j][1] for d, j, p in order]
    e_row = [_pair_row(rows[d, c // 2][0], rows[d, c // 2][1], cf, cf + 1, c)
             - _pair_row(rows[d, c // 2][2], rows[d, c // 2][3], ci, ci + 1, c)
             for (d, c, p), cf, ci in zip(ch, col_f, col_i)]
    q2 = [x_refs[d][_chunk_rows(c), _pair_lanes(p, 0, MLSTM_W)] for d, c, p in ch]
    k2 = [x_refs[d][_chunk_rows(c), _pair_lanes(p, 1, MLSTM_W)] * (HEAD_DIM ** -0.5) for d, c, p in ch]
    v2 = [x_refs[d][_chunk_rows(c), _pair_lanes(p, 2, MLSTM_W)] for d, c, p in ch]
    ones = jnp.ones((CHUNK, PAIR_W), F32)
    ones_st = _stack(ones, m0)

    fc2 = _each(lambda t, c0: _expand(t, c0, c0 + 1, m0), fc, col_f)
    li2 = _each(lambda t, c0: _expand(t, c0, c0 + 1, m0), li, col_i)
    fl = _each(lambda f, m: f[m.last_row:m.last_row + 1, :], fc2, mk)
    g_end = _each(lambda l, f, i_: l - f + i_, fl, fc2, li2)
    ml = _each(lambda g: jnp.max(g, axis=0, keepdims=True), g_end)
    kw2 = _each(lambda k, g, m: k * jnp.exp(g - m), k2, g_end, ml)
    upd = _each(lambda kw, v: jnp.where(bd2, _mm_tn(kw, jnp.concatenate([v, ones], axis=1)), 0.0), kw2, v2)
    tick()
    dmat = _each(lambda f, e, m: jnp.where(m.incl, f - e, NEG), fc2, e_row, mk)
    mi2 = _each(lambda x: jnp.where(m0, jnp.max(jnp.where(m0, x, NEG), axis=1, keepdims=True),
                                    jnp.max(jnp.where(m0, NEG, x), axis=1, keepdims=True)), dmat)
    tick()
    qk = _each(lambda q, k: _mm_nt(q, _stack(k, m0)), q2, k2)
    tick()
    s_loc = _each(lambda x, dm, mi: x * jnp.exp(dm - mi), qk, dmat, mi2)
    tick()
    intra = _each(lambda s, v: _mm(s, jnp.concatenate([_stack(v, m0), ones_st], axis=1)), s_loc, v2)
    tick()
    for i, (d, c, p) in enumerate(ch):
        lanes = _pair_lanes(p)
        upd_ref[d, c, p] = upd[i]
        intra_ref[d, c, p] = intra[i]
        mi_ref[d, c, :, lanes] = mi2[i]
        fc_ref[d, c, :, lanes] = fc2[i]
        fl_ref[d, c, :, lanes] = fl[i]
        ml_ref[d, c, :, lanes] = ml[i]
        if i % N_PAIRS == N_PAIRS - 1:
            tick()


def _mlstm_seq(chunk_ids, x_refs, o_refs, c_ref, m_ref, upd_ref, intra_ref, mi_ref, fc_ref, fl_ref, ml_ref):
    for j, d, p in [(j, d, p) for j in range(len(chunk_ids[0])) for d, p in CHAINS]:
        c = chunk_ids[d][j]
        lanes = _pair_lanes(p)
        m_st = m_ref[d, p]
        cn = c_ref[d, p]
        fl = fl_ref[d, c, :, lanes]
        ml = ml_ref[d, c, :, lanes]
        m_new = jnp.maximum(fl + m_st, ml)
        a_row = jnp.exp(fl + m_st - m_new)
        b_row = jnp.exp(ml - m_new)
        m_inter = fc_ref[d, c, :, lanes] + m_st
        mi2 = mi_ref[d, c, :, lanes]
        m_t = jnp.maximum(m_inter, mi2)
        w_intra = jnp.exp(mi2 - m_t)
        w_inter = jnp.exp(m_inter - m_t)
        inter = _mm(x_refs[d][_chunk_rows(c), _pair_lanes(p, 0, MLSTM_W)], cn)
        intra = intra_ref[d, c, p]
        num = w_intra * intra[:, :PAIR_W] + w_inter * inter[:, :PAIR_W]
        den = w_intra * intra[:, PAIR_W:] + w_inter * inter[:, PAIR_W:]
        out = num / jnp.maximum(jnp.abs(den), jnp.exp(-m_t))
        o_refs[d][_chunk_rows(c), lanes] = out.astype(o_refs[d].dtype)
        c_ref[d, p] = (jnp.concatenate([a_row, a_row], axis=1) * cn
                       + jnp.concatenate([b_row, b_row], axis=1) * upd_ref[d, c, p])
        m_ref[d, p] = m_new
        if p == N_PAIRS - 1:
            yield


def _mlstm_kernel(first_ref, last_ref, xf_ref, xb_ref, gf_ref, gb_ref, gtf_ref, gtb_ref, gp_ref, gpt_ref,
                  of_ref, ob_ref, c_ref, m_ref, upd_ref, intra_ref, mi_ref, fc_ref, fl_ref, ml_ref):
    i = pl.program_id(0)
    nb = pl.num_programs(0)
    n_chunks = xf_ref.shape[0] // CHUNK

    @pl.when(first_ref[i] == 1)
    def _():
        c_ref[0] = jnp.zeros_like(c_ref[0])
        m_ref[0] = jnp.full_like(m_ref[0], M_INIT)

    @pl.when(last_ref[nb - 1 - i] == 1)
    def _():
        c_ref[1] = jnp.zeros_like(c_ref[1])
        m_ref[1] = jnp.full_like(m_ref[1], M_INIT)

    m0 = _lane_half()
    r = lax.broadcasted_iota(jnp.int32, (STK, 2 * STK), 0) // HEAD_DIM
    c = (lax.broadcasted_iota(jnp.int32, (STK, 2 * STK), 1) % STK) // HEAD_DIM
    bd2 = r == c
    mks = [_Masks(backward=False), _Masks(backward=True)]
    cs = [_cumsum_matrix(backward=False), _cumsum_matrix(backward=True)]
    rcs = [_row_cumsum_matrix(backward=False), _row_cumsum_matrix(backward=True)]
    scratch = (upd_ref, intra_ref, mi_ref, fc_ref, fl_ref, ml_ref)

    def local(k):
        return lambda tick: _mlstm_local(mks, m0, bd2, (xf_ref, xb_ref), (gf_ref, gb_ref), (gtf_ref, gtb_ref),
                                         gp_ref, gpt_ref, cs, rcs, _part_chunks(n_chunks, k, MLSTM_PART_CHUNKS),
                                         *scratch, tick=tick)

    def seq(k):
        return _mlstm_seq(_part_chunks(n_chunks, k, MLSTM_PART_CHUNKS), (xf_ref, xb_ref), (of_ref, ob_ref), c_ref,
                          m_ref, *scratch)

    _run_parts(n_chunks // MLSTM_PART_CHUNKS, local, seq)


def _mlstm(ml_qkv, gates, gates_t, gate_params, tables):
    t = ml_qkv.shape[0]
    nb = t // REC_BLOCK
    nc = REC_BLOCK // CHUNK
    first, last = tables[0], tables[1]
    in_specs, out_specs = _rec_specs(nb, (3 * MLSTM_W, MLSTM_W))
    grid_spec = pltpu.PrefetchScalarGridSpec(
        num_scalar_prefetch=2,
        grid=(nb,),
        in_specs=in_specs,
        out_specs=out_specs,
        scratch_shapes=[pltpu.VMEM((2, N_PAIRS, STK, 2 * STK), F32),
                        pltpu.VMEM((2, N_PAIRS, 1, PAIR_W), F32),
                        pltpu.VMEM((2, nc, N_PAIRS, STK, 2 * STK), F32),
                        pltpu.VMEM((2, nc, N_PAIRS, CHUNK, 2 * PAIR_W), F32),
                        pltpu.VMEM((2, nc, CHUNK, MLSTM_W), F32),
                        pltpu.VMEM((2, nc, CHUNK, MLSTM_W), F32),
                        pltpu.VMEM((2, nc, 1, MLSTM_W), F32),
                        pltpu.VMEM((2, nc, 1, MLSTM_W), F32)],
    )
    return pl.pallas_call(
        _mlstm_kernel,
        grid_spec=grid_spec,
        out_shape=[jax.ShapeDtypeStruct((t, MLSTM_W), BF16)] * 2,
        compiler_params=_params("arbitrary"),
        name="mlstm",
    )(first, last, ml_qkv, ml_qkv, gates, gates, gates_t, gates_t, gate_params[0], gate_params[1])


OUT_TM = 512


def _outproj_kernel(x_ref, na_ref, gf_ref, gb_ref, z_ref, hf_ref, hb_ref, o_ref, gn_ref, mn_ref,
                    w_ref, y_ref):
    def head_rms(v, gain):
        ms = _head_sumsq(v) * (1.0 / HEAD_DIM)
        return v * lax.rsqrt(ms + EPS) * gain

    go = head_rms(gf_ref[...].astype(F32) + gb_ref[...].astype(F32), gn_ref[...])
    z = z_ref[...].astype(F32)
    gdn_out = go * (z * _sigmoid(z))
    mh = head_rms(hf_ref[...].astype(F32) + hb_ref[...].astype(F32), mn_ref[...])
    ml_out = _sigmoid(o_ref[...].astype(F32)) * mh
    y = x_ref[...]
    y = y + jnp.dot(na_ref[...], w_ref[0:NA_W, :], preferred_element_type=F32)
    y = y + jnp.dot(_bf(gdn_out), w_ref[NA_W:NA_W + GDN_W, :], preferred_element_type=F32)
    y = y + jnp.dot(_bf(ml_out), w_ref[NA_W + GDN_W:, :], preferred_element_type=F32)
    y_ref[...] = y


def _outproj(x, na_out, gf, gb, z, hf, hb, o, gdn_norm, mlstm_norm, w_out):
    t = x.shape[0]
    tm = min(OUT_TM, t)
    tok = lambda w: pl.BlockSpec((tm, w), lambda i: (i, 0))
    const = lambda r, w: pl.BlockSpec((r, w), lambda i: (0, 0))
    gn = jnp.tile(gdn_norm.astype(F32), GDN_HEADS).reshape(1, GDN_W)
    mn = mlstm_norm.astype(F32).reshape(1, MLSTM_W)
    return pl.pallas_call(
        _outproj_kernel,
        grid=(t // tm,),
        in_specs=[tok(D_MODEL), tok(NA_W), tok(GDN_W), tok(GDN_W), tok(GDN_W), tok(MLSTM_W), tok(MLSTM_W),
                  tok(MLSTM_W), const(1, GDN_W), const(1, MLSTM_W), const(D_MODEL, D_MODEL)],
        out_specs=tok(D_MODEL),
        out_shape=jax.ShapeDtypeStruct((t, D_MODEL), F32),
        compiler_params=_params("parallel"),
        name="outproj",
    )(x, na_out, gf, gb, z, hf, hb, o, gn, mn, _bf(w_out))


def _gate_params(bias_rows, a_log):
    pad = GATE_W - 4 * GATE_GROUP
    row0 = jnp.concatenate([b.astype(F32).reshape(-1) for b in bias_rows] + [jnp.zeros((pad,), F32)])
    row1 = jnp.concatenate([a_log.astype(F32).reshape(-1), jnp.zeros((GATE_W - GATE_GROUP,), F32)])
    rows = jnp.concatenate([row0[None], row1[None], jnp.zeros((6, GATE_W), F32)], axis=0)
    cols = jnp.concatenate([rows.T, jnp.zeros((GATE_W, GATE_W - 8), F32)], axis=1)
    return rows, cols


def _trunk(xs, seq_lens, p):
    depth = p["ffn1_norm"].shape[0]
    tables = tuple(jnp.asarray(a) for a in _block_tables(seq_lens, MIX_BLOCK))
    rec_tables = tuple(jnp.asarray(a) for a in _block_tables(seq_lens, REC_BLOCK))
    zeros_g = jnp.zeros((2, GDN_HEADS), F32)
    out_sizes = [x.shape[0] for x in xs]
    x = None
    for i in range(depth):
        x = _ffn(xs if i == 0 else [x], p["ffn1_norm"][i], _bf(p["ffn1_w_in"][i]), _bf(p["ffn1_w_out"][i]),
                 p["final_norm"], False)
        na_qkv, gqkv, gdn_z, ml_qkv, ml_o, gates = _inproj(x, p["mix_norm"][i], _permute_w_in(p["w_in"][i]),
                                                           p["gdn_conv"][i], tables)
        na_out = _na(na_qkv, _na_bias_table(p["na_rpb"][i]), tables)
        gdn_gp = _gate_params([p["gdn_dt_bias"][i], zeros_g, zeros_g, zeros_g], p["gdn_a_log"][i])
        gates_t = gates.T
        gf, gb = _gdn(gqkv, gates, gates_t, gdn_gp, rec_tables)
        ml_gp = _gate_params([zeros_g, zeros_g, p["mlstm_i_bias"][i], p["mlstm_f_bias"][i]], zeros_g)
        hf, hb = _mlstm(ml_qkv, gates, gates_t, ml_gp, rec_tables)
        x = _outproj(x, na_out, gf, gb, gdn_z, hf, hb, ml_o, p["gdn_norm"][i], p["mlstm_norm"][i], p["w_out"][i])
        last = i == depth - 1
        x = _ffn([x], p["ffn2_norm"][i], _bf(p["ffn2_w_in"][i]), _bf(p["ffn2_w_out"][i]), p["final_norm"],
                 last, out_sizes if last else None)
    return x if len(out_sizes) > 1 else [x]


def kernel(x_prompt, x_sample, ffn1_norm, ffn1_w_in, ffn1_w_out, mix_norm, w_in, w_out, na_rpb, gdn_conv,
           gdn_a_log, gdn_dt_bias, gdn_norm, mlstm_i_bias, mlstm_f_bias, mlstm_norm, ffn2_norm, ffn2_w_in,
           ffn2_w_out, final_norm):
    p = dict(ffn1_norm=ffn1_norm, ffn1_w_in=ffn1_w_in, ffn1_w_out=ffn1_w_out, mix_norm=mix_norm, w_in=w_in,
             w_out=w_out, na_rpb=na_rpb, gdn_conv=gdn_conv, gdn_a_log=gdn_a_log, gdn_dt_bias=gdn_dt_bias,
             gdn_norm=gdn_norm, mlstm_i_bias=mlstm_i_bias, mlstm_f_bias=mlstm_f_bias, mlstm_norm=mlstm_norm,
             ffn2_norm=ffn2_norm, ffn2_w_in=ffn2_w_in, ffn2_w_out=ffn2_w_out, final_norm=final_norm)
    bp, lp, _ = x_prompt.shape
    bs, ls, _ = x_sample.shape
    y_prompt, y_sample = _trunk([x_prompt.reshape(bp * lp, D_MODEL), x_sample.reshape(bs * ls, D_MODEL)],
                                (lp,) * bp + (ls,) * bs, p)
    return (y_prompt.reshape(bp, lp, D_MODEL), y_sample.reshape(bs, ls, D_MODEL))
```

```python
import functools

import jax
import jax.numpy as jnp
import numpy as np
from jax import lax
from jax.experimental import pallas as pl
from jax.experimental.pallas import tpu as pltpu

D_MODEL = 1024
D_FF = 2816
HEAD_DIM = 64
NA_HEADS = 4
GDN_HEADS = 6
MLSTM_HEADS = 6
NA_W = NA_HEADS * HEAD_DIM
GDN_W = GDN_HEADS * HEAD_DIM
MLSTM_W = MLSTM_HEADS * HEAD_DIM
GRID_W = 64
NA_WIN_ROWS = 8
NA_WIN_COLS = 16
GDN_CONV = 5
CHUNK = 64
M_INIT = -1e30
EPS = 1e-6
NEG = -1e30

LANES = 128
PAIR_W = 2 * HEAD_DIM
N_PAIRS = GDN_HEADS // 2
GATE_W = LANES
GATE_GROUP = 2 * GDN_HEADS
MIX_BLOCK = 512
REC_BLOCK = 1024
GDN_PART_CHUNKS = 4
MLSTM_PART_CHUNKS = 16
VMEM_LIMIT = 56 * 1024 * 1024

F32 = jnp.float32
BF16 = jnp.bfloat16


def _bf(x):
    return x.astype(BF16)


def _mm(a, b):
    return jnp.dot(_bf(a), _bf(b), preferred_element_type=F32)


def _mm_nt(a, b):
    return lax.dot_general(_bf(a), _bf(b), (((1,), (1,)), ((), ())), preferred_element_type=F32)


def _mm_tn(a, b):
    return lax.dot_general(_bf(a), _bf(b), (((0,), (0,)), ((), ())), preferred_element_type=F32)


def _split_mm(a, b_exact, parts):
    acc = None
    rem = a
    for _ in range(parts):
        piece = _bf(rem)
        term = jnp.dot(piece, b_exact, preferred_element_type=F32)
        acc = term if acc is None else acc + term
        rem = rem - piece.astype(F32)
    return acc


def _split_mm_left(a_exact, b, parts):
    acc = None
    rem = b
    for _ in range(parts):
        piece = _bf(rem)
        term = jnp.dot(a_exact, piece, preferred_element_type=F32)
        acc = term if acc is None else acc + term
        rem = rem - piece.astype(F32)
    return acc


def _sigmoid(x):
    return 1.0 / (1.0 + jnp.exp(-x))


def _softplus(x):
    return jnp.maximum(x, 0.0) + jnp.log(1.0 + jnp.exp(-jnp.abs(x)))


def _rms(x, gain):
    return x * lax.rsqrt(jnp.mean(x * x, axis=-1, keepdims=True) + EPS) * gain


def _params(*sem):
    return pltpu.CompilerParams(dimension_semantics=sem, vmem_limit_bytes=VMEM_LIMIT)


FFN_TM = 1024
FFN_SUB = 512
FFN_TF = 256


def _ffn_kernel(*refs, final, sub, in_blocks, out_blocks):
    n_in, n_out = len(in_blocks), len(out_blocks)
    x_refs = refs[:n_in]
    g_ref, wi_ref, wo_ref, fn_ref = refs[n_in:n_in + 4]
    o_refs = refs[n_in + 4:n_in + 4 + n_out]
    act_ref = refs[-1]
    i = pl.program_id(0)
    in_starts = np.cumsum((0,) + tuple(in_blocks))
    out_starts = np.cumsum((0,) + tuple(out_blocks))
    for s in range(x_refs[0].shape[0] // sub):
        rows = slice(s * sub, (s + 1) * sub)
        x = x_refs[-1][rows, :]
        for k in range(n_in - 2, -1, -1):
            x = jnp.where(i < in_starts[k + 1], x_refs[k][rows, :], x)
        xn = _bf(_rms(x, g_ref[...]))
        for j in range(D_FF // FFN_TF):
            gate = jnp.dot(xn, wi_ref[:, j * FFN_TF:(j + 1) * FFN_TF], preferred_element_type=F32)
            up = jnp.dot(xn, wi_ref[:, D_FF + j * FFN_TF:D_FF + (j + 1) * FFN_TF], preferred_element_type=F32)
            act_ref[rows, j * FFN_TF:(j + 1) * FFN_TF] = _bf(gate * _sigmoid(gate) * up)
        y = x + 0.5 * jnp.dot(act_ref[rows, :], wo_ref[...], preferred_element_type=F32)
        if final:
            y = _rms(y, fn_ref[...])
        if n_out == 1:
            o_refs[0][rows, :] = y
        else:
            for k in range(n_out):
                @pl.when(jnp.logical_and(i >= out_starts[k], i < out_starts[k + 1]))
                def _(k=k, y=y):
                    o_refs[k][rows, :] = y


def _ffn(xs, gain, w_in, w_out, final_gain, final, out_sizes=None):
    sizes = [x.shape[0] for x in xs]
    t = sum(sizes)
    out_sizes = [t] if out_sizes is None else list(out_sizes)
    tm = min([FFN_TM] + sizes + out_sizes)
    sub = min(FFN_SUB, tm)
    in_blocks = [s // tm for s in sizes]
    out_blocks = [s // tm for s in out_sizes]

    def part_spec(blocks, k):
        start = int(np.sum(blocks[:k]))
        return pl.BlockSpec((tm, D_MODEL), lambda i: (jnp.clip(i - start, 0, blocks[k] - 1), 0))

    resident = lambda shape: pl.BlockSpec(shape, lambda i: (0, 0), pipeline_mode=pl.Buffered(1))
    outs = pl.pallas_call(
        functools.partial(_ffn_kernel, final=final, sub=sub, in_blocks=tuple(in_blocks),
                          out_blocks=tuple(out_blocks)),
        grid=(t // tm,),
        in_specs=[part_spec(in_blocks, k) for k in range(len(xs))] + [
            resident((1, D_MODEL)),
            resident((D_MODEL, 2 * D_FF)),
            resident((D_FF, D_MODEL)),
            resident((1, D_MODEL)),
        ],
        out_specs=[part_spec(out_blocks, k) for k in range(len(out_sizes))],
        out_shape=[jax.ShapeDtypeStruct((s, D_MODEL), F32) for s in out_sizes],
        scratch_shapes=[pltpu.VMEM((tm, D_FF), BF16)],
        compiler_params=_params("arbitrary"),
        name="ffn",
    )(*xs, gain.reshape(1, D_MODEL), w_in, w_out, final_gain.reshape(1, D_MODEL))
    return outs[0] if len(outs) == 1 else outs


PROJ_TM = 512
PROJ_WIDTHS = (3 * NA_W, 3 * GDN_W, GDN_W, 3 * MLSTM_W, MLSTM_W, GATE_W)
PROJ_DTYPES = (BF16, F32, BF16, F32, BF16, F32)
P_PERM = sum(PROJ_WIDTHS)


def _permute_w_in(w_in):
    o = np.cumsum([0, 3 * NA_W, 3 * GDN_W, GDN_W, 2 * GDN_HEADS, 2 * GDN_HEADS,
                   3 * MLSTM_W, MLSTM_W, 2 * MLSTM_HEADS, 2 * MLSTM_HEADS])
    n_gates = 4 * 2 * GDN_HEADS
    cols = [w_in[:, o[0]:o[3]], w_in[:, o[5]:o[7]], w_in[:, o[3]:o[5]], w_in[:, o[7]:o[9]],
            jnp.zeros((D_MODEL, GATE_W - n_gates), w_in.dtype)]
    return _bf(jnp.concatenate(cols, axis=1))


PROJ_DOT_GROUPS = ((1, 2), (0,), (3, 4), (5,))
PROJ_OFFSETS = tuple(int(v) for v in np.cumsum((0,) + PROJ_WIDTHS[:-1]))
PROJ_HALO = 16


def _head_sumsq(x):
    r = lax.broadcasted_iota(jnp.int32, (PAIR_W, PAIR_W), 0) // HEAD_DIM
    c = lax.broadcasted_iota(jnp.int32, (PAIR_W, PAIR_W), 1) // HEAD_DIM
    ones_bd = jnp.where(r == c, 1.0, 0.0).astype(BF16)
    sq = x * x
    parts = [_split_mm(sq[:, g:g + PAIR_W], ones_bd, 2) for g in range(0, x.shape[1], PAIR_W)]
    return jnp.concatenate(parts, axis=1)


def _gdn_preprocess(y, w_ref, tb):
    n = y.shape[0]
    acc = None
    for j in range(GDN_CONV):
        shift = (GDN_CONV // 2 - j) % n
        tap = y if shift == 0 else pltpu.roll(y, shift, 0)
        term = tap[PROJ_HALO:PROJ_HALO + tb] * w_ref[j:j + 1, :]
        acc = term if acc is None else acc + term
    act = acc * _sigmoid(acc)
    q = act[:, 0:GDN_W]
    k = act[:, GDN_W:2 * GDN_W]
    qn = q * lax.rsqrt(_head_sumsq(q) + EPS) * (HEAD_DIM ** -0.5)
    kn = k * lax.rsqrt(_head_sumsq(k) + EPS)
    return qn, kn, act[:, 2 * GDN_W:3 * GDN_W]


def _inproj_kernel(first_ref, last_ref, x_ref, xp_ref, xn_ref, g_ref, w_ref, cw_ref, *o_refs):
    i = pl.program_id(0)
    tb = x_ref.shape[0]
    prev = jnp.where(first_ref[i] == 1, 0.0, xp_ref[...])
    nxt = jnp.where(last_ref[i] == 1, 0.0, xn_ref[...])
    xe = jnp.concatenate([prev, x_ref[...], nxt], axis=0)
    xn = _bf(_rms(xe, g_ref[...]))
    centre = slice(PROJ_HALO, PROJ_HALO + tb)
    xn_centre = xn[centre]
    gdn_part = None
    for group in PROJ_DOT_GROUPS:
        off = PROJ_OFFSETS[group[0]]
        width = sum(PROJ_WIDTHS[k] for k in group)
        with_halo = 1 in group
        res = jnp.dot(xn if with_halo else xn_centre, w_ref[:, off:off + width], preferred_element_type=F32)
        col = 0
        for k in group:
            part = res[:, col:col + PROJ_WIDTHS[k]]
            col += PROJ_WIDTHS[k]
            if k == 1:
                gdn_part = part
            else:
                o_refs[k][...] = (part[centre] if with_halo else part).astype(o_refs[k].dtype)
    qn, kn, v = _gdn_preprocess(gdn_part, cw_ref, tb)
    o_refs[1][:, 0:GDN_W] = qn
    o_refs[1][:, GDN_W:2 * GDN_W] = kn
    o_refs[1][:, 2 * GDN_W:3 * GDN_W] = v


def _inproj(x, gain, w_perm, conv_w, tables):
    t = x.shape[0]
    tm = MIX_BLOCK
    per = tm // PROJ_HALO
    n_halo = t // PROJ_HALO
    first, last = tables[0], tables[1]
    const = lambda shape: pl.BlockSpec(shape, lambda i, f, l: (0, 0))
    grid_spec = pltpu.PrefetchScalarGridSpec(
        num_scalar_prefetch=2,
        grid=(t // tm,),
        in_specs=[
            pl.BlockSpec((tm, D_MODEL), lambda i, f, l: (i, 0)),
            pl.BlockSpec((PROJ_HALO, D_MODEL), lambda i, f, l: (jnp.maximum(i * per - 1, 0), 0)),
            pl.BlockSpec((PROJ_HALO, D_MODEL), lambda i, f, l: (jnp.minimum((i + 1) * per, n_halo - 1), 0)),
            const((1, D_MODEL)),
            const((D_MODEL, P_PERM)),
            const((8, 3 * GDN_W)),
        ],
        out_specs=[pl.BlockSpec((tm, w), lambda i, f, l: (i, 0)) for w in PROJ_WIDTHS],
    )
    w8 = jnp.concatenate([conv_w.astype(F32), jnp.zeros((8 - GDN_CONV, 3 * GDN_W), F32)], axis=0)
    return pl.pallas_call(
        _inproj_kernel,
        grid_spec=grid_spec,
        out_shape=[jax.ShapeDtypeStruct((t, w), d) for w, d in zip(PROJ_WIDTHS, PROJ_DTYPES)],
        compiler_params=_params("arbitrary"),
        name="inproj",
    )(first, last, x, x, x, gain.reshape(1, D_MODEL), w_perm, w8)


def _block_tables(seq_lens, block):
    first, last = [], []
    for n in seq_lens:
        nb = n // block
        first += [1] + [0] * (nb - 1)
        last += [0] * (nb - 1) + [1]
    first = np.asarray(first, np.int32)
    last = np.asarray(last, np.int32)
    idx = np.arange(len(first), dtype=np.int32)
    prev = np.where(first == 1, idx, idx - 1).astype(np.int32)
    nxt = np.where(last == 1, idx, idx + 1).astype(np.int32)
    return first, last, prev, nxt


NA_ROWS = MIX_BLOCK // GRID_W
NA_KEYS = NA_WIN_ROWS * GRID_W
NA_ROW_GROUP = 4


def _na_bias_table(rpb):
    c = np.arange(GRID_W)
    c_start = np.clip(c - NA_WIN_COLS // 2, 0, GRID_W - NA_WIN_COLS)
    in_win = (c[None, :] >= c_start[:, None]) & (c[None, :] < c_start[:, None] + NA_WIN_COLS)
    dc = np.clip(c[None, :] - c[:, None], 1 - NA_WIN_COLS, NA_WIN_COLS - 1) + NA_WIN_COLS - 1
    one_hot = jnp.asarray(dc[None] == np.arange(2 * NA_WIN_COLS - 1)[:, None, None], F32)
    by_col = jnp.einsum("hrc,cqk->hrqk", rpb.astype(F32), one_hot, precision=lax.Precision.HIGHEST)
    tab = jnp.stack([by_col[:, NA_WIN_ROWS - 1 - v:2 * NA_WIN_ROWS - 1 - v] for v in range(NA_WIN_ROWS)],
                    axis=1)
    tab = jnp.where(in_win[None, None, None], tab, NEG).transpose(0, 1, 3, 2, 4)
    return tab.reshape(NA_HEADS, NA_WIN_ROWS, GRID_W, NA_KEYS)


def _na_kernel(first_ref, last_ref, prev_ref, next_ref,
               q_ref, kp_ref, kc_ref, kn_ref, vp_ref, vc_ref, vn_ref, bias_ref, o_ref, kbuf, vbuf):
    del prev_ref, next_ref
    g = pl.program_id(0)
    kbuf[0:MIX_BLOCK] = kp_ref[...]
    kbuf[MIX_BLOCK:2 * MIX_BLOCK] = kc_ref[...]
    kbuf[2 * MIX_BLOCK:3 * MIX_BLOCK] = kn_ref[...]
    vbuf[0:MIX_BLOCK] = vp_ref[...]
    vbuf[MIX_BLOCK:2 * MIX_BLOCK] = vc_ref[...]
    vbuf[2 * MIX_BLOCK:3 * MIX_BLOCK] = vn_ref[...]
    scale = jnp.asarray(HEAD_DIM ** -0.5, BF16)
    half = NA_WIN_ROWS // 2
    m0 = lax.broadcasted_iota(jnp.int32, (GRID_W, PAIR_W), 1) < HEAD_DIM
    for a0 in range(0, NA_ROWS, NA_ROW_GROUP):
        q, k, v, bias = [], [], [], []
        for a in range(a0, a0 + NA_ROW_GROUP):
            edge = (first_ref[g] if a < half else last_ref[g]) == 1
            start = pl.multiple_of(jnp.where(edge, NA_ROWS, a + half) * GRID_W, GRID_W)
            variant = jnp.where(edge, a, half)
            for h in range(NA_HEADS):
                lanes = slice((h // 2) * PAIR_W, (h // 2 + 1) * PAIR_W)
                q2 = q_ref[a * GRID_W:(a + 1) * GRID_W, lanes] * scale
                q.append(jnp.where(m0 if h % 2 == 0 else jnp.logical_not(m0), q2, jnp.zeros_like(q2)))
                k.append(kbuf[pl.ds(start, NA_KEYS), lanes])
                v.append(vbuf[pl.ds(start, NA_KEYS), lanes])
                bias.append(bias_ref[h, variant])
        s = _each(lambda q_, k_, b_: _mm_nt(q_, k_) + b_, q, k, bias)
        p = _each(lambda s_: jnp.exp(s_ - jnp.max(s_, axis=-1, keepdims=True)), s)
        l = _each(lambda p_: jnp.sum(p_, axis=-1, keepdims=True), p)
        o = _each(lambda p_, v_, l_: _mm(p_, v_) / l_, p, v, l)
        for j in range(0, len(o), 2):
            a = a0 + j // NA_HEADS
            lanes = slice(((j % NA_HEADS) // 2) * PAIR_W, ((j % NA_HEADS) // 2 + 1) * PAIR_W)
            o_ref[a * GRID_W:(a + 1) * GRID_W, lanes] = jnp.where(m0, o[j], o[j + 1]).astype(o_ref.dtype)


def _na(na_qkv, bias, tables):
    t = na_qkv.shape[0]
    nb = t // MIX_BLOCK
    blk = (MIX_BLOCK, NA_W)
    cur = lambda col: pl.BlockSpec(blk, lambda g, f, l, p, n: (g, col))
    prv = lambda col: pl.BlockSpec(blk, lambda g, f, l, p, n: (p[g], col))
    nxt = lambda col: pl.BlockSpec(blk, lambda g, f, l, p, n: (n[g], col))
    grid_spec = pltpu.PrefetchScalarGridSpec(
        num_scalar_prefetch=4,
        grid=(nb,),
        in_specs=[cur(0), prv(1), cur(1), nxt(1), prv(2), cur(2), nxt(2),
                  pl.BlockSpec((NA_HEADS, NA_WIN_ROWS, GRID_W, NA_KEYS), lambda g, f, l, p, n: (0, 0, 0, 0))],
        out_specs=pl.BlockSpec(blk, lambda g, f, l, p, n: (g, 0)),
        scratch_shapes=[pltpu.VMEM((3 * MIX_BLOCK, NA_W), BF16), pltpu.VMEM((3 * MIX_BLOCK, NA_W), BF16)],
    )
    return pl.pallas_call(
        _na_kernel,
        grid_spec=grid_spec,
        out_shape=jax.ShapeDtypeStruct((t, NA_W), BF16),
        compiler_params=_params("arbitrary"),
        name="na",
    )(*tables, na_qkv, na_qkv, na_qkv, na_qkv, na_qkv, na_qkv, na_qkv, bias)


STK = 2 * CHUNK
GATE_ROWS = 4 * GATE_GROUP


class _Masks:
    def __init__(self, backward):
        r = lax.broadcasted_iota(jnp.int32, (CHUNK, STK), 0)
        c = lax.broadcasted_iota(jnp.int32, (CHUNK, STK), 1) % CHUNK
        self.incl = (c >= r) if backward else (c <= r)
        self.strict = (c > r) if backward else (c < r)
        self.blk16 = (r // 16) == (c // 16)
        self.eye = jnp.where(r == c, 1.0, 0.0).astype(F32)
        rr = lax.broadcasted_iota(jnp.int32, (STK, STK), 0) // CHUNK
        cc = lax.broadcasted_iota(jnp.int32, (STK, STK), 1) // CHUNK
        self.same = rr == cc
        self.last_row = 0 if backward else CHUNK - 1


def _cumsum_matrix(backward):
    r = lax.broadcasted_iota(jnp.int32, (CHUNK, CHUNK), 0)
    c = lax.broadcasted_iota(jnp.int32, (CHUNK, CHUNK), 1)
    keep = (c >= r) if backward else (c <= r)
    return jnp.where(keep, 1.0, 0.0).astype(BF16)


def _row_cumsum_matrix(backward):
    r = lax.broadcasted_iota(jnp.int32, (STK, STK), 0)
    c = lax.broadcasted_iota(jnp.int32, (STK, STK), 1)
    keep = jnp.logical_and(r // CHUNK == c // CHUNK, (r >= c) if backward else (r <= c))
    return jnp.where(keep, 1.0, 0.0).astype(BF16)


def _lane_half():
    return lax.broadcasted_iota(jnp.int32, (CHUNK, PAIR_W), 1) < HEAD_DIM


def _expand(tile, c0, c1, m0):
    return jnp.where(m0, tile[:, c0:c0 + 1], tile[:, c1:c1 + 1])


def _pair_row(tile, rolled, r0, r1, chunk):
    low = lax.broadcasted_iota(jnp.int32, (1, STK), 1) < CHUNK
    if chunk % 2 == 0:
        return jnp.where(low, tile[r0:r0 + 1, :], rolled[r1:r1 + 1, :])
    return jnp.where(low, rolled[r0:r0 + 1, :], tile[r1:r1 + 1, :])


def _stack(x, m0):
    return jnp.concatenate([jnp.where(m0, x, 0.0), jnp.where(m0, 0.0, x)], axis=0)


def _each(fn, *lists):
    return [fn(*args) for args in zip(*lists)]


def _no_tick():
    pass


def _unit_tri_solve(a_list, mk_list, rhs_list, m0, tick=_no_tick):
    eye = mk_list[0].eye
    pmm = lambda x, y: _mm(x, _stack(y, m0))
    d = _each(lambda a, mk: jnp.where(mk.blk16, a, 0.0), a_list, mk_list)
    rest = _each(lambda a, x: a - x, a_list, d)
    d2 = _each(lambda x: pmm(x, x), d)
    tick()
    d4 = _each(lambda x: pmm(x, x), d2)
    td = _each(lambda x, y: pmm(eye - x, eye + y), d, d2)
    tick()
    d8 = _each(lambda x: pmm(x, x), d4)
    td = _each(lambda x, y: pmm(x, eye + y), td, d4)
    tick()
    td = _each(lambda x, y: pmm(x, eye + y), td, d8)
    tick()
    n = _each(pmm, td, rest)
    tick()
    n2 = _each(lambda x: pmm(x, x), n)
    tick()
    t = _each(lambda x, y: pmm(eye - x, eye + y), n, n2)
    tick()
    t = _each(pmm, t, td)
    tick()
    return _each(_mm, t, rhs_list)


DIRS = (0, 1)
CHAINS = [(d, p) for d in DIRS for p in range(N_PAIRS)]


def _pair_lanes(p, group=0, width=GDN_W):
    return slice(group * width + p * PAIR_W, group * width + (p + 1) * PAIR_W)


def _chunk_rows(c):
    return slice(c * CHUNK, (c + 1) * CHUNK)


def _group_lanes(c):
    return slice((c // 2) * STK, (c // 2 + 1) * STK)


def _interleaved(local_fn, seq_gen):
    if seq_gen is None:
        local_fn(_no_tick)
        return
    local_fn(lambda: next(seq_gen, None))
    for _ in seq_gen:
        pass


def _part_chunks(n_chunks, part, m):
    fwd = list(range(part * m, (part + 1) * m))
    bwd = list(range(n_chunks - 1 - part * m, n_chunks - 1 - (part + 1) * m, -1))
    return (fwd, bwd)


def _run_parts(n_parts, local, seq):
    _interleaved(local(0), None)
    for k in range(1, n_parts):
        _interleaved(local(k), seq(k - 1))
    for _ in seq(n_parts - 1):
        pass


def _gdn_local(mks, m0, x_refs, g_refs, gt_refs, gp_ref, gpt_ref, cs, rcs, chunk_ids,
               u_ref, wq_ref, kd_ref, p_ref, egl_ref, tick=_no_tick):
    bias_row = gp_ref[0:1, :]
    a_row = -jnp.exp(gp_ref[1:2, :])
    bias_col = gpt_ref[0:GATE_ROWS, 0:1]
    a_col = -jnp.exp(gpt_ref[0:GATE_ROWS, 1:2])
    tiles, rows = {}, {}
    for d in DIRS:
        for j, c in enumerate(chunk_ids[d]):
            gates = g_refs[d][_chunk_rows(c), :]
            log_decay = a_row * _softplus(gates + bias_row)
            tiles[d, j] = (_split_mm_left(cs[d], log_decay, 3), _sigmoid(gates))
            if (d, c // 2) not in rows:
                gt = gt_refs[d][0:GATE_ROWS, _group_lanes(c)]
                gct = _split_mm(a_col * _softplus(gt + bias_col), rcs[d], 3)
                rows[d, c // 2] = (gct, pltpu.roll(gct, CHUNK, 1))
    tick()
    order = [(d, j, p) for j in range(len(chunk_ids[0])) for d in DIRS for p in range(N_PAIRS)]
    ch = [(d, chunk_ids[d][j], p) for d, j, p in order]
    mk = [mks[d] for d, j, p in order]
    col = [d * GDN_HEADS + 2 * p for d, j, p in order]
    gc = [tiles[d, j][0] for d, j, p in order]
    beta = [tiles[d, j][1] for d, j, p in order]
    gc_row = [_pair_row(*rows[d, c // 2], c0, c0 + 1, c) for (d, c, p), c0 in zip(ch, col)]
    q2 = [x_refs[d][_chunk_rows(c), _pair_lanes(p, 0)] for d, c, p in ch]
    k2 = [x_refs[d][_chunk_rows(c), _pair_lanes(p, 1)] for d, c, p in ch]
    v2 = [x_refs[d][_chunk_rows(c), _pair_lanes(p, 2)] for d, c, p in ch]

    gc2 = _each(lambda t, c0: _expand(t, c0, c0 + 1, m0), gc, col)
    b2 = _each(lambda t, c0: _expand(t, GATE_GROUP + c0, GATE_GROUP + c0 + 1, m0), beta, col)
    gl = _each(lambda g, m: g[m.last_row:m.last_row + 1, :], gc2, mk)
    eg2 = _each(jnp.exp, gc2)
    kb2 = _each(lambda k, b: k * b, k2, b2)
    vb2 = _each(lambda v, b: v * b, v2, b2)
    kbg2 = _each(lambda k, e: k * e, kb2, eg2)
    qg2 = _each(lambda q, e: q * e, q2, eg2)
    kdec2 = _each(lambda k, l, g: k * jnp.exp(l - g), k2, gl, gc2)
    decay = _each(lambda g, r, m: jnp.exp(jnp.where(m.incl, g - r, NEG)), gc2, gc_row, mk)
    aq = _each(lambda kb, q, k: _mm_nt(jnp.concatenate([kb, q], axis=0), _stack(k, m0)), kb2, q2, k2)
    tick()
    a = _each(lambda x, dc, m: jnp.where(m.strict, x[:CHUNK] * dc, 0.0), aq, decay, mk)
    pm = _each(lambda x, dc: x[CHUNK:] * dc, aq, decay)
    rhs = _each(lambda vb, kbg: jnp.concatenate([_stack(vb, m0), _stack(kbg, m0)], axis=1), vb2, kbg2)
    uw = _unit_tri_solve(a, mk, rhs, m0, tick)
    tick()
    for i, (d, c, p) in enumerate(ch):
        lanes = _pair_lanes(p)
        u_ref[d, c, :, lanes] = uw[i][:, :PAIR_W]
        wq_ref[d, c, 0:CHUNK, lanes] = _bf(uw[i][:, PAIR_W:])
        wq_ref[d, c, CHUNK:STK, lanes] = _bf(qg2[i])
        kd_ref[d, c, :, lanes] = _bf(kdec2[i])
        p_ref[d, c, p] = _bf(pm[i])
        egl_ref[d, c, :, lanes] = jnp.exp(gl[i])


def _gdn_seq(mks, m0, chunk_ids, o_refs, s_ref, u_ref, wq_ref, kd_ref, p_ref, egl_ref):
    for j in range(len(chunk_ids[0])):
        cidx = [chunk_ids[d][j] for d in DIRS]
        s = [s_ref[d, p] for d, p in CHAINS]
        wq = [_mm(wq_ref[d, cidx[d], :, _pair_lanes(p)], s_) for (d, p), s_ in zip(CHAINS, s)]
        yield
        vn = [u_ref[d, cidx[d], :, _pair_lanes(p)] - x[:CHUNK] for (d, p), x in zip(CHAINS, wq)]
        pv = [_mm(p_ref[d, cidx[d], p], _stack(v, m0)) for (d, p), v in zip(CHAINS, vn)]
        kv = [_mm_tn(kd_ref[d, cidx[d], :, _pair_lanes(p)], v) for (d, p), v in zip(CHAINS, vn)]
        yield
        for i, (d, p) in enumerate(CHAINS):
            o_refs[d][_chunk_rows(cidx[d]), _pair_lanes(p)] = (wq[i][CHUNK:] + pv[i]).astype(o_refs[d].dtype)
            egl = egl_ref[d, cidx[d], :, _pair_lanes(p)]
            s_ref[d, p] = egl * s[i] + jnp.where(mks[d].same, kv[i], 0.0)


def _gdn_kernel(first_ref, last_ref, xf_ref, xb_ref, gf_ref, gb_ref, gtf_ref, gtb_ref, gp_ref, gpt_ref,
                of_ref, ob_ref, s_ref, u_ref, wq_ref, kd_ref, p_ref, egl_ref):
    i = pl.program_id(0)
    nb = pl.num_programs(0)
    n_chunks = xf_ref.shape[0] // CHUNK

    @pl.when(first_ref[i] == 1)
    def _():
        s_ref[0] = jnp.zeros_like(s_ref[0])

    @pl.when(last_ref[nb - 1 - i] == 1)
    def _():
        s_ref[1] = jnp.zeros_like(s_ref[1])

    m0 = _lane_half()
    mks = [_Masks(backward=False), _Masks(backward=True)]
    cs = [_cumsum_matrix(backward=False), _cumsum_matrix(backward=True)]
    rcs = [_row_cumsum_matrix(backward=False), _row_cumsum_matrix(backward=True)]
    scratch = (u_ref, wq_ref, kd_ref, p_ref, egl_ref)

    def local(k):
        return lambda tick: _gdn_local(mks, m0, (xf_ref, xb_ref), (gf_ref, gb_ref), (gtf_ref, gtb_ref), gp_ref,
                                       gpt_ref, cs, rcs, _part_chunks(n_chunks, k, GDN_PART_CHUNKS), *scratch,
                                       tick=tick)

    def seq(k):
        return _gdn_seq(mks, m0, _part_chunks(n_chunks, k, GDN_PART_CHUNKS), (of_ref, ob_ref), s_ref, *scratch)

    _run_parts(n_chunks // GDN_PART_CHUNKS, local, seq)


def _rec_specs(nb, widths):
    fwd = lambda w: pl.BlockSpec((REC_BLOCK, w), lambda i, f, l: (i, 0))
    bwd = lambda w: pl.BlockSpec((REC_BLOCK, w), lambda i, f, l: (nb - 1 - i, 0))
    const = lambda r: pl.BlockSpec((r, GATE_W), lambda i, f, l: (0, 0))
    ins = [fwd(widths[0]), bwd(widths[0]), fwd(GATE_W), bwd(GATE_W),
           pl.BlockSpec((GATE_W, REC_BLOCK), lambda i, f, l: (0, i)),
           pl.BlockSpec((GATE_W, REC_BLOCK), lambda i, f, l: (0, nb - 1 - i)),
           const(8), const(GATE_W)]
    return ins, [fwd(widths[1]), bwd(widths[1])]


def _gdn(gqkv, gates, gates_t, gate_params, tables):
    t = gqkv.shape[0]
    nb = t // REC_BLOCK
    nc = REC_BLOCK // CHUNK
    first, last = tables[0], tables[1]
    in_specs, out_specs = _rec_specs(nb, (3 * GDN_W, GDN_W))
    grid_spec = pltpu.PrefetchScalarGridSpec(
        num_scalar_prefetch=2,
        grid=(nb,),
        in_specs=in_specs,
        out_specs=out_specs,
        scratch_shapes=[pltpu.VMEM((2, N_PAIRS, STK, STK), F32),
                        pltpu.VMEM((2, nc, CHUNK, GDN_W), F32),
                        pltpu.VMEM((2, nc, STK, GDN_W), BF16),
                        pltpu.VMEM((2, nc, CHUNK, GDN_W), BF16),
                        pltpu.VMEM((2, nc, N_PAIRS, CHUNK, STK), BF16),
                        pltpu.VMEM((2, nc, 1, GDN_W), F32)],
    )
    return pl.pallas_call(
        _gdn_kernel,
        grid_spec=grid_spec,
        out_shape=[jax.ShapeDtypeStruct((t, GDN_W), BF16)] * 2,
        compiler_params=_params("arbitrary"),
        name="gdn",
    )(first, last, gqkv, gqkv, gates, gates, gates_t, gates_t, gate_params[0], gate_params[1])


def _mlstm_local(mks, m0, bd2, x_refs, g_refs, gt_refs, gp_ref, gpt_ref, cs, rcs, chunk_ids,
                 upd_ref, intra_ref, mi_ref, fc_ref, fl_ref, ml_ref, tick=_no_tick):
    bias_row = gp_ref[0:1, :]
    bias_col = gpt_ref[0:GATE_ROWS, 0:1]
    tiles, rows = {}, {}
    for d in DIRS:
        for j, c in enumerate(chunk_ids[d]):
            li = g_refs[d][_chunk_rows(c), :] + bias_row
            tiles[d, j] = (_split_mm_left(cs[d], -_softplus(-li), 3), li)
            if (d, c // 2) not in rows:
                lit = gt_refs[d][0:GATE_ROWS, _group_lanes(c)] + bias_col
                fct = _split_mm(-_softplus(-lit), rcs[d], 3)
                rows[d, c // 2] = (fct, pltpu.roll(fct, CHUNK, 1), lit, pltpu.roll(lit, CHUNK, 1))
    tick()
    order = [(d, j, p) for j in range(len(chunk_ids[0])) for d in DIRS for p in range(N_PAIRS)]
    ch = [(d, chunk_ids[d][j], p) for d, j, p in order]
    mk = [mks[d] for d, j, p in order]
    col_i = [2 * GATE_GROUP + d * MLSTM_HEADS + 2 * p for d, j, p in order]
    col_f = [3 * GATE_GROUP + d * MLSTM_HEADS + 2 * p for d, j, p in order]
    fc = [tiles[d, j][0] for d, j, p in order]
    li = [tiles[d, j][1] for d, j, p in order]
    e_row = [_pair_row(rows[d, c // 2][0], rows[d, c // 2][1], cf, cf + 1, c)
             - _pair_row(rows[d, c // 2][2], rows[d, c // 2][3], ci, ci + 1, c)
             for (d, c, p), cf, ci in zip(ch, col_f, col_i)]
    q2 = [x_refs[d][_chunk_rows(c), _pair_lanes(p, 0, MLSTM_W)] for d, c, p in ch]
    k2 = [x_refs[d][_chunk_rows(c), _pair_lanes(p, 1, MLSTM_W)] * (HEAD_DIM ** -0.5) for d, c, p in ch]
    v2 = [x_refs[d][_chunk_rows(c), _pair_lanes(p, 2, MLSTM_W)] for d, c, p in ch]
    ones = jnp.ones((CHUNK, PAIR_W), F32)
    ones_st = _stack(ones, m0)

    fc2 = _each(lambda t, c0: _expand(t, c0, c0 + 1, m0), fc, col_f)
    li2 = _each(lambda t, c0: _expand(t, c0, c0 + 1, m0), li, col_i)
    fl = _each(lambda f, m: f[m.last_row:m.last_row + 1, :], fc2, mk)
    g_end = _each(lambda l, f, i_: l - f + i_, fl, fc2, li2)
    ml = _each(lambda g: jnp.max(g, axis=0, keepdims=True), g_end)
    kw2 = _each(lambda k, g, m: k * jnp.exp(g - m), k2, g_end, ml)
    upd = _each(lambda kw, v: jnp.where(bd2, _mm_tn(kw, jnp.concatenate([v, ones], axis=1)), 0.0), kw2, v2)
    tick()
    dmat = _each(lambda f, e, m: jnp.where(m.incl, f - e, NEG), fc2, e_row, mk)
    mi2 = _each(lambda x: jnp.where(m0, jnp.max(jnp.where(m0, x, NEG), axis=1, keepdims=True),
                                    jnp.max(jnp.where(m0, NEG, x), axis=1, keepdims=True)), dmat)
    tick()
    qk = _each(lambda q, k: _mm_nt(q, _stack(k, m0)), q2, k2)
    tick()
    s_loc = _each(lambda x, dm, mi: x * jnp.exp(dm - mi), qk, dmat, mi2)
    tick()
    intra = _each(lambda s, v: _mm(s, jnp.concatenate([_stack(v, m0), ones_st], axis=1)), s_loc, v2)
    tick()
    for i, (d, c, p) in enumerate(ch):
        lanes = _pair_lanes(p)
        upd_ref[d, c, p] = upd[i]
        intra_ref[d, c, p] = intra[i]
        mi_ref[d, c, :, lanes] = mi2[i]
        fc_ref[d, c, :, lanes] = fc2[i]
        fl_ref[d, c, :, lanes] = fl[i]
        ml_ref[d, c, :, lanes] = ml[i]
        if i % N_PAIRS == N_PAIRS - 1:
            tick()


def _mlstm_seq(chunk_ids, x_refs, o_refs, c_ref, m_ref, upd_ref, intra_ref, mi_ref, fc_ref, fl_ref, ml_ref):
    for j, d, p in [(j, d, p) for j in range(len(chunk_ids[0])) for d, p in CHAINS]:
        c = chunk_ids[d][j]
        lanes = _pair_lanes(p)
        m_st = m_ref[d, p]
        cn = c_ref[d, p]
        fl = fl_ref[d, c, :, lanes]
        ml = ml_ref[d, c, :, lanes]
        m_new = jnp.maximum(fl + m_st, ml)
        a_row = jnp.exp(fl + m_st - m_new)
        b_row = jnp.exp(ml - m_new)
        m_inter = fc_ref[d, c, :, lanes] + m_st
        mi2 = mi_ref[d, c, :, lanes]
        m_t = jnp.maximum(m_inter, mi2)
        w_intra = jnp.exp(mi2 - m_t)
        w_inter = jnp.exp(m_inter - m_t)
        inter = _mm(x_refs[d][_chunk_rows(c), _pair_lanes(p, 0, MLSTM_W)], cn)
        intra = intra_ref[d, c, p]
        num = w_intra * intra[:, :PAIR_W] + w_inter * inter[:, :PAIR_W]
        den = w_intra * intra[:, PAIR_W:] + w_inter * inter[:, PAIR_W:]
        out = num / jnp.maximum(jnp.abs(den), jnp.exp(-m_t))
        o_refs[d][_chunk_rows(c), lanes] = out.astype(o_refs[d].dtype)
        c_ref[d, p] = (jnp.concatenate([a_row, a_row], axis=1) * cn
                       + jnp.concatenate([b_row, b_row], axis=1) * upd_ref[d, c, p])
        m_ref[d, p] = m_new
        if p == N_PAIRS - 1:
            yield


def _mlstm_kernel(first_ref, last_ref, xf_ref, xb_ref, gf_ref, gb_ref, gtf_ref, gtb_ref, gp_ref, gpt_ref,
                  of_ref, ob_ref, c_ref, m_ref, upd_ref, intra_ref, mi_ref, fc_ref, fl_ref, ml_ref):
    i = pl.program_id(0)
    nb = pl.num_programs(0)
    n_chunks = xf_ref.shape[0] // CHUNK

    @pl.when(first_ref[i] == 1)
    def _():
        c_ref[0] = jnp.zeros_like(c_ref[0])
        m_ref[0] = jnp.full_like(m_ref[0], M_INIT)

    @pl.when(last_ref[nb - 1 - i] == 1)
    def _():
        c_ref[1] = jnp.zeros_like(c_ref[1])
        m_ref[1] = jnp.full_like(m_ref[1], M_INIT)

    m0 = _lane_half()
    r = lax.broadcasted_iota(jnp.int32, (STK, 2 * STK), 0) // HEAD_DIM
    c = (lax.broadcasted_iota(jnp.int32, (STK, 2 * STK), 1) % STK) // HEAD_DIM
    bd2 = r == c
    mks = [_Masks(backward=False), _Masks(backward=True)]
    cs = [_cumsum_matrix(backward=False), _cumsum_matrix(backward=True)]
    rcs = [_row_cumsum_matrix(backward=False), _row_cumsum_matrix(backward=True)]
    scratch = (upd_ref, intra_ref, mi_ref, fc_ref, fl_ref, ml_ref)

    def local(k):
        return lambda tick: _mlstm_local(mks, m0, bd2, (xf_ref, xb_ref), (gf_ref, gb_ref), (gtf_ref, gtb_ref),
                                         gp_ref, gpt_ref, cs, rcs, _part_chunks(n_chunks, k, MLSTM_PART_CHUNKS),
                                         *scratch, tick=tick)

    def seq(k):
        return _mlstm_seq(_part_chunks(n_chunks, k, MLSTM_PART_CHUNKS), (xf_ref, xb_ref), (of_ref, ob_ref), c_ref,
                          m_ref, *scratch)

    _run_parts(n_chunks // MLSTM_PART_CHUNKS, local, seq)


def _mlstm(ml_qkv, gates, gates_t, gate_params, tables):
    t = ml_qkv.shape[0]
    nb = t // REC_BLOCK
    nc = REC_BLOCK // CHUNK
    first, last = tables[0], tables[1]
    in_specs, out_specs = _rec_specs(nb, (3 * MLSTM_W, MLSTM_W))
    grid_spec = pltpu.PrefetchScalarGridSpec(
        num_scalar_prefetch=2,
        grid=(nb,),
        in_specs=in_specs,
        out_specs=out_specs,
        scratch_shapes=[pltpu.VMEM((2, N_PAIRS, STK, 2 * STK), F32),
                        pltpu.VMEM((2, N_PAIRS, 1, PAIR_W), F32),
                        pltpu.VMEM((2, nc, N_PAIRS, STK, 2 * STK), F32),
                        pltpu.VMEM((2, nc, N_PAIRS, CHUNK, 2 * PAIR_W), F32),
                        pltpu.VMEM((2, nc, CHUNK, MLSTM_W), F32),
                        pltpu.VMEM((2, nc, CHUNK, MLSTM_W), F32),
                        pltpu.VMEM((2, nc, 1, MLSTM_W), F32),
                        pltpu.VMEM((2, nc, 1, MLSTM_W), F32)],
    )
    return pl.pallas_call(
        _mlstm_kernel,
        grid_spec=grid_spec,
        out_shape=[jax.ShapeDtypeStruct((t, MLSTM_W), BF16)] * 2,
        compiler_params=_params("arbitrary"),
        name="mlstm",
    )(first, last, ml_qkv, ml_qkv, gates, gates, gates_t, gates_t, gate_params[0], gate_params[1])


OUT_TM = 512


OUT_SLOTS = 3


def _outproj_kernel(x_hbm, na_ref, gf_ref, gb_ref, z_ref, hf_ref, hb_ref, o_ref, gn_ref, mn_ref,
                    w_ref, y_ref, xbuf, sem, *, n_steps):
    i = pl.program_id(0)
    tm = y_ref.shape[0]

    def x_copy(step, slot):
        rows = pl.ds(pl.multiple_of(step * tm, tm), tm)
        return pltpu.make_async_copy(x_hbm.at[rows, :], xbuf.at[slot], sem.at[slot])

    @pl.when(i == 0)
    def _():
        for s in range(min(OUT_SLOTS - 1, n_steps)):
            x_copy(s, s).start()

    ahead = i + (OUT_SLOTS - 1)

    @pl.when(ahead < n_steps)
    def _():
        x_copy(ahead, lax.rem(ahead, OUT_SLOTS)).start()

    slot = lax.rem(i, OUT_SLOTS)
    x_copy(i, slot).wait()

    def head_rms(v, gain):
        ms = _head_sumsq(v) * (1.0 / HEAD_DIM)
        return v * lax.rsqrt(ms + EPS) * gain

    go = head_rms(gf_ref[...].astype(F32) + gb_ref[...].astype(F32), gn_ref[...])
    z = z_ref[...].astype(F32)
    gdn_out = go * (z * _sigmoid(z))
    mh = head_rms(hf_ref[...].astype(F32) + hb_ref[...].astype(F32), mn_ref[...])
    ml_out = _sigmoid(o_ref[...].astype(F32)) * mh
    y = xbuf[slot]
    y = y + jnp.dot(na_ref[...], w_ref[0:NA_W, :], preferred_element_type=F32)
    y = y + jnp.dot(_bf(gdn_out), w_ref[NA_W:NA_W + GDN_W, :], preferred_element_type=F32)
    y = y + jnp.dot(_bf(ml_out), w_ref[NA_W + GDN_W:, :], preferred_element_type=F32)
    y_ref[...] = y


def _outproj(x, na_out, gf, gb, z, hf, hb, o, gdn_norm, mlstm_norm, w_out):
    t = x.shape[0]
    tm = min(OUT_TM, t)
    tok = lambda w: pl.BlockSpec((tm, w), lambda i: (i, 0))
    const = lambda r, w: pl.BlockSpec((r, w), lambda i: (0, 0))
    gn = jnp.tile(gdn_norm.astype(F32), GDN_HEADS).reshape(1, GDN_W)
    mn = mlstm_norm.astype(F32).reshape(1, MLSTM_W)
    return pl.pallas_call(
        functools.partial(_outproj_kernel, n_steps=t // tm),
        grid=(t // tm,),
        in_specs=[pl.BlockSpec(memory_space=pl.ANY), tok(NA_W), tok(GDN_W), tok(GDN_W), tok(GDN_W), tok(MLSTM_W),
                  tok(MLSTM_W), tok(MLSTM_W), const(1, GDN_W), const(1, MLSTM_W), const(D_MODEL, D_MODEL)],
        out_specs=tok(D_MODEL),
        out_shape=jax.ShapeDtypeStruct((t, D_MODEL), F32),
        scratch_shapes=[pltpu.VMEM((OUT_SLOTS, tm, D_MODEL), F32), pltpu.SemaphoreType.DMA((OUT_SLOTS,))],
        compiler_params=_params("arbitrary"),
        name="outproj",
    )(x, na_out, gf, gb, z, hf, hb, o, gn, mn, _bf(w_out))


def _gate_params(bias_rows, a_log):
    pad = GATE_W - 4 * GATE_GROUP
    row0 = jnp.concatenate([b.astype(F32).reshape(-1) for b in bias_rows] + [jnp.zeros((pad,), F32)])
    row1 = jnp.concatenate([a_log.astype(F32).reshape(-1), jnp.zeros((GATE_W - GATE_GROUP,), F32)])
    rows = jnp.concatenate([row0[None], row1[None], jnp.zeros((6, GATE_W), F32)], axis=0)
    cols = jnp.concatenate([rows.T, jnp.zeros((GATE_W, GATE_W - 8), F32)], axis=1)
    return rows, cols


def _trunk(xs, seq_lens, p):
    depth = p["ffn1_norm"].shape[0]
    tables = tuple(jnp.asarray(a) for a in _block_tables(seq_lens, MIX_BLOCK))
    rec_tables = tuple(jnp.asarray(a) for a in _block_tables(seq_lens, REC_BLOCK))
    zeros_g = jnp.zeros((2, GDN_HEADS), F32)
    out_sizes = [x.shape[0] for x in xs]
    x = None
    for i in range(depth):
        x = _ffn(xs if i == 0 else [x], p["ffn1_norm"][i], _bf(p["ffn1_w_in"][i]), _bf(p["ffn1_w_out"][i]),
                 p["final_norm"], False)
        na_qkv, gqkv, gdn_z, ml_qkv, ml_o, gates = _inproj(x, p["mix_norm"][i], _permute_w_in(p["w_in"][i]),
                                                           p["gdn_conv"][i], tables)
        na_out = _na(na_qkv, _na_bias_table(p["na_rpb"][i]), tables)
        gdn_gp = _gate_params([p["gdn_dt_bias"][i], zeros_g, zeros_g, zeros_g], p["gdn_a_log"][i])
        gates_t = gates.T
        gf, gb = _gdn(gqkv, gates, gates_t, gdn_gp, rec_tables)
        ml_gp = _gate_params([zeros_g, zeros_g, p["mlstm_i_bias"][i], p["mlstm_f_bias"][i]], zeros_g)
        hf, hb = _mlstm(ml_qkv, gates, gates_t, ml_gp, rec_tables)
        x = _outproj(x, na_out, gf, gb, gdn_z, hf, hb, ml_o, p["gdn_norm"][i], p["mlstm_norm"][i], p["w_out"][i])
        last = i == depth - 1
        x = _ffn([x], p["ffn2_norm"][i], _bf(p["ffn2_w_in"][i]), _bf(p["ffn2_w_out"][i]), p["final_norm"],
                 last, out_sizes if last else None)
    return x if len(out_sizes) > 1 else [x]


def kernel(x_prompt, x_sample, ffn1_norm, ffn1_w_in, ffn1_w_out, mix_norm, w_in, w_out, na_rpb, gdn_conv,
           gdn_a_log, gdn_dt_bias, gdn_norm, mlstm_i_bias, mlstm_f_bias, mlstm_norm, ffn2_norm, ffn2_w_in,
           ffn2_w_out, final_norm):
    p = dict(ffn1_norm=ffn1_norm, ffn1_w_in=ffn1_w_in, ffn1_w_out=ffn1_w_out, mix_norm=mix_norm, w_in=w_in,
             w_out=w_out, na_rpb=na_rpb, gdn_conv=gdn_conv, gdn_a_log=gdn_a_log, gdn_dt_bias=gdn_dt_bias,
             gdn_norm=gdn_norm, mlstm_i_bias=mlstm_i_bias, mlstm_f_bias=mlstm_f_bias, mlstm_norm=mlstm_norm,
             ffn2_norm=ffn2_norm, ffn2_w_in=ffn2_w_in, ffn2_w_out=ffn2_w_out, final_norm=final_norm)
    bp, lp, _ = x_prompt.shape
    bs, ls, _ = x_sample.shape
    y_prompt, y_sample = _trunk([x_prompt.reshape(bp * lp, D_MODEL), x_sample.reshape(bs * ls, D_MODEL)],
                                (lp,) * bp + (ls,) * bs, p)
    return (y_prompt.reshape(bp, lp, D_MODEL), y_sample.reshape(bs, ls, D_MODEL))
```
